```python
import jax
import jax.numpy as jnp
from jax import lax
import numpy as np

D_MODEL = 4096
BATCH = 1
SEQ = 8192
DEPTH = 2
DEC_BATCH = 8
DEC_SEQ = 16
PAST_LEN = 4096

CHUNK = 64
N_MIXERS = 2
N_S5_LAYERS = (DEPTH + 1) // 2
N_GLA_LAYERS = DEPTH // 2
EPS = 1e-6
S5_GROUP = 16
S5_GROUPS = D_MODEL // S5_GROUP
S5_STATE = 64
S5_DT_MIN = 1e-3
S5_DT_MAX = 1e-1
GLA_HEADS = 8
GLA_DK = (D_MODEL // 2) // GLA_HEADS
GLA_DV = D_MODEL // GLA_HEADS
GLA_GATE_RANK = 16
GLA_TAU = 16.0
N_EXPERTS = 64
F_EXPERT = 512
F_SHARED = 512
TOP_K = 8
N_GROUPS = 8
TOPK_GROUPS = 4
ROUTED_SCALE = 2.5
MOE_BLOCK = 64

kernel_name = "hybrid_s5_gla_moe_stream_step"

F32 = jnp.float32


def rmsnorm(x, g):
    xf = x.astype(F32)
    y = xf * lax.rsqrt(jnp.mean(xf * xf, axis=-1, keepdims=True) + EPS)
    return (y * g.astype(F32)).astype(x.dtype)


def modulate(h, shift, scale):
    return h * (1.0 + scale[:, None, :]) + shift[:, None, :]


def chunked_scan(step, carry, xs):
    L = xs[0].shape[1]
    if L <= CHUNK:
        return step(carry, xs)
    n = L // CHUNK
    xs_c = tuple(a.reshape(a.shape[0], n, CHUNK, *a.shape[2:]).swapaxes(0, 1) for a in xs)
    carry, ys = lax.scan(step, carry, xs_c)
    ys = ys.swapaxes(0, 1)
    return carry, ys.reshape(ys.shape[0], L, *ys.shape[3:])


def s5_discretize(lam_re, lam_im, log_dt, b_re, b_im):
    lr = lam_re.astype(F32)
    li = lam_im.astype(F32)
    dt = jnp.exp(log_dt.astype(F32))[:, None]
    mag = jnp.exp(lr * dt)
    ar = mag * jnp.cos(li * dt)
    ai = mag * jnp.sin(li * dt)
    den = lr * lr + li * li
    fr = ((ar - 1.0) * lr + ai * li) / den
    fi = (ai * lr - (ar - 1.0) * li) / den
    br = b_re.astype(F32)
    bi = b_im.astype(F32)
    bbr = fr[..., None] * br - fi[..., None] * bi
    bbi = fr[..., None] * bi + fi[..., None] * br
    return ar, ai, bbr, bbi


def _complex_affine_combine(e1, e2):
    a1r, a1i, b1r, b1i = e1
    a2r, a2i, b2r, b2i = e2
    return (a2r * a1r - a2i * a1i,
            a2r * a1i + a2i * a1r,
            a2r * b1r - a2i * b1i + b2r,
            a2r * b1i + a2i * b1r + b2i)


def s5_chunk(s_re, s_im, u, ar, ai, bbr, bbi, cr, ci, d):
    bsz, L, _ = u.shape
    uf = u.astype(F32)
    ug = uf.reshape(bsz, L, S5_GROUPS, S5_GROUP)
    bu_r = jnp.einsum('blgh,gph->blgp', ug, bbr)
    bu_i = jnp.einsum('blgh,gph->blgp', ug, bbi)
    a_r = jnp.broadcast_to(ar, bu_r.shape)
    a_i = jnp.broadcast_to(ai, bu_i.shape)
    p_r, p_i, x_r, x_i = lax.associative_scan(_complex_affine_combine, (a_r, a_i, bu_r, bu_i), axis=1)
    s_r0 = s_re[:, None]
    s_i0 = s_im[:, None]
    x_r = x_r + p_r * s_r0 - p_i * s_i0
    x_i = x_i + p_r * s_i0 + p_i * s_r0
    y = jnp.einsum('blgp,ghp->blgh', x_r, cr) - jnp.einsum('blgp,ghp->blgh', x_i, ci)
    y = y.reshape(bsz, L, D_MODEL) + d * uf
    return y, x_r[:, -1], x_i[:, -1]


def s5_mixer(h, s_re, s_im, lam_re, lam_im, log_dt, b_re, b_im, c_re, c_im, d_skip, w_glu, b_glu):
    ar, ai, bbr, bbi = s5_discretize(lam_re, lam_im, log_dt, b_re, b_im)
    cr = c_re.astype(F32)
    ci = c_im.astype(F32)
    d = d_skip.astype(F32)

    def step(carry, xs):
        (u,) = xs
        y, sr, si = s5_chunk(carry[0], carry[1], u, ar, ai, bbr, bbi, cr, ci, d)
        return (sr, si), y

    (s_re, s_im), y = chunked_scan(step, (s_re.astype(F32), s_im.astype(F32)), (h,))
    z = jax.nn.gelu(y).astype(h.dtype)
    out = z * jax.nn.sigmoid(z @ w_glu + b_glu)
    return out, s_re, s_im


def gla_chunk(S, q, k, v, lg):
    L = q.shape[1]
    b = jnp.cumsum(lg, axis=1)
    causal = jnp.tril(jnp.ones((L, L), dtype=bool))
    diff = b[:, :, None] - b[:, None, :]
    decay = jnp.exp(jnp.where(causal[None, :, :, None, None], diff, -jnp.inf))
    att = jnp.einsum('bthk,bshk,btshk->bhts', q, k, decay)
    o = (jnp.einsum('bhts,bshv->bthv', att, v)
         + jnp.einsum('bthk,bhkv->bthv', q * jnp.exp(b), S))
    b_last = b[:, -1]
    k_dec = k * jnp.exp(b_last[:, None] - b)
    S_new = jnp.exp(b_last)[..., None] * S + jnp.einsum('bshk,bshv->bhkv', k_dec, v)
    return o, S_new


def gla_mixer(h, S0, w_in, w_a1, w_a2, b_a, g_norm, w_o):
    bsz, L, _ = h.shape
    dk_tot = GLA_HEADS * GLA_DK
    proj = h @ w_in
    q, k, v, g = jnp.split(proj, [dk_tot, 2 * dk_tot, 2 * dk_tot + D_MODEL], axis=-1)
    lg = jax.nn.log_sigmoid(((h @ w_a1) @ w_a2 + b_a).astype(F32)) / GLA_TAU
    q = q.astype(F32).reshape(bsz, L, GLA_HEADS, GLA_DK) * (GLA_DK ** -0.5)
    k = k.astype(F32).reshape(bsz, L, GLA_HEADS, GLA_DK)
    v = v.astype(F32).reshape(bsz, L, GLA_HEADS, GLA_DV)
    lg = lg.reshape(bsz, L, GLA_HEADS, GLA_DK)

    def step(S, xs):
        o, S_new = gla_chunk(S, *xs)
        return S_new, o

    S, o = chunked_scan(step, S0.astype(F32), (q, k, v, lg))
    o = o * lax.rsqrt(jnp.mean(o * o, axis=-1, keepdims=True) + EPS) * g_norm.astype(F32)
    o = o.reshape(bsz, L, D_MODEL).astype(h.dtype) * jax.nn.silu(g)
    return o @ w_o, S


def route(h2, w_router, b_router):
    n = h2.shape[0]
    per = N_EXPERTS // N_GROUPS
    scores = jax.nn.sigmoid(h2.astype(F32) @ w_router.astype(F32))
    choice = scores + b_router.astype(F32)
    gscore = lax.top_k(choice.reshape(n, N_GROUPS, per), 2)[0].sum(-1)
    _, gidx = lax.top_k(gscore, TOPK_GROUPS)
    gmask = jnp.any(gidx[:, :, None] == jnp.arange(N_GROUPS)[None, None, :], axis=1)
    emask = jnp.repeat(gmask, per, axis=1)
    _, idx = lax.top_k(jnp.where(emask, choice, -jnp.inf), TOP_K)
    w = jnp.take_along_axis(scores, idx, axis=1)
    w = w / jnp.sum(w, axis=-1, keepdims=True) * ROUTED_SCALE
    return idx, w


def routed_experts(h2, idx, wts, w_in, w_out, layer):
    n = h2.shape[0]
    a = n * TOP_K
    flat_e = idx.reshape(a)
    order = jnp.argsort(flat_e)
    e_s = flat_e[order]
    tok_s = (order // TOP_K).astype(jnp.int32)
    w_s = wts.reshape(a)[order]
    counts = jnp.bincount(flat_e, length=N_EXPERTS)
    padded = (counts + MOE_BLOCK - 1) // MOE_BLOCK * MOE_BLOCK
    pad_end = jnp.cumsum(padded)
    pad_start = pad_end - padded
    start = jnp.cumsum(counts) - counts
    dest = pad_start[e_s] + jnp.arange(a) - start[e_s]
    n_blocks = a // MOE_BLOCK + N_EXPERTS
    rows_total = n_blocks * MOE_BLOCK
    row_tok = jnp.zeros((rows_total,), jnp.int32).at[dest].set(tok_s)
    row_w = jnp.zeros((rows_total,), wts.dtype).at[dest].set(w_s)
    blk_start = jnp.arange(n_blocks, dtype=pad_end.dtype) * MOE_BLOCK
    blk_e = jnp.minimum(jnp.searchsorted(pad_end, blk_start, side='right'), N_EXPERTS - 1)

    def body(out, blk):
        rows, rw, e = blk
        x = h2[rows]
        gt, up = jnp.split(x @ w_in[layer, e], 2, axis=-1)
        y = (jax.nn.silu(gt) * up) @ w_out[layer, e]
        return out.at[rows].add(y * rw[:, None].astype(y.dtype)), None

    out, _ = lax.scan(body, jnp.zeros_like(h2),
                      (row_tok.reshape(n_blocks, MOE_BLOCK), row_w.reshape(n_blocks, MOE_BLOCK), blk_e))
    return out


def moe(h, layer, w_router, b_router, w_in, w_out, ws_in, ws_out):
    bsz, L, _ = h.shape
    h2 = h.reshape(bsz * L, D_MODEL)
    idx, wts = route(h2, w_router[layer], b_router[layer])
    routed = routed_experts(h2, idx, wts.astype(h2.dtype), w_in, w_out, layer)
    gt, up = jnp.split(h2 @ ws_in[layer], 2, axis=-1)
    shared = (jax.nn.silu(gt) * up) @ ws_out[layer]
    return (routed + shared).reshape(bsz, L, D_MODEL)


def trunk(x, c, s5_re, s5_im, gla_s, p):
    new_re, new_im, new_gla = [], [], []
    for i in range(DEPTH):
        mod = jax.nn.silu(c) @ p['w_ada'][i] + p['b_ada'][i]
        sh1, sc1, g1, sh2, sc2, g2 = jnp.split(mod, 6, axis=-1)
        h = modulate(rmsnorm(x, p['norm_g'][i, 0]), sh1, sc1)
        j = i // N_MIXERS
        if i % N_MIXERS == 0:
            o, sr, si = s5_mixer(h, s5_re[j], s5_im[j], p['s5_lam_re'][j], p['s5_lam_im'][j],
                                 p['s5_log_dt'][j], p['s5_b_re'][j], p['s5_b_im'][j],
                                 p['s5_c_re'][j], p['s5_c_im'][j], p['s5_d'][j],
                                 p['s5_w_glu'][j], p['s5_b_glu'][j])
            new_re.append(sr)
            new_im.append(si)
        else:
            o, S = gla_mixer(h, gla_s[j], p['gla_w_in'][j], p['gla_w_a1'][j], p['gla_w_a2'][j],
                             p['gla_b_a'][j], p['gla_g_norm'][j], p['gla_w_o'][j])
            new_gla.append(S)
        x = x + g1[:, None, :] * o
        h = modulate(rmsnorm(x, p['norm_g'][i, 1]), sh2, sc2)
        x = x + g2[:, None, :] * moe(h, i, p['moe_w_router'], p['moe_b_router'], p['moe_w_in'],
                                     p['moe_w_out'], p['moe_ws_in'], p['moe_ws_out'])
    y = rmsnorm(x, p['final_g'])
    return y, jnp.stack(new_re), jnp.stack(new_im), jnp.stack(new_gla)


def setup_inputs(seed: int = 0) -> dict:
    key = jax.random.key(seed)
    ks = iter(jax.random.split(key, 40))

    def nrm(shape, scale):
        return jax.random.normal(next(ks), shape, F32) * scale

    D = D_MODEL
    G, P, H = S5_GROUPS, S5_STATE, S5_GROUP
    lam_im = jnp.broadcast_to(jnp.pi * jnp.arange(P, dtype=F32), (N_S5_LAYERS, G, P))
    log_dt = jax.random.uniform(next(ks), (N_S5_LAYERS, G), F32,
                                minval=np.log(S5_DT_MIN), maxval=np.log(S5_DT_MAX))
    return {
        "x_prompt": nrm((BATCH, SEQ, D), 1.0),
        "x_sample": nrm((DEC_BATCH, DEC_SEQ, D), 1.0),
        "state_s5_re": nrm((N_S5_LAYERS, DEC_BATCH, G, P), 0.1),
        "state_s5_im": nrm((N_S5_LAYERS, DEC_BATCH, G, P), 0.1),
        "state_gla": nrm((N_GLA_LAYERS, DEC_BATCH, GLA_HEADS, GLA_DK, GLA_DV), 1.0),
        "c_prompt": nrm((BATCH, D), 1.0),
        "c_sample": nrm((DEC_BATCH, D), 1.0),
        "w_ada": nrm((DEPTH, D, 6 * D), 0.5 * D ** -0.5),
        "b_ada": nrm((DEPTH, 6 * D), 0.02),
        "norm_g": 1.0 + nrm((DEPTH, 2, D), 0.02),
        "s5_lam_re": -0.5 + nrm((N_S5_LAYERS, G, P), 0.01),
        "s5_lam_im": lam_im + nrm((N_S5_LAYERS, G, P), 0.01),
        "s5_log_dt": log_dt,
        "s5_b_re": nrm((N_S5_LAYERS, G, P, H), (2 * H) ** -0.5),
        "s5_b_im": nrm((N_S5_LAYERS, G, P, H), (2 * H) ** -0.5),
        "s5_c_re": nrm((N_S5_LAYERS, G, H, P), P ** -0.5),
        "s5_c_im": nrm((N_S5_LAYERS, G, H, P), P ** -0.5),
        "s5_d": nrm((N_S5_LAYERS, D), 1.0),
        "s5_w_glu": nrm((N_S5_LAYERS, D, D), D ** -0.5),
        "s5_b_glu": nrm((N_S5_LAYERS, D), 0.02),
        "gla_w_in": nrm((N_GLA_LAYERS, D, 3 * D), D ** -0.5),
        "gla_w_a1": nrm((N_GLA_LAYERS, D, GLA_GATE_RANK), D ** -0.5),
        "gla_w_a2": nrm((N_GLA_LAYERS, GLA_GATE_RANK, GLA_HEADS * GLA_DK), GLA_GATE_RANK ** -0.5),
        "gla_b_a": nrm((N_GLA_LAYERS, GLA_HEADS * GLA_DK), 0.1),
        "gla_g_norm": 1.0 + nrm((N_GLA_LAYERS, GLA_DV), 0.02),
        "gla_w_o": nrm((N_GLA_LAYERS, D, D), D ** -0.5),
        "moe_w_router": nrm((DEPTH, D, N_EXPERTS), D ** -0.5),
        "moe_b_router": nrm((DEPTH, N_EXPERTS), 0.01),
        "moe_w_in": nrm((DEPTH, N_EXPERTS, D, 2 * F_EXPERT), D ** -0.5),
        "moe_w_out": nrm((DEPTH, N_EXPERTS, F_EXPERT, D), F_EXPERT ** -0.5),
        "moe_ws_in": nrm((DEPTH, D, 2 * F_SHARED), D ** -0.5),
        "moe_ws_out": nrm((DEPTH, F_SHARED, D), F_SHARED ** -0.5),
        "final_g": 1.0 + nrm((D,), 0.02),
    }


def reference(x_prompt, x_sample, state_s5_re, state_s5_im, state_gla, c_prompt, c_sample,
              w_ada, b_ada, norm_g, s5_lam_re, s5_lam_im, s5_log_dt, s5_b_re, s5_b_im,
              s5_c_re, s5_c_im, s5_d, s5_w_glu, s5_b_glu, gla_w_in, gla_w_a1, gla_w_a2,
              gla_b_a, gla_g_norm, gla_w_o, moe_w_router, moe_b_router, moe_w_in, moe_w_out,
              moe_ws_in, moe_ws_out, final_g):
    p = dict(w_ada=w_ada, b_ada=b_ada, norm_g=norm_g, s5_lam_re=s5_lam_re, s5_lam_im=s5_lam_im,
             s5_log_dt=s5_log_dt, s5_b_re=s5_b_re, s5_b_im=s5_b_im, s5_c_re=s5_c_re,
             s5_c_im=s5_c_im, s5_d=s5_d, s5_w_glu=s5_w_glu, s5_b_glu=s5_b_glu,
             gla_w_in=gla_w_in, gla_w_a1=gla_w_a1, gla_w_a2=gla_w_a2, gla_b_a=gla_b_a,
             gla_g_norm=gla_g_norm, gla_w_o=gla_w_o, moe_w_router=moe_w_router,
             moe_b_router=moe_b_router, moe_w_in=moe_w_in, moe_w_out=moe_w_out,
             moe_ws_in=moe_ws_in, moe_ws_out=moe_ws_out, final_g=final_g)
    bp = x_prompt.shape[0]
    s5_zero = jnp.zeros((N_S5_LAYERS, bp, S5_GROUPS, S5_STATE), F32)
    gla_zero = jnp.zeros((N_GLA_LAYERS, bp, GLA_HEADS, GLA_DK, GLA_DV), F32)
    y_prompt, s5r_p, s5i_p, gla_p = trunk(x_prompt, c_prompt, s5_zero, s5_zero, gla_zero, p)
    y_sample, s5r_s, s5i_s, gla_s = trunk(x_sample, c_sample, state_s5_re, state_s5_im, state_gla, p)
    return (y_prompt, y_sample, s5r_p, s5i_p, gla_p, s5r_s, s5i_s, gla_s)
```

```python
import functools

import numpy as np
import jax
import jax.numpy as jnp
from jax import lax
from jax.experimental import pallas as pl
from jax.experimental.pallas import tpu as pltpu

F32 = jnp.float32
BF16 = jnp.bfloat16
I32 = jnp.int32

EPS = 1e-6
GLA_TAU = 16.0
TOP_K = 8
N_GROUPS = 8
TOPK_GROUPS = 4
ROUTED_SCALE = 2.5

LANES = 128
SUBLANES = 8
MXU_DIM = 256
VMEM_LIMIT = 56 << 20

PROMPT_TILE = 512
S5_KTILE = 256
S5_COLS = 512
EXP_ROWBLK = 128
EXP_NBLK = 9
EXP_TF = 256
EXP_TD = 512
COMB_TILE = 64


def _cparams(n_axes, vmem=VMEM_LIMIT):
    return pltpu.CompilerParams(dimension_semantics=("arbitrary",) * n_axes, vmem_limit_bytes=vmem)


class _Region:
    def __init__(self, row0, rows, tile, nseq, lrep):
        self.row0, self.rows, self.tile, self.nseq, self.lrep = row0, rows, tile, nseq, lrep
        self.blk0 = row0 // tile
        self.ntiles = rows // tile
        assert row0 % tile == 0 and rows % tile == 0


def _expand_rows(m, lrep, rows=None, row0=0):
    nseq, n = m.shape
    if nseq == 1:
        return m
    rows = nseq * lrep if rows is None else rows
    r = lax.broadcasted_iota(I32, (rows, n), 0) + row0
    out = jnp.broadcast_to(m[0:1, :], (rows, n))
    for b in range(1, nseq):
        out = jnp.where(r >= b * lrep, jnp.broadcast_to(m[b:b + 1, :], (rows, n)), out)
    return out


def _sigmoid(x):
    return 1.0 / (1.0 + jnp.exp(-x))


def _silu(x):
    return x * _sigmoid(x)


def _gelu_tanh(x):
    return 0.5 * x * (1.0 + jnp.tanh(0.7978845608028654 * (x + 0.044715 * x * x * x)))


def _log_sigmoid(x):
    return jnp.minimum(x, 0.0) - jnp.log1p(jnp.exp(-jnp.abs(x)))


def _split3(x):
    p1 = x.astype(BF16)
    r1 = x - p1.astype(F32)
    p2 = r1.astype(BF16)
    r2 = r1 - p2.astype(F32)
    return p1, p2, r2.astype(BF16)


def _dot(a, b):
    return jnp.dot(a, b, preferred_element_type=F32)


def _dot_t0(a, b):
    return lax.dot_general(a, b, (((0,), (0,)), ((), ())), preferred_element_type=F32)


def _dot_t1(a, b):
    return lax.dot_general(a, b, (((1,), (1,)), ((), ())), preferred_element_type=F32)


def _ada_kernel(c_ref, w_ref, b_ref, o_ref):
    s = _silu(c_ref[...])
    o_ref[0] = _dot(s.astype(BF16), w_ref[0].astype(BF16)) + b_ref[0]


def _ada_call(c_all, w_ada, b_ada):
    depth, d, n6 = w_ada.shape
    nc = c_all.shape[0]
    tn = 512
    return pl.pallas_call(
        _ada_kernel,
        grid=(depth, n6 // tn),
        in_specs=[pl.BlockSpec((nc, d), lambda l, j: (0, 0)),
                  pl.BlockSpec((1, d, tn), lambda l, j: (l, 0, j)),
                  pl.BlockSpec((1, 1, tn), lambda l, j: (l, 0, j))],
        out_specs=pl.BlockSpec((1, nc, tn), lambda l, j: (l, 0, j)),
        out_shape=jax.ShapeDtypeStruct((depth, nc, n6), F32),
        compiler_params=_cparams(2),
        name="adaln",
    )(c_all, w_ada, b_ada.reshape(depth, 1, n6))


def _norm_mod(x, g, sc, sh, lrep):
    ms = jnp.mean(x * x, axis=-1, keepdims=True)
    y = x * lax.rsqrt(ms + EPS) * g
    return y * (1.0 + _expand_rows(sc, lrep)) + _expand_rows(sh, lrep)


def _norm_kernel(x_ref, g_ref, sc_ref, sh_ref, *rest, lrep, aliased):
    o_ref = rest[-1]
    h = _norm_mod(x_ref[...], g_ref[...], sc_ref[...], sh_ref[...], lrep)
    o_ref[...] = h.astype(o_ref.dtype)


def _norm_router_kernel(x_ref, g_ref, sc_ref, sh_ref, w_ref, *rest, lrep, aliased):
    h_ref, hb_ref, s_ref = rest[-3:]
    h = _norm_mod(x_ref[...], g_ref[...], sc_ref[...], sh_ref[...], lrep)
    hb_ref[...] = h.astype(BF16)
    n2 = h.shape[1] // 2
    lo = pltpu.bitcast(h[:, :n2].astype(BF16).astype(F32), jnp.uint32)
    hi = pltpu.bitcast(h[:, n2:].astype(BF16).astype(F32), jnp.uint32)
    h_ref[...] = hi | (lo >> 16)
    w = w_ref[...]
    hh = h.astype(BF16)
    hl = (h - hh.astype(F32)).astype(BF16)
    wh = w.astype(BF16)
    wl = (w - wh.astype(F32)).astype(BF16)
    logits = _dot(hh, wh) + _dot(hl, wh) + _dot(hh, wl)
    s_ref[...] = _sigmoid(logits)


def _norm_gate_kernel(x_ref, g_ref, sc_ref, sh_ref, wa1_ref, wa2_ref, ba_ref, *rest, lrep, aliased):
    h_ref, lg_ref = rest[-2:]
    h = _norm_mod(x_ref[...], g_ref[...], sc_ref[...], sh_ref[...], lrep)
    hb = h.astype(BF16)
    h_ref[...] = hb
    a = _dot(hb, wa1_ref[...].astype(BF16))
    z = _dot(a.astype(BF16), wa2_ref[...].astype(BF16)) + ba_ref[...]
    lg_ref[...] = _log_sigmoid(z) * (1.0 / GLA_TAU)


def _norm_call(kind, reg, t_rows, x_all, g, sc, sh, extra, outs, prev):
    d = x_all.shape[1]
    kern = {"plain": _norm_kernel, "router": _norm_router_kernel, "gate": _norm_gate_kernel}[kind]
    row = lambda i: (reg.blk0 + i, 0)
    fixed = lambda i: (0, 0)
    in_specs = [pl.BlockSpec((reg.tile, d), row), pl.BlockSpec((1, d), fixed),
                pl.BlockSpec((reg.nseq, d), fixed), pl.BlockSpec((reg.nseq, d), fixed)]
    args = [x_all, g, sc, sh]
    for e in extra:
        in_specs.append(pl.BlockSpec(e.shape, fixed))
        args.append(e)
    aliases = {}
    if prev is not None:
        for k, p in enumerate(prev):
            aliases[len(args)] = k
            in_specs.append(pl.BlockSpec(memory_space=pl.ANY))
            args.append(p)
    res = pl.pallas_call(
        functools.partial(kern, lrep=reg.lrep, aliased=prev is not None),
        grid=(reg.ntiles,),
        in_specs=in_specs,
        out_specs=[pl.BlockSpec((reg.tile, n), row) for n, _ in outs],
        out_shape=[jax.ShapeDtypeStruct((t_rows, n), dt) for n, dt in outs],
        input_output_aliases=aliases,
        compiler_params=_cparams(1),
        name="norm_" + kind,
    )(*args)
    return list(res)


def _s5_disc_kernel(lr_ref, li_ref, ldt_ref, br_ref, bi_ref, lam_ref, bbr_ref, bbi_ref, *, nsteps):
    lr = lr_ref[...]
    li = li_ref[...]
    dt = jnp.exp(ldt_ref[...])
    mag = jnp.exp(lr * dt)
    ar = mag * jnp.cos(li * dt)
    ai = mag * jnp.sin(li * dt)
    den = lr * lr + li * li
    fr = ((ar - 1.0) * lr + ai * li) / den
    fi = (ai * lr - (ar - 1.0) * li) / den
    lam_ref[0] = ar
    lam_ref[1] = ai
    mags = jnp.exp(nsteps * (lr * dt))
    lam_ref[2] = mags * jnp.cos(nsteps * (li * dt))
    lam_ref[3] = mags * jnp.sin(nsteps * (li * dt))
    for h in range(br_ref.shape[0]):
        br = br_ref[h]
        bi = bi_ref[h]
        bbr_ref[h] = fr * br - fi * bi
        bbi_ref[h] = fr * bi + fi * br


def _s5_disc_call(lam_re, lam_im, log_dt, b_re, b_im, nsteps):
    g, p = lam_re.shape
    hg = b_re.shape[2]
    ldt = jnp.broadcast_to(log_dt[:, None], (g, p))
    brt = jnp.transpose(b_re, (2, 0, 1))
    bit = jnp.transpose(b_im, (2, 0, 1))
    return pl.pallas_call(
        functools.partial(_s5_disc_kernel, nsteps=float(nsteps)),
        out_shape=[jax.ShapeDtypeStruct((4, g, p), F32),
                   jax.ShapeDtypeStruct((hg, g, p), F32),
                   jax.ShapeDtypeStruct((hg, g, p), F32)],
        name="s5_disc",
    )(lam_re, lam_im, ldt, brt, bit)


def _s5_block_weights(bbr, bbi, c_re, c_im):
    hg, g, p = bbr.shape
    gt = S5_KTILE // hg
    kt = g // gt
    eye = jnp.eye(gt, dtype=F32)

    def bd_in(b):
        b = b.reshape(hg, kt, gt, p)
        return jnp.einsum("hkgp,gq->kghqp", b, eye).reshape(kt, gt * hg, gt * p)

    def bd_out(c):
        c = c.reshape(kt, gt, hg, p)
        return jnp.einsum("kghp,gq->kqpgh", c, eye).reshape(kt, gt * p, gt * hg)

    bmat = jnp.concatenate([bd_in(bbr), bd_in(bbi)], axis=2).astype(BF16)
    cmat = jnp.concatenate([bd_out(c_re), -bd_out(c_im)], axis=1).astype(BF16)
    return bmat, cmat


def _s5_kernel(h_ref, b_ref, c_ref, lam_ref, d_ref, sre_ref, sim_ref, z_ref, ore_ref, oim_ref,
               bu_ref, st_ref, *, nsteps, chain):
    rb = pl.program_id(1)
    nc = sre_ref.shape[1]

    @pl.when(rb == 0)
    def _():
        st_ref[0] = sre_ref[...]
        st_ref[1] = sim_ref[...]

    u = h_ref[...]
    bu_ref[...] = _dot(u.astype(BF16), b_ref[0])
    rowid = lax.broadcasted_iota(I32, (SUBLANES, S5_COLS), 0)

    for cb in range(nc // S5_COLS):
        c_re = slice(cb * S5_COLS, (cb + 1) * S5_COLS)
        c_im = slice(nc + cb * S5_COLS, nc + (cb + 1) * S5_COLS)
        ar = jnp.broadcast_to(lam_ref[0, 0:1, c_re], (SUBLANES, S5_COLS))
        ai = jnp.broadcast_to(lam_ref[0, 1:2, c_re], (SUBLANES, S5_COLS))

        def step(s, carry, store):
            xr, xi = carry
            r0 = pl.multiple_of(s * SUBLANES, SUBLANES)
            br = bu_ref[pl.ds(r0, SUBLANES), c_re]
            bi = bu_ref[pl.ds(r0, SUBLANES), c_im]
            nxr = ar * xr - ai * xi + br
            nxi = ar * xi + ai * xr + bi
            if store:
                bu_ref[pl.ds(r0, SUBLANES), c_re] = nxr
                bu_ref[pl.ds(r0, SUBLANES), c_im] = nxi
            return nxr, nxi

        if chain:
            zero = jnp.zeros((SUBLANES, S5_COLS), F32)
            er, ei = lax.fori_loop(0, nsteps, functools.partial(step, store=False), (zero, zero))
            asr = lam_ref[0, 2:3, c_re]
            asi = lam_ref[0, 3:4, c_re]
            pr = st_ref[0, 0:1, c_re]
            pi = st_ref[1, 0:1, c_re]
            x0r, x0i = zero, zero
            for j in range(SUBLANES):
                x0r = jnp.where(rowid == j, jnp.broadcast_to(pr, (SUBLANES, S5_COLS)), x0r)
                x0i = jnp.where(rowid == j, jnp.broadcast_to(pi, (SUBLANES, S5_COLS)), x0i)
                nr = asr * pr - asi * pi + er[j:j + 1]
                ni = asr * pi + asi * pr + ei[j:j + 1]
                pr, pi = nr, ni
            st_ref[0, :, c_re] = jnp.broadcast_to(pr, (SUBLANES, S5_COLS))
            st_ref[1, :, c_re] = jnp.broadcast_to(pi, (SUBLANES, S5_COLS))
            lax.fori_loop(0, nsteps, functools.partial(step, store=True), (x0r, x0i))
        else:
            fr, fi = lax.fori_loop(0, nsteps, functools.partial(step, store=True),
                                   (st_ref[0, :, c_re], st_ref[1, :, c_re]))
            st_ref[0, :, c_re] = fr
            st_ref[1, :, c_re] = fi

    y = _dot(bu_ref[...].astype(BF16), c_ref[0]) + d_ref[...] * u
    z_ref[...] = _gelu_tanh(y).astype(z_ref.dtype)

    @pl.when(rb == pl.num_programs(1) - 1)
    def _():
        ore_ref[...] = st_ref[0]
        oim_ref[...] = st_ref[1]


def _s5_call(hp, bmat, cmat, lam, d_skip, s_re, s_im, nsteps, chain):
    rows, d = hp.shape
    kt = bmat.shape[0]
    nc = bmat.shape[2] // 2
    rblk = nsteps * SUBLANES
    return pl.pallas_call(
        functools.partial(_s5_kernel, nsteps=nsteps, chain=chain),
        grid=(kt, rows // rblk),
        in_specs=[pl.BlockSpec((rblk, S5_KTILE), lambda k, r: (r, k)),
                  pl.BlockSpec((1, S5_KTILE, 2 * nc), lambda k, r: (k, 0, 0)),
                  pl.BlockSpec((1, 2 * nc, S5_KTILE), lambda k, r: (k, 0, 0)),
                  pl.BlockSpec((1, SUBLANES, nc), lambda k, r: (k, 0, 0)),
                  pl.BlockSpec((1, S5_KTILE), lambda k, r: (0, k)),
                  pl.BlockSpec((SUBLANES, nc), lambda k, r: (0, k)),
                  pl.BlockSpec((SUBLANES, nc), lambda k, r: (0, k))],
        out_specs=[pl.BlockSpec((rblk, S5_KTILE), lambda k, r: (r, k)),
                   pl.BlockSpec((SUBLANES, nc), lambda k, r: (0, k)),
                   pl.BlockSpec((SUBLANES, nc), lambda k, r: (0, k))],
        out_shape=[jax.ShapeDtypeStruct((rows, d), BF16),
                   jax.ShapeDtypeStruct(s_re.shape, F32),
                   jax.ShapeDtypeStruct(s_im.shape, F32)],
        scratch_shapes=[pltpu.VMEM((rblk, 2 * nc), F32), pltpu.VMEM((2, SUBLANES, nc), F32)],
        compiler_params=_cparams(2),
        name="s5_scan",
    )(hp, bmat, cmat, lam, d_skip, s_re, s_im)


def _mm_kernel(*refs, mode, lrep, n_w, n_alias):
    lhs_ref = refs[0]
    w_refs = refs[1:1 + n_w]
    pos = 1 + n_w
    i = pl.program_id(1)
    wbf = refs[len(refs) - n_w:]
    o_ref = refs[len(refs) - n_w - 1]

    @pl.when(i == 0)
    def _():
        for w_ref, s_ref in zip(w_refs, wbf):
            s_ref[...] = w_ref[...].astype(BF16)

    lhs = lhs_ref[...]
    if mode == "glu":
        b_ref, zt_ref, x_ref, g_ref = refs[pos:pos + 4]
        t = _dot(lhs, wbf[0][...]) + b_ref[...]
        o = zt_ref[...].astype(F32) * _sigmoid(t)
        o_ref[...] = x_ref[...] + _expand_rows(g_ref[...], lrep) * o
    elif mode == "res":
        x_ref, g_ref = refs[pos:pos + 2]
        o_ref[...] = x_ref[...] + _expand_rows(g_ref[...], lrep) * _dot(lhs, wbf[0][...])
    elif mode == "plain":
        o_ref[...] = _dot(lhs, wbf[0][...]).astype(o_ref.dtype)
    elif mode == "swiglu":
        o_ref[...] = (_silu(_dot(lhs, wbf[0][...])) * _dot(lhs, wbf[1][...])).astype(o_ref.dtype)


def _mm_call(mode, reg, t_rows, lhs, w_list, w_colblk0, n_out, tn, out_dtype, extras, prev, name):
    k_dim = lhs.shape[1]
    nj = n_out // tn
    in_specs = [pl.BlockSpec((reg.tile, k_dim), lambda j, i: (reg.blk0 + i, 0))]
    args = [lhs]
    for w, c0 in zip(w_list, w_colblk0):
        in_specs.append(pl.BlockSpec((k_dim, tn), lambda j, i, c0=c0: (0, c0 + j)))
        args.append(w)
    for a, kind in extras:
        if kind == "col":
            in_specs.append(pl.BlockSpec((1, tn), lambda j, i: (0, j)))
        elif kind == "tile":
            in_specs.append(pl.BlockSpec((reg.tile, tn), lambda j, i: (reg.blk0 + i, j)))
        else:
            in_specs.append(pl.BlockSpec((reg.nseq, tn), lambda j, i: (0, j)))
        args.append(a)
    aliases = {}
    if prev is not None:
        aliases[len(args)] = 0
        in_specs.append(pl.BlockSpec(memory_space=pl.ANY))
        args.append(prev)
    return pl.pallas_call(
        functools.partial(_mm_kernel, mode=mode, lrep=reg.lrep, n_w=len(w_list), n_alias=len(aliases)),
        grid=(nj, reg.ntiles),
        in_specs=in_specs,
        out_specs=pl.BlockSpec((reg.tile, tn), lambda j, i: (reg.blk0 + i, j)),
        out_shape=jax.ShapeDtypeStruct((t_rows, n_out), out_dtype),
        scratch_shapes=[pltpu.VMEM((k_dim, tn), BF16) for _ in w_list],
        input_output_aliases=aliases,
        compiler_params=_cparams(2),
        name=name,
    )(*args)


def _gla_consts(chunk):
    nlev = int(np.log2(chunk))
    assert 1 << nlev == chunk
    tri = np.tril(np.ones((chunk, chunk), np.float32))
    r = np.arange(chunk)
    wall, masks = [tri], []
    for l in range(nlev):
        w = chunk >> (l + 1)
        blk = r // (2 * w)
        second = (r & w) != 0
        wall.append(tri[blk * 2 * w + w - 1])
        masks.append(((blk[:, None] == blk[None, :]) & second[:, None] & (~second)[None, :]).astype(np.float32))
    masks.append(np.eye(chunk, dtype=np.float32))
    return jnp.asarray(np.concatenate(wall, 0), BF16), jnp.asarray(np.stack(masks, 0), F32)


def _gla_kernel(q_ref, k_ref, v_ref, gate_ref, lg_ref, s0_ref, wall_ref, mask_ref, gn_ref,
                *rest, chunk, nheads, dk, dv):
    o_ref, sout_ref, s_ref = rest[-3:]
    c = pl.program_id(1)
    nlev = mask_ref.shape[0] - 1

    @pl.when(c == 0)
    def _():
        s_ref[...] = s0_ref[0]

    lg = lg_ref[...]
    p1, p2, p3 = _split3(lg)
    wall = wall_ref[...]
    bg = _dot(wall, p1) + _dot(wall, p2) + _dot(wall, p3)
    b = bg[0:chunk]
    q = q_ref[...] * (dk ** -0.5)
    k = k_ref[...]
    row = lax.broadcasted_iota(I32, q.shape, 0)
    qs, ks = [], []
    for l in range(nlev):
        w = chunk >> (l + 1)
        g = bg[(l + 1) * chunk:(l + 2) * chunk]
        second = (row & w) != 0
        e = jnp.exp(jnp.where(second, b - g, g - b))
        qk = jnp.where(second, q, k) * e
        qs.append(jnp.where(second, qk, 0.0).astype(BF16))
        ks.append(jnp.where(second, 0.0, qk).astype(BF16))
    qb = q.astype(BF16)
    kb = k.astype(BF16)
    q_in = (q * jnp.exp(b)).astype(BF16)
    k_dec = (k * jnp.exp(b[chunk - 1:chunk] - b)).astype(BF16)
    ones = jnp.ones((chunk, LANES), BF16)
    gn = gn_ref[...]
    for h in range(nheads):
        ck = slice(h * dk, (h + 1) * dk)
        cv = slice(h * dv, (h + 1) * dv)
        att = _dot_t1(qb[:, ck], kb[:, ck]) * mask_ref[nlev]
        for l in range(nlev):
            att = att + _dot_t1(qs[l][:, ck], ks[l][:, ck]) * mask_ref[l]
        vh = v_ref[:, cv].astype(BF16)
        s_h = s_ref[h]
        o = _dot(att.astype(BF16), vh) + _dot(q_in[:, ck], s_h.astype(BF16))
        dcol = jnp.exp(_dot_t0(p1[:, ck], ones) + _dot_t0(p2[:, ck], ones) + _dot_t0(p3[:, ck], ones))
        s_ref[h] = jnp.concatenate([dcol] * (dv // LANES), axis=1) * s_h + _dot_t0(k_dec[:, ck], vh)
        ms = jnp.mean(o * o, axis=-1, keepdims=True)
        on = o * lax.rsqrt(ms + EPS) * gn
        o_ref[:, cv] = (on * _silu(gate_ref[:, cv])).astype(o_ref.dtype)

    @pl.when(c == pl.num_programs(1) - 1)
    def _():
        sout_ref[0] = s_ref[...]


def _gla_call(reg_row0, nseq, seqlen, chunk, t_rows, proj, lg, s0, g_norm, prev):
    nheads, dk, dv = s0.shape[1:]
    hk = nheads * dk
    d = nheads * dv
    nch = seqlen // chunk
    rb0 = reg_row0 // chunk
    wall, masks = _gla_consts(chunk)
    rowblk = lambda b, c: rb0 + b * nch + c
    in_specs = [pl.BlockSpec((chunk, hk), lambda b, c: (rowblk(b, c), 0)),
                pl.BlockSpec((chunk, hk), lambda b, c: (rowblk(b, c), 1)),
                pl.BlockSpec((chunk, d), lambda b, c: (rowblk(b, c), 1)),
                pl.BlockSpec((chunk, d), lambda b, c: (rowblk(b, c), 2)),
                pl.BlockSpec((chunk, hk), lambda b, c: (rowblk(b, c), 0)),
                pl.BlockSpec((1, nheads, dk, dv), lambda b, c: (b, 0, 0, 0)),
                pl.BlockSpec(wall.shape, lambda b, c: (0, 0)),
                pl.BlockSpec(masks.shape, lambda b, c: (0, 0, 0)),
                pl.BlockSpec((1, dv), lambda b, c: (0, 0))]
    args = [proj, proj, proj, proj, lg, s0, wall, masks, g_norm]
    aliases = {}
    if prev is not None:
        aliases[len(args)] = 0
        in_specs.append(pl.BlockSpec(memory_space=pl.ANY))
        args.append(prev)
    return pl.pallas_call(
        functools.partial(_gla_kernel, chunk=chunk, nheads=nheads, dk=dk, dv=dv),
        grid=(nseq, nch),
        in_specs=in_specs,
        out_specs=[pl.BlockSpec((chunk, d), lambda b, c: (rowblk(b, c), 0)),
                   pl.BlockSpec((1, nheads, dk, dv), lambda b, c: (b, 0, 0, 0))],
        out_shape=[jax.ShapeDtypeStruct((t_rows, d), BF16),
                   jax.ShapeDtypeStruct(s0.shape, F32)],
        scratch_shapes=[pltpu.VMEM((nheads, dk, dv), F32)],
        input_output_aliases=aliases,
        compiler_params=_cparams(2),
        name="gla_chunk",
    )(*args)


def _route(scores, b_router):
    n, e = scores.shape
    per = e // N_GROUPS
    choice = scores + b_router.astype(F32)
    gscore = lax.top_k(choice.reshape(n, N_GROUPS, per), 2)[0].sum(-1)
    _, gidx = lax.top_k(gscore, TOPK_GROUPS)
    gmask = jnp.any(gidx[:, :, None] == jnp.arange(N_GROUPS)[None, None, :], axis=1)
    emask = jnp.repeat(gmask, per, axis=1)
    _, idx = lax.top_k(jnp.where(emask, choice, -jnp.inf), TOP_K)
    w = jnp.take_along_axis(scores, idx, axis=1)
    w = w / jnp.sum(w, axis=-1, keepdims=True) * ROUTED_SCALE
    return idx, w


def _dispatch_plan(idx, n_experts):
    n = idx.shape[0]
    a = n * TOP_K
    rb = EXP_ROWBLK
    flat_e = idx.reshape(a).astype(I32)
    order = jnp.argsort(flat_e).astype(I32)
    e_s = flat_e[order]
    counts = jnp.bincount(flat_e, length=n_experts).astype(I32)
    blocks_e = (counts + rb - 1) // rb
    padded = blocks_e * rb
    pad_end = jnp.cumsum(padded)
    pad_start = pad_end - padded
    start = jnp.cumsum(counts) - counts
    dest = (pad_start[e_s] + jnp.arange(a, dtype=I32) - start[e_s]).astype(I32)
    np_rows = (a + rb - 1) // rb * rb + rb * n_experts
    n_blocks = np_rows // rb
    row_tok = jnp.zeros((np_rows + EXP_NBLK * rb,), I32).at[dest].set(order // TOP_K)
    pos = jnp.zeros((a,), I32).at[order].set(dest)
    items_e = (blocks_e + EXP_NBLK - 1) // EXP_NBLK
    item_end = jnp.cumsum(items_e)
    item_start = item_end - items_e
    n_items = n_experts + n_blocks // EXP_NBLK + 1
    ii = jnp.arange(n_items, dtype=I32)
    total = item_end[-1]
    e_of = jnp.minimum(jnp.searchsorted(item_end, ii, side="right"), n_experts - 1).astype(I32)
    valid = ii < total
    local = ii - item_start[e_of]
    e_last = e_of[jnp.maximum(total - 1, 0)]
    ie = jnp.where(valid, e_of, e_last).astype(I32)
    rsb = jnp.where(valid, pad_start[e_of] // rb + local * EXP_NBLK, 0).astype(I32)
    nrb = jnp.where(valid, jnp.minimum(EXP_NBLK, blocks_e[e_of] - local * EXP_NBLK), 0).astype(I32)
    return ie, rsb, nrb, row_tok.reshape(-1, rb), pos, np_rows, n_items


def _experts_kernel(ie_ref, rsb_ref, nrb_ref, tok_hbm, h_hbm, wg_ref, wu_ref, wo_ref, y_hbm,
                    idx_ref, xbuf, act, ybuf, wgbf, wubf, wobf, sem_idx, sem_g, sem_y, ycnt_ref,
                    *, n_items, n_a, n_b):
    i = pl.program_id(0)
    st = pl.program_id(1)
    slot = i % 2
    nslot = 1 - slot
    nrb = nrb_ref[i]
    rb = EXP_ROWBLK
    tf = wgbf.shape[1]
    td = wobf.shape[1]
    half = wgbf.shape[0] // 2

    def idx_copy(item, s):
        return pltpu.make_async_copy(tok_hbm.at[pl.ds(rsb_ref[item], EXP_NBLK)], idx_ref.at[s], sem_idx.at[s])

    def issue_gather(s, nblk):
        def blk(bi, carry):
            def row(r, carry2):
                tok = idx_ref[s, bi, r]
                pltpu.make_async_copy(h_hbm.at[pl.ds(tok, 1)], xbuf.at[s, pl.ds(bi * rb + r, 1)],
                                      sem_g.at[s]).start()
                return carry2
            return lax.fori_loop(0, rb, row, carry)
        lax.fori_loop(0, nblk, blk, 0)

    def wait_gather(s, nblk):
        def blk(bi, carry):
            pltpu.make_async_copy(h_hbm.at[pl.ds(0, rb)], xbuf.at[s, pl.ds(0, rb)], sem_g.at[s]).wait()
            return carry
        lax.fori_loop(0, nblk, blk, 0)

    def wait_out(s):
        def blk(bi, carry):
            pltpu.make_async_copy(ybuf.at[s, pl.ds(0, rb)], y_hbm.at[pl.ds(0, rb), pl.ds(0, td)],
                                  sem_y.at[s]).wait()
            return carry
        lax.fori_loop(0, ycnt_ref[s], blk, 0)
        ycnt_ref[s] = 0

    @pl.when(st == 0)
    def _():
        @pl.when(i == 0)
        def _():
            ycnt_ref[0] = 0
            ycnt_ref[1] = 0
            first = idx_copy(0, 0)
            first.start()
            first.wait()
            issue_gather(0, nrb_ref[0])
            if n_items > 1:
                idx_copy(1, 1).start()

        @pl.when(i + 1 < n_items)
        def _():
            idx_copy(i + 1, nslot).wait()

        wait_gather(slot, nrb)

        @pl.when(i + 1 < n_items)
        def _():
            issue_gather(nslot, nrb_ref[jnp.minimum(i + 1, n_items - 1)])

        @pl.when(i + 2 < n_items)
        def _():
            idx_copy(jnp.minimum(i + 2, n_items - 1), slot).start()

    @pl.when((st < n_a) & (nrb > 0))
    def _():
        wgbf[...] = wg_ref[0, 0].astype(BF16)
        wubf[...] = wu_ref[0, 0].astype(BF16)

        def blk(bi, carry):
            r0 = pl.multiple_of(bi * rb, rb)
            xw = xbuf[slot, pl.ds(r0, rb), :]
            xlo = pltpu.bitcast(xw << 16, F32).astype(BF16)
            xhi = pltpu.bitcast(xw & jnp.uint32(0xFFFF0000), F32).astype(BF16)
            gt = _dot(xlo, wgbf[0:half, :]) + _dot(xhi, wgbf[half:, :])
            up = _dot(xlo, wubf[0:half, :]) + _dot(xhi, wubf[half:, :])
            act[jnp.minimum(st, n_a - 1), pl.ds(r0, rb), :] = (_silu(gt) * up).astype(BF16)
            return carry
        lax.fori_loop(0, nrb, blk, 0)

    @pl.when((st >= n_a) & (nrb > 0))
    def _():
        dj = st - n_a
        ys = (i * n_b + dj) % 2
        wobf[...] = wo_ref[0, 0].astype(BF16)
        wait_out(ys)
        row0 = rsb_ref[i] * rb
        col0 = pl.multiple_of(dj * td, td)

        def blk(bi, carry):
            r0 = pl.multiple_of(bi * rb, rb)
            y = _dot(act[0, pl.ds(r0, rb), :], wobf[0:tf, :])
            for a in range(1, n_a):
                y = y + _dot(act[a, pl.ds(r0, rb), :], wobf[a * tf:(a + 1) * tf, :])
            ybuf[ys, pl.ds(r0, rb), :] = y
            pltpu.make_async_copy(ybuf.at[ys, pl.ds(r0, rb)],
                                  y_hbm.at[pl.ds(pl.multiple_of(row0 + r0, rb), rb), pl.ds(col0, td)],
                                  sem_y.at[ys]).start()
            return carry
        lax.fori_loop(0, nrb, blk, 0)
        ycnt_ref[ys] = nrb

    @pl.when((i == n_items - 1) & (st == n_a + n_b - 1))
    def _():
        wait_out(0)
        wait_out(1)


def _experts_call(layer, plan, h_packed, w_in, w_out):
    ie, rsb, nrb, tok2d, _, np_rows, n_items = plan
    _, _, d, f2 = w_in.shape
    f = f2 // 2
    tf = min(EXP_TF, f)
    td = min(EXP_TD, d)
    n_a, n_b = f // tf, d // td
    rmax = EXP_NBLK * EXP_ROWBLK

    def fa(st, nr):
        return jnp.where(nr > 0, jnp.minimum(st, n_a - 1), n_a - 1)

    def fb(st, nr):
        return jnp.where(nr > 0, jnp.maximum(st - n_a, 0), n_b - 1)

    grid_spec = pltpu.PrefetchScalarGridSpec(
        num_scalar_prefetch=3,
        grid=(n_items, n_a + n_b),
        in_specs=[pl.BlockSpec(memory_space=pl.ANY),
                  pl.BlockSpec(memory_space=pl.ANY),
                  pl.BlockSpec((1, 1, d, tf), lambda i, st, ie, rsb, nrb: (layer, ie[i], 0, fa(st, nrb[i]))),
                  pl.BlockSpec((1, 1, d, tf), lambda i, st, ie, rsb, nrb: (layer, ie[i], 0, n_a + fa(st, nrb[i]))),
                  pl.BlockSpec((1, 1, f, td), lambda i, st, ie, rsb, nrb: (layer, ie[i], 0, fb(st, nrb[i])))],
        out_specs=pl.BlockSpec(memory_space=pl.ANY),
        scratch_shapes=[pltpu.SMEM((2, EXP_NBLK, EXP_ROWBLK), I32),
                        pltpu.VMEM((2, rmax, d // 2), jnp.uint32),
                        pltpu.VMEM((n_a, rmax, tf), BF16),
                        pltpu.VMEM((2, rmax, td), F32),
                        pltpu.VMEM((d, tf), BF16),
                        pltpu.VMEM((d, tf), BF16),
                        pltpu.VMEM((f, td), BF16),
                        pltpu.SemaphoreType.DMA((2,)),
                        pltpu.SemaphoreType.DMA((2,)),
                        pltpu.SemaphoreType.DMA((2,)),
                        pltpu.SMEM((2,), I32)])
    return pl.pallas_call(
        functools.partial(_experts_kernel, n_items=n_items, n_a=n_a, n_b=n_b),
        grid_spec=grid_spec,
        out_shape=jax.ShapeDtypeStruct((np_rows, d), F32),
        compiler_params=_cparams(2),
        name="moe_experts",
    )(ie, rsb, nrb, tok2d, h_packed, w_in, w_in, w_out)


def _combine_kernel(pos_hbm, w_ref, sh_ref, x_ref, g_ref, fg_ref, y_hbm, *rest,
                    lrep, blk0, ntiles, final):
    o_ref, idx_ref, gbuf, sem_i, sem_g = rest[-5:]
    i = pl.program_id(0)
    slot = i % 2
    nslot = 1 - slot
    tile = x_ref.shape[0]
    nidx = tile * TOP_K // LANES
    tok_per_row = LANES // TOP_K

    def idx_copy(t, s):
        return pltpu.make_async_copy(pos_hbm.at[pl.ds((blk0 + t) * nidx, nidx)], idx_ref.at[s], sem_i.at[s])

    def issue_gather(s):
        def tok(t, carry):
            for j in range(TOP_K):
                p = idx_ref[s, t // tok_per_row, (t % tok_per_row) * TOP_K + j]
                pltpu.make_async_copy(y_hbm.at[pl.ds(p, 1)], gbuf.at[s, j, pl.ds(t, 1)], sem_g.at[s]).start()
            return carry
        lax.fori_loop(0, tile, tok, 0)

    @pl.when(i == 0)
    def _():
        first = idx_copy(0, 0)
        first.start()
        first.wait()
        issue_gather(0)
        if ntiles > 1:
            idx_copy(1, 1).start()

    @pl.when(i + 1 < ntiles)
    def _():
        idx_copy(jnp.minimum(i + 1, ntiles - 1), nslot).wait()

    for j in range(TOP_K):
        pltpu.make_async_copy(y_hbm.at[pl.ds(0, tile)], gbuf.at[slot, j], sem_g.at[slot]).wait()

    @pl.when(i + 1 < ntiles)
    def _():
        issue_gather(nslot)

    @pl.when(i + 2 < ntiles)
    def _():
        idx_copy(jnp.minimum(i + 2, ntiles - 1), slot).start()

    acc = sh_ref[...]
    for j in range(TOP_K):
        acc = acc + w_ref[:, j:j + 1] * gbuf[slot, j]
    o = x_ref[...] + _expand_rows(g_ref[...], lrep, tile, i * tile) * acc
    if final:
        ms = jnp.mean(o * o, axis=-1, keepdims=True)
        o = o * lax.rsqrt(ms + EPS) * fg_ref[...]
    o_ref[...] = o


def _combine_call(reg, t_rows, pos2d, wts, shared, x_all, g, final_g, y_sorted, final, prev):
    d = x_all.shape[1]
    row = lambda i: (reg.blk0 + i, 0)
    fixed = lambda i: (0, 0)
    in_specs = [pl.BlockSpec(memory_space=pl.ANY),
                pl.BlockSpec((reg.tile, TOP_K), row),
                pl.BlockSpec((reg.tile, d), row),
                pl.BlockSpec((reg.tile, d), row),
                pl.BlockSpec((reg.nseq, d), fixed),
                pl.BlockSpec((1, d), fixed),
                pl.BlockSpec(memory_space=pl.ANY)]
    args = [pos2d, wts, shared, x_all, g, final_g, y_sorted]
    aliases = {}
    if prev is not None:
        aliases[len(args)] = 0
        in_specs.append(pl.BlockSpec(memory_space=pl.ANY))
        args.append(prev)
    return pl.pallas_call(
        functools.partial(_combine_kernel, lrep=reg.lrep, blk0=reg.blk0, ntiles=reg.ntiles, final=final),
        grid=(reg.ntiles,),
        in_specs=in_specs,
        out_specs=pl.BlockSpec((reg.tile, d), row),
        out_shape=jax.ShapeDtypeStruct((t_rows, d), F32),
        scratch_shapes=[pltpu.SMEM((2, reg.tile * TOP_K // LANES, LANES), I32),
                        pltpu.VMEM((2, TOP_K, reg.tile, d), F32),
                        pltpu.SemaphoreType.DMA((2,)),
                        pltpu.SemaphoreType.DMA((2,))],
        input_output_aliases=aliases,
        compiler_params=_cparams(1),
        name="moe_combine",
    )(*args)


def _both(fn, regs):
    out = fn(regs[0], None)
    return fn(regs[1], out)


def kernel(x_prompt, x_sample, state_s5_re, state_s5_im, state_gla, c_prompt, c_sample, w_ada, b_ada, norm_g, s5_lam_re, s5_lam_im, s5_log_dt, s5_b_re, s5_b_im, s5_c_re, s5_c_im, s5_d, s5_w_glu, s5_b_glu, gla_w_in, gla_w_a1, gla_w_a2, gla_b_a, gla_g_norm, gla_w_o, moe_w_router, moe_b_router, moe_w_in, moe_w_out, moe_ws_in, moe_ws_out, final_g):
    bp, lp, d = x_prompt.shape
    bs, ls, _ = x_sample.shape
    assert bp == 1 and bs == SUBLANES and ls % SUBLANES == 0
    depth = w_ada.shape[0]
    n_exp = moe_w_in.shape[1]
    f_sh = moe_ws_out.shape[1]
    srows = bs * ls
    t_rows = lp + srows
    s5_steps = 64
    s5_blk = s5_steps * SUBLANES
    gla_chunk = 64
    assert lp % PROMPT_TILE == 0 and lp % s5_blk == 0 and lp % srows == 0

    regs_mm = (_Region(0, lp, PROMPT_TILE, 1, PROMPT_TILE), _Region(lp, srows, srows, bs, ls))
    regs_nm = (_Region(0, lp, PROMPT_TILE // 2, 1, PROMPT_TILE // 2), _Region(lp, srows, srows, bs, ls))
    regs_cb = (_Region(0, lp, COMB_TILE, 1, COMB_TILE), _Region(lp, srows, COMB_TILE, bs, ls))

    x_all = jnp.concatenate([x_prompt.reshape(lp, d), x_sample.reshape(srows, d)], axis=0)
    c_all = jnp.concatenate([c_prompt, c_sample, jnp.zeros((2 * SUBLANES - 1 - bs, d), F32)], axis=0)
    mod = _ada_call(c_all, w_ada, b_ada)

    def mods(layer, k):
        m = mod[layer, :, k * d:(k + 1) * d]
        return (m[0:1], m[1:1 + bs])

    new_re_p, new_im_p, new_gla_p, new_re_s, new_im_s, new_gla_s = [], [], [], [], [], []
    for i in range(depth):
        sh1, sc1, g1, sh2, sc2, g2 = [mods(i, k) for k in range(6)]
        j = i // 2
        ng1 = norm_g[i, 0].reshape(1, d)
        ng2 = norm_g[i, 1].reshape(1, d)
        if i % 2 == 0:
            (h_all,) = _both(lambda r, prev: _norm_call(
                "plain", r, t_rows, x_all, ng1, sc1[r.nseq > 1], sh1[r.nseq > 1], [], [(d, F32)], prev), regs_nm)
            grp, pst = s5_lam_re.shape[1:]
            ncol = grp * pst
            lam, bbr, bbi = _s5_disc_call(s5_lam_re[j], s5_lam_im[j], s5_log_dt[j], s5_b_re[j], s5_b_im[j], s5_steps)
            bmat, cmat = _s5_block_weights(bbr, bbi, s5_c_re[j], s5_c_im[j])
            kt = bmat.shape[0]
            lamt = jnp.transpose(lam.reshape(4, kt, ncol // kt), (1, 0, 2))
            lamt = jnp.concatenate([lamt, jnp.zeros_like(lamt)], axis=1)
            dsk = s5_d[j].reshape(1, d)
            hp = h_all[:lp].reshape(lp // s5_blk, SUBLANES, s5_steps, d).transpose(0, 2, 1, 3).reshape(lp, d)
            hs = h_all[lp:].reshape(bs, ls, d).transpose(1, 0, 2).reshape(srows, d)
            zeros_st = jnp.zeros((SUBLANES, ncol), F32)
            zp, pre, pim = _s5_call(hp, bmat, cmat, lamt, dsk, zeros_st, zeros_st, s5_steps, True)
            zs, sre, sim = _s5_call(hs, bmat, cmat, lamt, dsk, state_s5_re[j].reshape(bs, ncol),
                                    state_s5_im[j].reshape(bs, ncol), ls, False)
            z_all = jnp.concatenate(
                [zp.reshape(lp // s5_blk, s5_steps, SUBLANES, d).transpose(0, 2, 1, 3).reshape(lp, d),
                 zs.reshape(ls, bs, d).transpose(1, 0, 2).reshape(srows, d)], axis=0)
            new_re_p.append(pre[0].reshape(1, grp, pst))
            new_im_p.append(pim[0].reshape(1, grp, pst))
            new_re_s.append(sre.reshape(bs, grp, pst))
            new_im_s.append(sim.reshape(bs, grp, pst))
            bglu = s5_b_glu[j].reshape(1, d)
            x_all = _both(lambda r, prev: _mm_call(
                "glu", r, t_rows, z_all, [s5_w_glu[j]], [0], d, 512, F32,
                [(bglu, "col"), (z_all, "tile"), (x_all, "tile"), (g1[r.nseq > 1], "seq")], prev, "s5_glu"), regs_mm)
        else:
            hk = gla_w_a2.shape[2]
            assert 2 * hk == d
            h_bf, lg = _both(lambda r, prev: _norm_call(
                "gate", r, t_rows, x_all, ng1, sc1[r.nseq > 1], sh1[r.nseq > 1],
                [gla_w_a1[j], gla_w_a2[j], gla_b_a[j].reshape(1, hk)], [(d, BF16), (hk, F32)], prev), regs_nm)
            proj = _both(lambda r, prev: _mm_call(
                "plain", r, t_rows, h_bf, [gla_w_in[j]], [0], 3 * d, 512, F32, [], prev, "gla_proj"), regs_mm)
            gn = gla_g_norm[j].reshape(1, -1)
            s0p = jnp.zeros((1,) + state_gla.shape[2:], F32)
            o_all, gla_p = _gla_call(0, 1, lp, gla_chunk, t_rows, proj, lg, s0p, gn, None)
            o_all, gla_s = _gla_call(lp, bs, ls, ls, t_rows, proj, lg, state_gla[j], gn, o_all)
            new_gla_p.append(gla_p)
            new_gla_s.append(gla_s)
            x_all = _both(lambda r, prev: _mm_call(
                "res", r, t_rows, o_all, [gla_w_o[j]], [0], d, 512, F32,
                [(x_all, "tile"), (g1[r.nseq > 1], "seq")], prev, "gla_out"), regs_mm)

        h_pk, h_bf, scores = _both(lambda r, prev: _norm_call(
            "router", r, t_rows, x_all, ng2, sc2[r.nseq > 1], sh2[r.nseq > 1], [moe_w_router[i]],
            [(d // 2, jnp.uint32), (d, BF16), (n_exp, F32)], prev), regs_nm)
        idx, wts = _route(scores, moe_b_router[i])
        plan = _dispatch_plan(idx, n_exp)
        y_sorted = _experts_call(i, plan, h_pk, moe_w_in, moe_w_out)
        act = _both(lambda r, prev: _mm_call(
            "swiglu", r, t_rows, h_bf, [moe_ws_in[i], moe_ws_in[i]], [0, f_sh // 256], f_sh, 256, BF16,
            [], prev, "shared_in"), regs_mm)
        shared = _both(lambda r, prev: _mm_call(
            "plain", r, t_rows, act, [moe_ws_out[i]], [0], d, 512, F32, [], prev, "shared_out"), regs_mm)
        pos2d = plan[4].reshape(-1, LANES)
        last = i == depth - 1
        fg = final_g.reshape(1, d)
        x_all = _both(lambda r, prev: _combine_call(
            r, t_rows, pos2d, wts, shared, x_all, g2[r.nseq > 1], fg, y_sorted, last, prev), regs_cb)

    y_prompt = x_all[:lp].reshape(bp, lp, d)
    y_sample = x_all[lp:].reshape(bs, ls, d)
    return (y_prompt, y_sample, jnp.stack(new_re_p), jnp.stack(new_im_p), jnp.stack(new_gla_p),
            jnp.stack(new_re_s), jnp.stack(new_im_s), jnp.stack(new_gla_s))
```

```python
import functools

import numpy as np
import jax
import jax.numpy as jnp
from jax import lax
from jax.experimental import pallas as pl
from jax.experimental.pallas import tpu as pltpu

F32 = jnp.float32
BF16 = jnp.bfloat16
I32 = jnp.int32

EPS = 1e-6
GLA_TAU = 16.0
TOP_K = 8
N_GROUPS = 8
TOPK_GROUPS = 4
ROUTED_SCALE = 2.5

LANES = 128
SUBLANES = 8
MXU_DIM = 256
VMEM_LIMIT = 56 << 20

PROMPT_TILE = 512
S5_KTILE = 256
S5_COLS = 512
EXP_ROWBLK = 128
EXP_NBLK = 9
EXP_TF = 256
EXP_TD = 1024
DISP_TILE = 128
COMB_TILE = 64


def _cparams(n_axes, vmem=VMEM_LIMIT):
    return pltpu.CompilerParams(dimension_semantics=("arbitrary",) * n_axes, vmem_limit_bytes=vmem)


class _Region:
    def __init__(self, row0, rows, tile, nseq, lrep):
        self.row0, self.rows, self.tile, self.nseq, self.lrep = row0, rows, tile, nseq, lrep
        self.blk0 = row0 // tile
        self.ntiles = rows // tile
        assert row0 % tile == 0 and rows % tile == 0


def _expand_rows(m, lrep, rows=None, row0=0):
    nseq, n = m.shape
    if nseq == 1:
        return m
    rows = nseq * lrep if rows is None else rows
    r = lax.broadcasted_iota(I32, (rows, n), 0) + row0
    out = jnp.broadcast_to(m[0:1, :], (rows, n))
    for b in range(1, nseq):
        out = jnp.where(r >= b * lrep, jnp.broadcast_to(m[b:b + 1, :], (rows, n)), out)
    return out


def _sigmoid(x):
    return 1.0 / (1.0 + jnp.exp(-x))


def _silu(x):
    return x * _sigmoid(x)


def _gelu_tanh(x):
    return 0.5 * x * (1.0 + jnp.tanh(0.7978845608028654 * (x + 0.044715 * x * x * x)))


def _log_sigmoid(x):
    return jnp.minimum(x, 0.0) - jnp.log1p(jnp.exp(-jnp.abs(x)))


def _split3(x):
    p1 = x.astype(BF16)
    r1 = x - p1.astype(F32)
    p2 = r1.astype(BF16)
    r2 = r1 - p2.astype(F32)
    return p1, p2, r2.astype(BF16)


def _dot(a, b):
    return jnp.dot(a, b, preferred_element_type=F32)


def _dot_t0(a, b):
    return lax.dot_general(a, b, (((0,), (0,)), ((), ())), preferred_element_type=F32)


def _dot_t1(a, b):
    return lax.dot_general(a, b, (((1,), (1,)), ((), ())), preferred_element_type=F32)


def _ada_kernel(c_ref, w_ref, b_ref, o_ref):
    s = _silu(c_ref[...])
    o_ref[0] = _dot(s.astype(BF16), w_ref[0].astype(BF16)) + b_ref[0]


def _ada_call(c_all, w_ada, b_ada):
    depth, d, n6 = w_ada.shape
    nc = c_all.shape[0]
    tn = 512
    return pl.pallas_call(
        _ada_kernel,
        grid=(depth, n6 // tn),
        in_specs=[pl.BlockSpec((nc, d), lambda l, j: (0, 0)),
                  pl.BlockSpec((1, d, tn), lambda l, j: (l, 0, j)),
                  pl.BlockSpec((1, 1, tn), lambda l, j: (l, 0, j))],
        out_specs=pl.BlockSpec((1, nc, tn), lambda l, j: (l, 0, j)),
        out_shape=jax.ShapeDtypeStruct((depth, nc, n6), F32),
        compiler_params=_cparams(2),
        name="adaln",
    )(c_all, w_ada, b_ada.reshape(depth, 1, n6))


def _norm_mod(x, g, sc, sh, lrep):
    ms = jnp.mean(x * x, axis=-1, keepdims=True)
    y = x * lax.rsqrt(ms + EPS) * g
    return y * (1.0 + _expand_rows(sc, lrep)) + _expand_rows(sh, lrep)


def _norm_kernel(x_ref, g_ref, sc_ref, sh_ref, *rest, lrep, aliased):
    o_ref = rest[-1]
    h = _norm_mod(x_ref[...], g_ref[...], sc_ref[...], sh_ref[...], lrep)
    o_ref[...] = h.astype(o_ref.dtype)


def _cols_to_lanes(cols, dtype):
    r = cols[0].shape[0]
    lane = lax.broadcasted_iota(I32, (r, len(cols)), 1)
    out = jnp.zeros((r, len(cols)), dtype)
    for j, c in enumerate(cols):
        out = jnp.where(lane == j, c.astype(dtype), out)
    return out


def _router_kernel(x_ref, g_ref, sc_ref, sh_ref, w_ref, b_ref, cin_ref, *rest, lrep):
    hp_ref, hb_ref, idx_ref, wt_ref, rk_ref, cout_ref, run_ref = rest[-7:]
    i = pl.program_id(0)

    @pl.when(i == 0)
    def _():
        run_ref[...] = cin_ref[0:1, :]

    h = _norm_mod(x_ref[...], g_ref[...], sc_ref[...], sh_ref[...], lrep)
    hb_ref[...] = h.astype(BF16)
    n2 = h.shape[1] // 2
    lo = pltpu.bitcast(h[:, :n2].astype(BF16).astype(F32), jnp.uint32)
    hi = pltpu.bitcast(h[:, n2:].astype(BF16).astype(F32), jnp.uint32)
    hp_ref[...] = hi | (lo >> 16)
    w = w_ref[...]
    hh = h.astype(BF16)
    hl = (h - hh.astype(F32)).astype(BF16)
    wh = w.astype(BF16)
    wl = (w - wh.astype(F32)).astype(BF16)
    scores = _sigmoid(_dot(hh, wh) + _dot(hl, wh) + _dot(hh, wl))

    rows, n_exp = scores.shape
    per = n_exp // N_GROUPS
    neg = -jnp.inf
    big = n_exp + 1
    lane = lax.broadcasted_iota(I32, (rows, n_exp), 1)
    grp = lane // per
    choice = scores + b_ref[...]
    gs = jnp.zeros((rows, n_exp), F32)
    for g in range(N_GROUPS):
        ing = grp == g
        m = jnp.where(ing, choice, neg)
        m1 = jnp.max(m, axis=-1, keepdims=True)
        i1 = jnp.min(jnp.where(m == m1, lane, big), axis=-1, keepdims=True)
        m2 = jnp.max(jnp.where(lane == i1, neg, m), axis=-1, keepdims=True)
        gs = jnp.where(ing, m1 + m2, gs)
    masked = jnp.full((rows, n_exp), neg, F32)
    for _ in range(TOPK_GROUPS):
        mx = jnp.max(gs, axis=-1, keepdims=True)
        gi = jnp.min(jnp.where(gs == mx, grp, big), axis=-1, keepdims=True)
        sel = grp == gi
        masked = jnp.where(sel, choice, masked)
        gs = jnp.where(sel, neg, gs)
    idx_cols, w_cols, hits = [], [], []
    for _ in range(TOP_K):
        mx = jnp.max(masked, axis=-1, keepdims=True)
        ei = jnp.min(jnp.where(masked == mx, lane, big), axis=-1, keepdims=True)
        hit = lane == ei
        hits.append(hit)
        idx_cols.append(ei)
        w_cols.append(jnp.sum(jnp.where(hit, scores, 0.0), axis=-1, keepdims=True))
        masked = jnp.where(hit, neg, masked)
    wsum = w_cols[0]
    for c in w_cols[1:]:
        wsum = wsum + c
    scale = ROUTED_SCALE / wsum
    idx_ref[...] = _cols_to_lanes(idx_cols, I32)
    wt_ref[...] = _cols_to_lanes([c * scale for c in w_cols], F32)
    onehot = jnp.zeros((rows, n_exp), F32)
    for hit in hits:
        onehot = jnp.where(hit, 1.0, onehot)
    rr = lax.broadcasted_iota(I32, (rows, rows), 0)
    cc = lax.broadcasted_iota(I32, (rows, rows), 1)
    before = jnp.where(rr > cc, 1.0, 0.0).astype(BF16)
    cum = _dot(before, onehot.astype(BF16)) + run_ref[...]
    rk_ref[...] = _cols_to_lanes(
        [jnp.sum(jnp.where(hit, cum, 0.0), axis=-1, keepdims=True) for hit in hits], I32)
    run_ref[...] = run_ref[...] + jnp.sum(onehot, axis=0, keepdims=True)
    cout_ref[...] = jnp.broadcast_to(run_ref[...], cout_ref.shape)


def _router_call(reg, t_rows, x_all, g, sc, sh, w_router, b_router, cnt_in, prev):
    d = x_all.shape[1]
    n_exp = w_router.shape[1]
    row = lambda i: (reg.blk0 + i, 0)
    fixed = lambda i: (0, 0)
    in_specs = [pl.BlockSpec((reg.tile, d), row), pl.BlockSpec((1, d), fixed),
                pl.BlockSpec((reg.nseq, d), fixed), pl.BlockSpec((reg.nseq, d), fixed),
                pl.BlockSpec((d, n_exp), fixed), pl.BlockSpec((1, n_exp), fixed),
                pl.BlockSpec((SUBLANES, n_exp), fixed)]
    args = [x_all, g, sc, sh, w_router, b_router, cnt_in]
    outs = [(d // 2, jnp.uint32), (d, BF16), (TOP_K, I32), (TOP_K, F32), (TOP_K, I32)]
    aliases = {}
    if prev is not None:
        for k in range(len(outs)):
            aliases[len(args)] = k
            in_specs.append(pl.BlockSpec(memory_space=pl.ANY))
            args.append(prev[k])
    res = pl.pallas_call(
        functools.partial(_router_kernel, lrep=reg.lrep),
        grid=(reg.ntiles,),
        in_specs=in_specs,
        out_specs=[pl.BlockSpec((reg.tile, n), row) for n, _ in outs] + [pl.BlockSpec((SUBLANES, n_exp), fixed)],
        out_shape=[jax.ShapeDtypeStruct((t_rows, n), dt) for n, dt in outs]
        + [jax.ShapeDtypeStruct((SUBLANES, n_exp), F32)],
        scratch_shapes=[pltpu.VMEM((1, n_exp), F32)],
        input_output_aliases=aliases,
        compiler_params=_cparams(1),
        name="moe_router",
    )(*args)
    return list(res)


def _norm_gate_kernel(x_ref, g_ref, sc_ref, sh_ref, wa1_ref, wa2_ref, ba_ref, *rest, lrep, aliased):
    h_ref, lg_ref = rest[-2:]
    h = _norm_mod(x_ref[...], g_ref[...], sc_ref[...], sh_ref[...], lrep)
    hb = h.astype(BF16)
    h_ref[...] = hb
    a = _dot(hb, wa1_ref[...].astype(BF16))
    z = _dot(a.astype(BF16), wa2_ref[...].astype(BF16)) + ba_ref[...]
    lg_ref[...] = _log_sigmoid(z) * (1.0 / GLA_TAU)


def _norm_call(kind, reg, t_rows, x_all, g, sc, sh, extra, outs, prev):
    d = x_all.shape[1]
    kern = {"plain": _norm_kernel, "gate": _norm_gate_kernel}[kind]
    row = lambda i: (reg.blk0 + i, 0)
    fixed = lambda i: (0, 0)
    in_specs = [pl.BlockSpec((reg.tile, d), row), pl.BlockSpec((1, d), fixed),
                pl.BlockSpec((reg.nseq, d), fixed), pl.BlockSpec((reg.nseq, d), fixed)]
    args = [x_all, g, sc, sh]
    for e in extra:
        in_specs.append(pl.BlockSpec(e.shape, fixed))
        args.append(e)
    aliases = {}
    if prev is not None:
        for k, p in enumerate(prev):
            aliases[len(args)] = k
            in_specs.append(pl.BlockSpec(memory_space=pl.ANY))
            args.append(p)
    res = pl.pallas_call(
        functools.partial(kern, lrep=reg.lrep, aliased=prev is not None),
        grid=(reg.ntiles,),
        in_specs=in_specs,
        out_specs=[pl.BlockSpec((reg.tile, n), row) for n, _ in outs],
        out_shape=[jax.ShapeDtypeStruct((t_rows, n), dt) for n, dt in outs],
        input_output_aliases=aliases,
        compiler_params=_cparams(1),
        name="norm_" + kind,
    )(*args)
    return list(res)


def _s5_disc_kernel(lr_ref, li_ref, ldt_ref, br_ref, bi_ref, lam_ref, bbr_ref, bbi_ref, *, nsteps):
    lr = lr_ref[...]
    li = li_ref[...]
    dt = jnp.exp(ldt_ref[...])
    mag = jnp.exp(lr * dt)
    ar = mag * jnp.cos(li * dt)
    ai = mag * jnp.sin(li * dt)
    den = lr * lr + li * li
    fr = ((ar - 1.0) * lr + ai * li) / den
    fi = (ai * lr - (ar - 1.0) * li) / den
    lam_ref[0] = ar
    lam_ref[1] = ai
    mags = jnp.exp(nsteps * (lr * dt))
    lam_ref[2] = mags * jnp.cos(nsteps * (li * dt))
    lam_ref[3] = mags * jnp.sin(nsteps * (li * dt))
    for h in range(br_ref.shape[0]):
        br = br_ref[h]
        bi = bi_ref[h]
        bbr_ref[h] = fr * br - fi * bi
        bbi_ref[h] = fr * bi + fi * br


def _s5_disc_call(lam_re, lam_im, log_dt, b_re, b_im, nsteps):
    g, p = lam_re.shape
    hg = b_re.shape[2]
    ldt = jnp.broadcast_to(log_dt[:, None], (g, p))
    brt = jnp.transpose(b_re, (2, 0, 1))
    bit = jnp.transpose(b_im, (2, 0, 1))
    return pl.pallas_call(
        functools.partial(_s5_disc_kernel, nsteps=float(nsteps)),
        out_shape=[jax.ShapeDtypeStruct((4, g, p), F32),
                   jax.ShapeDtypeStruct((hg, g, p), F32),
                   jax.ShapeDtypeStruct((hg, g, p), F32)],
        name="s5_disc",
    )(lam_re, lam_im, ldt, brt, bit)


def _s5_block_weights(bbr, bbi, c_re, c_im):
    hg, g, p = bbr.shape
    gt = S5_KTILE // hg
    kt = g // gt
    eye = jnp.eye(gt, dtype=F32)

    def bd_in(b):
        b = b.reshape(hg, kt, gt, p)
        return jnp.einsum("hkgp,gq->kghqp", b, eye).reshape(kt, gt * hg, gt * p)

    def bd_out(c):
        c = c.reshape(kt, gt, hg, p)
        return jnp.einsum("kghp,gq->kqpgh", c, eye).reshape(kt, gt * p, gt * hg)

    bmat = jnp.concatenate([bd_in(bbr), bd_in(bbi)], axis=2).astype(BF16)
    cmat = jnp.concatenate([bd_out(c_re), -bd_out(c_im)], axis=1).astype(BF16)
    return bmat, cmat


def _s5_kernel(h_ref, b_ref, c_ref, lam_ref, d_ref, sre_ref, sim_ref, z_ref, ore_ref, oim_ref,
               bu_ref, st_ref, *, nsteps, chain):
    rb = pl.program_id(1)
    nc = sre_ref.shape[1]

    @pl.when(rb == 0)
    def _():
        st_ref[0] = sre_ref[...]
        st_ref[1] = sim_ref[...]

    u = h_ref[...]
    bu_ref[...] = _dot(u.astype(BF16), b_ref[0])
    rowid = lax.broadcasted_iota(I32, (SUBLANES, S5_COLS), 0)

    for cb in range(nc // S5_COLS):
        c_re = slice(cb * S5_COLS, (cb + 1) * S5_COLS)
        c_im = slice(nc + cb * S5_COLS, nc + (cb + 1) * S5_COLS)
        ar = jnp.broadcast_to(lam_ref[0, 0:1, c_re], (SUBLANES, S5_COLS))
        ai = jnp.broadcast_to(lam_ref[0, 1:2, c_re], (SUBLANES, S5_COLS))

        def step(s, carry, store):
            xr, xi = carry
            r0 = pl.multiple_of(s * SUBLANES, SUBLANES)
            br = bu_ref[pl.ds(r0, SUBLANES), c_re]
            bi = bu_ref[pl.ds(r0, SUBLANES), c_im]
            nxr = ar * xr - ai * xi + br
            nxi = ar * xi + ai * xr + bi
            if store:
                bu_ref[pl.ds(r0, SUBLANES), c_re] = nxr
                bu_ref[pl.ds(r0, SUBLANES), c_im] = nxi
            return nxr, nxi

        if chain:
            zero = jnp.zeros((SUBLANES, S5_COLS), F32)
            er, ei = lax.fori_loop(0, nsteps, functools.partial(step, store=False), (zero, zero))
            asr = lam_ref[0, 2:3, c_re]
            asi = lam_ref[0, 3:4, c_re]
            pr = st_ref[0, 0:1, c_re]
            pi = st_ref[1, 0:1, c_re]
            x0r, x0i = zero, zero
            for j in range(SUBLANES):
                x0r = jnp.where(rowid == j, jnp.broadcast_to(pr, (SUBLANES, S5_COLS)), x0r)
                x0i = jnp.where(rowid == j, jnp.broadcast_to(pi, (SUBLANES, S5_COLS)), x0i)
                nr = asr * pr - asi * pi + er[j:j + 1]
                ni = asr * pi + asi * pr + ei[j:j + 1]
                pr, pi = nr, ni
            st_ref[0, :, c_re] = jnp.broadcast_to(pr, (SUBLANES, S5_COLS))
            st_ref[1, :, c_re] = jnp.broadcast_to(pi, (SUBLANES, S5_COLS))
            lax.fori_loop(0, nsteps, functools.partial(step, store=True), (x0r, x0i))
        else:
            fr, fi = lax.fori_loop(0, nsteps, functools.partial(step, store=True),
                                   (st_ref[0, :, c_re], st_ref[1, :, c_re]))
            st_ref[0, :, c_re] = fr
            st_ref[1, :, c_re] = fi

    y = _dot(bu_ref[...].astype(BF16), c_ref[0]) + d_ref[...] * u
    z_ref[...] = _gelu_tanh(y).astype(z_ref.dtype)

    @pl.when(rb == pl.num_programs(1) - 1)
    def _():
        ore_ref[...] = st_ref[0]
        oim_ref[...] = st_ref[1]


def _s5_call(hp, bmat, cmat, lam, d_skip, s_re, s_im, nsteps, chain):
    rows, d = hp.shape
    kt = bmat.shape[0]
    nc = bmat.shape[2] // 2
    rblk = nsteps * SUBLANES
    return pl.pallas_call(
        functools.partial(_s5_kernel, nsteps=nsteps, chain=chain),
        grid=(kt, rows // rblk),
        in_specs=[pl.BlockSpec((rblk, S5_KTILE), lambda k, r: (r, k)),
                  pl.BlockSpec((1, S5_KTILE, 2 * nc), lambda k, r: (k, 0, 0)),
                  pl.BlockSpec((1, 2 * nc, S5_KTILE), lambda k, r: (k, 0, 0)),
                  pl.BlockSpec((1, SUBLANES, nc), lambda k, r: (k, 0, 0)),
                  pl.BlockSpec((1, S5_KTILE), lambda k, r: (0, k)),
                  pl.BlockSpec((SUBLANES, nc), lambda k, r: (0, k)),
                  pl.BlockSpec((SUBLANES, nc), lambda k, r: (0, k))],
        out_specs=[pl.BlockSpec((rblk, S5_KTILE), lambda k, r: (r, k)),
                   pl.BlockSpec((SUBLANES, nc), lambda k, r: (0, k)),
                   pl.BlockSpec((SUBLANES, nc), lambda k, r: (0, k))],
        out_shape=[jax.ShapeDtypeStruct((rows, d), BF16),
                   jax.ShapeDtypeStruct(s_re.shape, F32),
                   jax.ShapeDtypeStruct(s_im.shape, F32)],
        scratch_shapes=[pltpu.VMEM((rblk, 2 * nc), F32), pltpu.VMEM((2, SUBLANES, nc), F32)],
        compiler_params=_cparams(2),
        name="s5_scan",
    )(hp, bmat, cmat, lam, d_skip, s_re, s_im)


def _mm_kernel(*refs, mode, lrep, n_w, n_alias):
    lhs_ref = refs[0]
    w_refs = refs[1:1 + n_w]
    pos = 1 + n_w
    i = pl.program_id(1)
    wbf = refs[len(refs) - n_w:]
    o_ref = refs[len(refs) - n_w - 1]

    @pl.when(i == 0)
    def _():
        for w_ref, s_ref in zip(w_refs, wbf):
            s_ref[...] = w_ref[...].astype(BF16)

    lhs = lhs_ref[...]
    if mode == "glu":
        b_ref, zt_ref, x_ref, g_ref = refs[pos:pos + 4]
        t = _dot(lhs, wbf[0][...]) + b_ref[...]
        o = zt_ref[...].astype(F32) * _sigmoid(t)
        o_ref[...] = x_ref[...] + _expand_rows(g_ref[...], lrep) * o
    elif mode == "res":
        x_ref, g_ref = refs[pos:pos + 2]
        o_ref[...] = x_ref[...] + _expand_rows(g_ref[...], lrep) * _dot(lhs, wbf[0][...])
    elif mode == "plain":
        o_ref[...] = _dot(lhs, wbf[0][...]).astype(o_ref.dtype)
    elif mode == "swiglu":
        o_ref[...] = (_silu(_dot(lhs, wbf[0][...])) * _dot(lhs, wbf[1][...])).astype(o_ref.dtype)


def _mm_call(mode, reg, t_rows, lhs, w_list, w_colblk0, n_out, tn, out_dtype, extras, prev, name):
    k_dim = lhs.shape[1]
    nj = n_out // tn
    in_specs = [pl.BlockSpec((reg.tile, k_dim), lambda j, i: (reg.blk0 + i, 0))]
    args = [lhs]
    for w, c0 in zip(w_list, w_colblk0):
        in_specs.append(pl.BlockSpec((k_dim, tn), lambda j, i, c0=c0: (0, c0 + j)))
        args.append(w)
    for a, kind in extras:
        if kind == "col":
            in_specs.append(pl.BlockSpec((1, tn), lambda j, i: (0, j)))
        elif kind == "tile":
            in_specs.append(pl.BlockSpec((reg.tile, tn), lambda j, i: (reg.blk0 + i, j)))
        else:
            in_specs.append(pl.BlockSpec((reg.nseq, tn), lambda j, i: (0, j)))
        args.append(a)
    aliases = {}
    if prev is not None:
        aliases[len(args)] = 0
        in_specs.append(pl.BlockSpec(memory_space=pl.ANY))
        args.append(prev)
    return pl.pallas_call(
        functools.partial(_mm_kernel, mode=mode, lrep=reg.lrep, n_w=len(w_list), n_alias=len(aliases)),
        grid=(nj, reg.ntiles),
        in_specs=in_specs,
        out_specs=pl.BlockSpec((reg.tile, tn), lambda j, i: (reg.blk0 + i, j)),
        out_shape=jax.ShapeDtypeStruct((t_rows, n_out), out_dtype),
        scratch_shapes=[pltpu.VMEM((k_dim, tn), BF16) for _ in w_list],
        input_output_aliases=aliases,
        compiler_params=_cparams(2),
        name=name,
    )(*args)


def _gla_consts(chunk):
    nlev = int(np.log2(chunk))
    assert 1 << nlev == chunk
    tri = np.tril(np.ones((chunk, chunk), np.float32))
    r = np.arange(chunk)
    wall, masks = [tri], []
    for l in range(nlev):
        w = chunk >> (l + 1)
        blk = r // (2 * w)
        second = (r & w) != 0
        wall.append(tri[blk * 2 * w + w - 1])
        masks.append(((blk[:, None] == blk[None, :]) & second[:, None] & (~second)[None, :]).astype(np.float32))
    masks.append(np.eye(chunk, dtype=np.float32))
    return jnp.asarray(np.concatenate(wall, 0), BF16), jnp.asarray(np.stack(masks, 0), F32)


def _gla_kernel(q_ref, k_ref, v_ref, gate_ref, lg_ref, s0_ref, wall_ref, mask_ref, gn_ref,
                *rest, chunk, nheads, dk, dv):
    o_ref, sout_ref, s_ref = rest[-3:]
    c = pl.program_id(1)
    nlev = mask_ref.shape[0] - 1

    @pl.when(c == 0)
    def _():
        s_ref[...] = s0_ref[0]

    lg = lg_ref[...]
    p1, p2, p3 = _split3(lg)
    wall = wall_ref[...]
    bg = _dot(wall, p1) + _dot(wall, p2) + _dot(wall, p3)
    b = bg[0:chunk]
    q = q_ref[...] * (dk ** -0.5)
    k = k_ref[...]
    row = lax.broadcasted_iota(I32, q.shape, 0)
    qs, ks = [], []
    for l in range(nlev):
        w = chunk >> (l + 1)
        g = bg[(l + 1) * chunk:(l + 2) * chunk]
        second = (row & w) != 0
        e = jnp.exp(jnp.where(second, b - g, g - b))
        qk = jnp.where(second, q, k) * e
        qs.append(jnp.where(second, qk, 0.0).astype(BF16))
        ks.append(jnp.where(second, 0.0, qk).astype(BF16))
    qb = q.astype(BF16)
    kb = k.astype(BF16)
    q_in = (q * jnp.exp(b)).astype(BF16)
    k_dec = (k * jnp.exp(b[chunk - 1:chunk] - b)).astype(BF16)
    ones = jnp.ones((chunk, LANES), BF16)
    gn = gn_ref[...]
    for h in range(nheads):
        ck = slice(h * dk, (h + 1) * dk)
        cv = slice(h * dv, (h + 1) * dv)
        att = _dot_t1(qb[:, ck], kb[:, ck]) * mask_ref[nlev]
        for l in range(nlev):
            att = att + _dot_t1(qs[l][:, ck], ks[l][:, ck]) * mask_ref[l]
        vh = v_ref[:, cv].astype(BF16)
        s_h = s_ref[h]
        o = _dot(att.astype(BF16), vh) + _dot(q_in[:, ck], s_h.astype(BF16))
        dcol = jnp.exp(_dot_t0(p1[:, ck], ones) + _dot_t0(p2[:, ck], ones) + _dot_t0(p3[:, ck], ones))
        s_ref[h] = jnp.concatenate([dcol] * (dv // LANES), axis=1) * s_h + _dot_t0(k_dec[:, ck], vh)
        ms = jnp.mean(o * o, axis=-1, keepdims=True)
        on = o * lax.rsqrt(ms + EPS) * gn
        o_ref[:, cv] = (on * _silu(gate_ref[:, cv])).astype(o_ref.dtype)

    @pl.when(c == pl.num_programs(1) - 1)
    def _():
        sout_ref[0] = s_ref[...]


def _gla_call(reg_row0, nseq, seqlen, chunk, t_rows, proj, lg, s0, g_norm, prev):
    nheads, dk, dv = s0.shape[1:]
    hk = nheads * dk
    d = nheads * dv
    nch = seqlen // chunk
    rb0 = reg_row0 // chunk
    wall, masks = _gla_consts(chunk)
    rowblk = lambda b, c: rb0 + b * nch + c
    in_specs = [pl.BlockSpec((chunk, hk), lambda b, c: (rowblk(b, c), 0)),
                pl.BlockSpec((chunk, hk), lambda b, c: (rowblk(b, c), 1)),
                pl.BlockSpec((chunk, d), lambda b, c: (rowblk(b, c), 1)),
                pl.BlockSpec((chunk, d), lambda b, c: (rowblk(b, c), 2)),
                pl.BlockSpec((chunk, hk), lambda b, c: (rowblk(b, c), 0)),
                pl.BlockSpec((1, nheads, dk, dv), lambda b, c: (b, 0, 0, 0)),
                pl.BlockSpec(wall.shape, lambda b, c: (0, 0)),
                pl.BlockSpec(masks.shape, lambda b, c: (0, 0, 0)),
                pl.BlockSpec((1, dv), lambda b, c: (0, 0))]
    args = [proj, proj, proj, proj, lg, s0, wall, masks, g_norm]
    aliases = {}
    if prev is not None:
        aliases[len(args)] = 0
        in_specs.append(pl.BlockSpec(memory_space=pl.ANY))
        args.append(prev)
    return pl.pallas_call(
        functools.partial(_gla_kernel, chunk=chunk, nheads=nheads, dk=dk, dv=dv),
        grid=(nseq, nch),
        in_specs=in_specs,
        out_specs=[pl.BlockSpec((chunk, d), lambda b, c: (rowblk(b, c), 0)),
                   pl.BlockSpec((1, nheads, dk, dv), lambda b, c: (b, 0, 0, 0))],
        out_shape=[jax.ShapeDtypeStruct((t_rows, d), BF16),
                   jax.ShapeDtypeStruct(s0.shape, F32)],
        scratch_shapes=[pltpu.VMEM((nheads, dk, dv), F32)],
        input_output_aliases=aliases,
        compiler_params=_cparams(2),
        name="gla_chunk",
    )(*args)


def _dispatch_plan(idx, rank, counts, n_tokens):
    n_experts = counts.shape[0]
    rb = EXP_ROWBLK
    a = n_tokens * TOP_K
    blocks_e = (counts + rb - 1) // rb
    padded = blocks_e * rb
    pad_start = jnp.cumsum(padded) - padded
    hit = idx[:, :, None] == jnp.arange(n_experts, dtype=I32)[None, None, :]
    dest = jnp.sum(jnp.where(hit, pad_start[None, None, :], 0), axis=-1) + rank
    np_rows = (a + rb - 1) // rb * rb + rb * n_experts
    items_e = (blocks_e + EXP_NBLK - 1) // EXP_NBLK
    item_end = jnp.cumsum(items_e)
    item_start = item_end - items_e
    n_items = n_experts + (np_rows // rb) // EXP_NBLK
    ii = jnp.arange(n_items, dtype=I32)
    total = item_end[-1]
    e_of = jnp.minimum(jnp.sum(item_end[None, :] <= ii[:, None], axis=1), n_experts - 1).astype(I32)
    valid = ii < total
    local = ii - item_start[e_of]
    e_last = e_of[jnp.maximum(total - 1, 0)]
    ie = jnp.where(valid, e_of, e_last).astype(I32)
    rsb = jnp.where(valid, pad_start[e_of] // rb + local * EXP_NBLK, 0).astype(I32)
    nrb = jnp.where(valid, jnp.minimum(EXP_NBLK, blocks_e[e_of] - local * EXP_NBLK), 0).astype(I32)
    return ie, rsb, nrb, pad_start.astype(I32), dest.astype(I32), np_rows, n_items


def _dispatch_kernel(cnt_ref, pst_ref, dest_hbm, h_ref, xs_hbm, idx_ref, hbuf, zrow, sem_i, sem_d, sem_z,
                     *, ntiles, n_experts):
    i = pl.program_id(0)
    slot = i % 2
    tile = h_ref.shape[0]
    nidx = tile * TOP_K // LANES
    tok_per_row = LANES // TOP_K
    rb = EXP_ROWBLK

    def idx_copy(t, s):
        return pltpu.make_async_copy(dest_hbm.at[pl.ds(t * nidx, nidx)], idx_ref.at[s], sem_i.at[s])

    def wait_rows(s):
        for _ in range(TOP_K):
            pltpu.make_async_copy(hbuf.at[s], xs_hbm.at[pl.ds(0, tile)], sem_d.at[s]).wait()

    @pl.when(i == 0)
    def _():
        idx_copy(0, 0).start()

    idx_copy(i, slot).wait()

    @pl.when(i + 1 < ntiles)
    def _():
        idx_copy(jnp.minimum(i + 1, ntiles - 1), 1 - slot).start()

    hbuf[slot] = h_ref[...]

    def tok(t, carry):
        for j in range(TOP_K):
            p = idx_ref[slot, t // tok_per_row, (t % tok_per_row) * TOP_K + j]
            pltpu.make_async_copy(hbuf.at[slot, pl.ds(t, 1)], xs_hbm.at[pl.ds(p, 1)], sem_d.at[slot]).start()
        return carry
    lax.fori_loop(0, tile, tok, 0)

    @pl.when(i > 0)
    def _():
        wait_rows(1 - slot)

    @pl.when(i == ntiles - 1)
    def _():
        wait_rows(slot)
        zrow[...] = jnp.zeros(zrow.shape, zrow.dtype)

        def expert(e, carry):
            cnt = cnt_ref[e]
            base = pst_ref[e]
            end = (cnt + rb - 1) // rb * rb

            def zstart(r, c2):
                pltpu.make_async_copy(zrow.at[pl.ds(0, 1)], xs_hbm.at[pl.ds(base + r, 1)], sem_z).start()
                return c2

            def zwait(r, c2):
                pltpu.make_async_copy(zrow.at[pl.ds(0, 1)], xs_hbm.at[pl.ds(0, 1)], sem_z).wait()
                return c2
            lax.fori_loop(cnt, end, zstart, 0)
            lax.fori_loop(cnt, end, zwait, 0)
            return carry
        lax.fori_loop(0, n_experts, expert, 0)


def _dispatch_call(counts, pad_start, dest2d, h_packed, np_rows, tile):
    t_rows, dh = h_packed.shape
    ntiles = t_rows // tile
    n_experts = counts.shape[0]
    grid_spec = pltpu.PrefetchScalarGridSpec(
        num_scalar_prefetch=2,
        grid=(ntiles,),
        in_specs=[pl.BlockSpec(memory_space=pl.ANY),
                  pl.BlockSpec((tile, dh), lambda i, c, p: (i, 0))],
        out_specs=pl.BlockSpec(memory_space=pl.ANY),
        scratch_shapes=[pltpu.SMEM((2, tile * TOP_K // LANES, LANES), I32),
                        pltpu.VMEM((2, tile, dh), jnp.uint32),
                        pltpu.VMEM((SUBLANES, dh), jnp.uint32),
                        pltpu.SemaphoreType.DMA((2,)),
                        pltpu.SemaphoreType.DMA((2,)),
                        pltpu.SemaphoreType.DMA])
    return pl.pallas_call(
        functools.partial(_dispatch_kernel, ntiles=ntiles, n_experts=n_experts),
        grid_spec=grid_spec,
        out_shape=jax.ShapeDtypeStruct((np_rows, dh), jnp.uint32),
        compiler_params=_cparams(1),
        name="moe_dispatch",
    )(counts, pad_start, dest2d, h_packed)


def _experts_kernel(ie_ref, rsb_ref, nrb_ref, xs_hbm, wg_ref, wu_ref, wo_ref, y_hbm,
                    xbuf, gacc, uacc, act, ybuf, wgbf, wubf, wobf, sem_x, sem_y, ycnt_ref,
                    *, n_items, n_f, n_b):
    i = pl.program_id(0)
    st = pl.program_id(1)
    n_a = 2 * n_f
    nrb = nrb_ref[i]
    rb = EXP_ROWBLK
    tf = wgbf.shape[1]
    td = wobf.shape[1]

    def x_copy(item, bi):
        src = pl.multiple_of((rsb_ref[item] + bi) * rb, rb)
        return pltpu.make_async_copy(xs_hbm.at[pl.ds(src, rb)], xbuf.at[pl.ds(pl.multiple_of(bi * rb, rb), rb)],
                                     sem_x)

    def start_x(item):
        def blk(bi, carry):
            x_copy(item, bi).start()
            return carry
        lax.fori_loop(0, nrb_ref[item], blk, 0)

    def wait_x(item):
        def blk(bi, carry):
            x_copy(item, bi).wait()
            return carry
        lax.fori_loop(0, nrb_ref[item], blk, 0)

    def wait_out(s):
        def blk(bi, carry):
            pltpu.make_async_copy(ybuf.at[s, pl.ds(0, rb)], y_hbm.at[pl.ds(0, rb), pl.ds(0, td)],
                                  sem_y.at[s]).wait()
            return carry
        lax.fori_loop(0, ycnt_ref[s], blk, 0)
        ycnt_ref[s] = 0

    @pl.when((i == 0) & (st == 0))
    def _():
        ycnt_ref[0] = 0
        ycnt_ref[1] = 0
        xbuf[...] = jnp.zeros(xbuf.shape, xbuf.dtype)
        start_x(0)

    @pl.when(st == 0)
    def _():
        wait_x(i)

    @pl.when((st == n_a) & (i + 1 < n_items))
    def _():
        start_x(jnp.minimum(i + 1, n_items - 1))

    @pl.when((st < n_a) & (nrb > 0))
    def _():
        kh = st % 2
        fj = st // 2
        wgbf[...] = wg_ref[0, 0].astype(BF16)
        wubf[...] = wu_ref[0, 0].astype(BF16)
        shift = jnp.where(kh == 0, 16, 0).astype(jnp.uint32)
        xh = pltpu.bitcast((xbuf[...] << shift) & jnp.uint32(0xFFFF0000), F32).astype(BF16)
        g = _dot(xh, wgbf[...])
        u = _dot(xh, wubf[...])

        @pl.when(kh == 0)
        def _():
            gacc[...] = g
            uacc[...] = u

        for f_static in range(n_f):
            @pl.when((kh == 1) & (fj == f_static))
            def _():
                act[:, f_static * tf:(f_static + 1) * tf] = (
                    _silu(gacc[...] + g) * (uacc[...] + u)).astype(BF16)

    @pl.when((st >= n_a) & (nrb > 0))
    def _():
        dj = st - n_a
        ys = (i * n_b + dj) % 2
        wobf[...] = wo_ref[0, 0].astype(BF16)
        wait_out(ys)
        ybuf[ys] = _dot(act[...], wobf[...])
        row0 = rsb_ref[i] * rb
        col0 = pl.multiple_of(dj * td, td)

        def blk(bi, carry):
            r0 = pl.multiple_of(bi * rb, rb)
            pltpu.make_async_copy(ybuf.at[ys, pl.ds(r0, rb)],
                                  y_hbm.at[pl.ds(pl.multiple_of(row0 + r0, rb), rb), pl.ds(col0, td)],
                                  sem_y.at[ys]).start()
            return carry
        lax.fori_loop(0, nrb, blk, 0)
        ycnt_ref[ys] = nrb

    @pl.when((i == n_items - 1) & (st == n_a + n_b - 1))
    def _():
        wait_out(0)
        wait_out(1)


def _experts_call(layer, ie, rsb, nrb, n_items, xs, w_in, w_out):
    _, _, d, f2 = w_in.shape
    f = f2 // 2
    np_rows, dh = xs.shape
    tf = min(EXP_TF, f)
    td = min(EXP_TD, d)
    n_f, n_b = f // tf, d // td
    n_a = 2 * n_f
    rmax = EXP_NBLK * EXP_ROWBLK

    def kh(st, nr):
        return jnp.where((nr > 0) & (st < n_a), st % 2, 1)

    def fj(st, nr):
        return jnp.where((nr > 0) & (st < n_a), st // 2, n_f - 1)

    def dj(st, nr):
        return jnp.where(nr > 0, jnp.maximum(st - n_a, 0), n_b - 1)

    grid_spec = pltpu.PrefetchScalarGridSpec(
        num_scalar_prefetch=3,
        grid=(n_items, n_a + n_b),
        in_specs=[pl.BlockSpec(memory_space=pl.ANY),
                  pl.BlockSpec((1, 1, dh, tf),
                               lambda i, st, ie, rsb, nrb: (layer, ie[i], kh(st, nrb[i]), fj(st, nrb[i]))),
                  pl.BlockSpec((1, 1, dh, tf),
                               lambda i, st, ie, rsb, nrb: (layer, ie[i], kh(st, nrb[i]), n_f + fj(st, nrb[i]))),
                  pl.BlockSpec((1, 1, f, td), lambda i, st, ie, rsb, nrb: (layer, ie[i], 0, dj(st, nrb[i])))],
        out_specs=pl.BlockSpec(memory_space=pl.ANY),
        scratch_shapes=[pltpu.VMEM((rmax, dh), jnp.uint32),
                        pltpu.VMEM((rmax, tf), F32),
                        pltpu.VMEM((rmax, tf), F32),
                        pltpu.VMEM((rmax, f), BF16),
                        pltpu.VMEM((2, rmax, td), F32),
                        pltpu.VMEM((dh, tf), BF16),
                        pltpu.VMEM((dh, tf), BF16),
                        pltpu.VMEM((f, td), BF16),
                        pltpu.SemaphoreType.DMA,
                        pltpu.SemaphoreType.DMA((2,)),
                        pltpu.SMEM((2,), I32)])
    return pl.pallas_call(
        functools.partial(_experts_kernel, n_items=n_items, n_f=n_f, n_b=n_b),
        grid_spec=grid_spec,
        out_shape=jax.ShapeDtypeStruct((np_rows, d), F32),
        compiler_params=_cparams(2),
        name="moe_experts",
    )(ie, rsb, nrb, xs, w_in, w_in, w_out)


def _combine_kernel(pos_hbm, w_ref, sh_ref, x_ref, g_ref, fg_ref, y_hbm, *rest,
                    lrep, blk0, ntiles, final):
    o_ref, idx_ref, gbuf, sem_i, sem_g = rest[-5:]
    i = pl.program_id(0)
    slot = i % 2
    nslot = 1 - slot
    tile = x_ref.shape[0]
    nidx = tile * TOP_K // LANES
    tok_per_row = LANES // TOP_K

    def idx_copy(t, s):
        return pltpu.make_async_copy(pos_hbm.at[pl.ds((blk0 + t) * nidx, nidx)], idx_ref.at[s], sem_i.at[s])

    def issue_gather(s):
        def tok(t, carry):
            for j in range(TOP_K):
                p = idx_ref[s, t // tok_per_row, (t % tok_per_row) * TOP_K + j]
                pltpu.make_async_copy(y_hbm.at[pl.ds(p, 1)], gbuf.at[s, j, pl.ds(t, 1)], sem_g.at[s]).start()
            return carry
        lax.fori_loop(0, tile, tok, 0)

    @pl.when(i == 0)
    def _():
        first = idx_copy(0, 0)
        first.start()
        first.wait()
        issue_gather(0)
        if ntiles > 1:
            idx_copy(1, 1).start()

    @pl.when(i + 1 < ntiles)
    def _():
        idx_copy(jnp.minimum(i + 1, ntiles - 1), nslot).wait()

    for j in range(TOP_K):
        pltpu.make_async_copy(y_hbm.at[pl.ds(0, tile)], gbuf.at[slot, j], sem_g.at[slot]).wait()

    @pl.when(i + 1 < ntiles)
    def _():
        issue_gather(nslot)

    @pl.when(i + 2 < ntiles)
    def _():
        idx_copy(jnp.minimum(i + 2, ntiles - 1), slot).start()

    acc = sh_ref[...]
    for j in range(TOP_K):
        acc = acc + w_ref[:, j:j + 1] * gbuf[slot, j]
    o = x_ref[...] + _expand_rows(g_ref[...], lrep, tile, i * tile) * acc
    if final:
        ms = jnp.mean(o * o, axis=-1, keepdims=True)
        o = o * lax.rsqrt(ms + EPS) * fg_ref[...]
    o_ref[...] = o


def _combine_call(reg, t_rows, pos2d, wts, shared, x_all, g, final_g, y_sorted, final, prev):
    d = x_all.shape[1]
    row = lambda i: (reg.blk0 + i, 0)
    fixed = lambda i: (0, 0)
    in_specs = [pl.BlockSpec(memory_space=pl.ANY),
                pl.BlockSpec((reg.tile, TOP_K), row),
                pl.BlockSpec((reg.tile, d), row),
                pl.BlockSpec((reg.tile, d), row),
                pl.BlockSpec((reg.nseq, d), fixed),
                pl.BlockSpec((1, d), fixed),
                pl.BlockSpec(memory_space=pl.ANY)]
    args = [pos2d, wts, shared, x_all, g, final_g, y_sorted]
    aliases = {}
    if prev is not None:
        aliases[len(args)] = 0
        in_specs.append(pl.BlockSpec(memory_space=pl.ANY))
        args.append(prev)
    return pl.pallas_call(
        functools.partial(_combine_kernel, lrep=reg.lrep, blk0=reg.blk0, ntiles=reg.ntiles, final=final),
        grid=(reg.ntiles,),
        in_specs=in_specs,
        out_specs=pl.BlockSpec((reg.tile, d), row),
        out_shape=jax.ShapeDtypeStruct((t_rows, d), F32),
        scratch_shapes=[pltpu.SMEM((2, reg.tile * TOP_K // LANES, LANES), I32),
                        pltpu.VMEM((2, TOP_K, reg.tile, d), F32),
                        pltpu.SemaphoreType.DMA((2,)),
                        pltpu.SemaphoreType.DMA((2,))],
        input_output_aliases=aliases,
        compiler_params=_cparams(1),
        name="moe_combine",
    )(*args)


def _both(fn, regs):
    out = fn(regs[0], None)
    return fn(regs[1], out)


def kernel(x_prompt, x_sample, state_s5_re, state_s5_im, state_gla, c_prompt, c_sample, w_ada, b_ada, norm_g, s5_lam_re, s5_lam_im, s5_log_dt, s5_b_re, s5_b_im, s5_c_re, s5_c_im, s5_d, s5_w_glu, s5_b_glu, gla_w_in, gla_w_a1, gla_w_a2, gla_b_a, gla_g_norm, gla_w_o, moe_w_router, moe_b_router, moe_w_in, moe_w_out, moe_ws_in, moe_ws_out, final_g):
    bp, lp, d = x_prompt.shape
    bs, ls, _ = x_sample.shape
    assert bp == 1 and bs == SUBLANES and ls % SUBLANES == 0
    depth = w_ada.shape[0]
    n_exp = moe_w_in.shape[1]
    f_sh = moe_ws_out.shape[1]
    srows = bs * ls
    t_rows = lp + srows
    s5_steps = 64
    s5_blk = s5_steps * SUBLANES
    gla_chunk = 64
    assert lp % PROMPT_TILE == 0 and lp % s5_blk == 0 and lp % srows == 0

    regs_mm = (_Region(0, lp, PROMPT_TILE, 1, PROMPT_TILE), _Region(lp, srows, srows, bs, ls))
    regs_nm = (_Region(0, lp, PROMPT_TILE // 2, 1, PROMPT_TILE // 2), _Region(lp, srows, srows, bs, ls))
    regs_cb = (_Region(0, lp, COMB_TILE, 1, COMB_TILE), _Region(lp, srows, COMB_TILE, bs, ls))

    x_all = jnp.concatenate([x_prompt.reshape(lp, d), x_sample.reshape(srows, d)], axis=0)
    c_all = jnp.concatenate([c_prompt, c_sample, jnp.zeros((2 * SUBLANES - 1 - bs, d), F32)], axis=0)
    mod = _ada_call(c_all, w_ada, b_ada)

    def mods(layer, k):
        m = mod[layer, :, k * d:(k + 1) * d]
        return (m[0:1], m[1:1 + bs])

    new_re_p, new_im_p, new_gla_p, new_re_s, new_im_s, new_gla_s = [], [], [], [], [], []
    for i in range(depth):
        sh1, sc1, g1, sh2, sc2, g2 = [mods(i, k) for k in range(6)]
        j = i // 2
        ng1 = norm_g[i, 0].reshape(1, d)
        ng2 = norm_g[i, 1].reshape(1, d)
        if i % 2 == 0:
            (h_all,) = _both(lambda r, prev: _norm_call(
                "plain", r, t_rows, x_all, ng1, sc1[r.nseq > 1], sh1[r.nseq > 1], [], [(d, F32)], prev), regs_nm)
            grp, pst = s5_lam_re.shape[1:]
            ncol = grp * pst
            lam, bbr, bbi = _s5_disc_call(s5_lam_re[j], s5_lam_im[j], s5_log_dt[j], s5_b_re[j], s5_b_im[j], s5_steps)
            bmat, cmat = _s5_block_weights(bbr, bbi, s5_c_re[j], s5_c_im[j])
            kt = bmat.shape[0]
            lamt = jnp.transpose(lam.reshape(4, kt, ncol // kt), (1, 0, 2))
            lamt = jnp.concatenate([lamt, jnp.zeros_like(lamt)], axis=1)
            dsk = s5_d[j].reshape(1, d)
            hp = h_all[:lp].reshape(lp // s5_blk, SUBLANES, s5_steps, d).transpose(0, 2, 1, 3).reshape(lp, d)
            hs = h_all[lp:].reshape(bs, ls, d).transpose(1, 0, 2).reshape(srows, d)
            zeros_st = jnp.zeros((SUBLANES, ncol), F32)
            zp, pre, pim = _s5_call(hp, bmat, cmat, lamt, dsk, zeros_st, zeros_st, s5_steps, True)
            zs, sre, sim = _s5_call(hs, bmat, cmat, lamt, dsk, state_s5_re[j].reshape(bs, ncol),
                                    state_s5_im[j].reshape(bs, ncol), ls, False)
            z_all = jnp.concatenate(
                [zp.reshape(lp // s5_blk, s5_steps, SUBLANES, d).transpose(0, 2, 1, 3).reshape(lp, d),
                 zs.reshape(ls, bs, d).transpose(1, 0, 2).reshape(srows, d)], axis=0)
            new_re_p.append(pre[0].reshape(1, grp, pst))
            new_im_p.append(pim[0].reshape(1, grp, pst))
            new_re_s.append(sre.reshape(bs, grp, pst))
            new_im_s.append(sim.reshape(bs, grp, pst))
            bglu = s5_b_glu[j].reshape(1, d)
            x_all = _both(lambda r, prev: _mm_call(
                "glu", r, t_rows, z_all, [s5_w_glu[j]], [0], d, 512, F32,
                [(bglu, "col"), (z_all, "tile"), (x_all, "tile"), (g1[r.nseq > 1], "seq")], prev, "s5_glu"), regs_mm)
        else:
            hk = gla_w_a2.shape[2]
            assert 2 * hk == d
            h_bf, lg = _both(lambda r, prev: _norm_call(
                "gate", r, t_rows, x_all, ng1, sc1[r.nseq > 1], sh1[r.nseq > 1],
                [gla_w_a1[j], gla_w_a2[j], gla_b_a[j].reshape(1, hk)], [(d, BF16), (hk, F32)], prev), regs_nm)
            proj = _both(lambda r, prev: _mm_call(
                "plain", r, t_rows, h_bf, [gla_w_in[j]], [0], 3 * d, 512, F32, [], prev, "gla_proj"), regs_mm)
            gn = gla_g_norm[j].reshape(1, -1)
            s0p = jnp.zeros((1,) + state_gla.shape[2:], F32)
            o_all, gla_p = _gla_call(0, 1, lp, gla_chunk, t_rows, proj, lg, s0p, gn, None)
            o_all, gla_s = _gla_call(lp, bs, ls, ls, t_rows, proj, lg, state_gla[j], gn, o_all)
            new_gla_p.append(gla_p)
            new_gla_s.append(gla_s)
            x_all = _both(lambda r, prev: _mm_call(
                "res", r, t_rows, o_all, [gla_w_o[j]], [0], d, 512, F32,
                [(x_all, "tile"), (g1[r.nseq > 1], "seq")], prev, "gla_out"), regs_mm)

        brt = moe_b_router[i].reshape(1, n_exp)
        r_p = _router_call(regs_nm[0], t_rows, x_all, ng2, sc2[0], sh2[0], moe_w_router[i], brt,
                           jnp.zeros((SUBLANES, n_exp), F32), None)
        h_pk, h_bf, idx, wts, rank, cnt = _router_call(
            regs_nm[1], t_rows, x_all, ng2, sc2[1], sh2[1], moe_w_router[i], brt, r_p[5], r_p[:5])
        counts = cnt[0].astype(I32)
        ie, rsb, nrb, pad_start, dest, np_rows, n_items = _dispatch_plan(idx, rank, counts, t_rows)
        pos2d = dest.reshape(-1, LANES)
        xs = _dispatch_call(counts, pad_start, pos2d, h_pk, np_rows, DISP_TILE)
        y_sorted = _experts_call(i, ie, rsb, nrb, n_items, xs, moe_w_in, moe_w_out)
        act = _both(lambda r, prev: _mm_call(
            "swiglu", r, t_rows, h_bf, [moe_ws_in[i], moe_ws_in[i]], [0, f_sh // 256], f_sh, 256, BF16,
            [], prev, "shared_in"), regs_mm)
        shared = _both(lambda r, prev: _mm_call(
            "plain", r, t_rows, act, [moe_ws_out[i]], [0], d, 512, F32, [], prev, "shared_out"), regs_mm)
        last = i == depth - 1
        fg = final_g.reshape(1, d)
        x_all = _both(lambda r, prev: _combine_call(
            r, t_rows, pos2d, wts, shared, x_all, g2[r.nseq > 1], fg, y_sorted, last, prev), regs_cb)

    y_prompt = x_all[:lp].reshape(bp, lp, d)
    y_sample = x_all[lp:].reshape(bs, ls, d)
    return (y_prompt, y_sample, jnp.stack(new_re_p), jnp.stack(new_im_p), jnp.stack(new_gla_p),
            jnp.stack(new_re_s), jnp.stack(new_im_s), jnp.stack(new_gla_s))
```

```python
import functools

import numpy as np
import jax
import jax.numpy as jnp
from jax import lax
from jax.experimental import pallas as pl
from jax.experimental.pallas import tpu as pltpu

F32 = jnp.float32
BF16 = jnp.bfloat16
I32 = jnp.int32

EPS = 1e-6
GLA_TAU = 16.0
TOP_K = 8
N_GROUPS = 8
TOPK_GROUPS = 4
ROUTED_SCALE = 2.5

LANES = 128
SUBLANES = 8
MXU_DIM = 256
VMEM_LIMIT = 56 << 20

PROMPT_TILE = 512
MM_TILE = 1024
S5_KTILE = 256
S5_COLS = 512
EXP_ROWBLK = 128
EXP_NBLK = 9
EXP_TK = 1024
EXP_TD = 1024
PACK_GROUP = 512
DISP_TILE = 128
COMB_TILE = 128


def _cparams(n_axes, vmem=VMEM_LIMIT):
    return pltpu.CompilerParams(dimension_semantics=("arbitrary",) * n_axes, vmem_limit_bytes=vmem)


class _Region:
    def __init__(self, row0, rows, tile, nseq, lrep):
        self.row0, self.rows, self.tile, self.nseq, self.lrep = row0, rows, tile, nseq, lrep
        self.blk0 = row0 // tile
        self.ntiles = rows // tile
        assert row0 % tile == 0 and rows % tile == 0


def _expand_rows(m, lrep, rows=None, row0=0):
    nseq, n = m.shape
    if nseq == 1:
        return m
    rows = nseq * lrep if rows is None else rows
    r = lax.broadcasted_iota(I32, (rows, n), 0) + row0
    out = jnp.broadcast_to(m[0:1, :], (rows, n))
    for b in range(1, nseq):
        out = jnp.where(r >= b * lrep, jnp.broadcast_to(m[b:b + 1, :], (rows, n)), out)
    return out


def _sigmoid(x):
    return 1.0 / (1.0 + jnp.exp(-x))


def _silu(x):
    return x * _sigmoid(x)


def _gelu_tanh(x):
    return 0.5 * x * (1.0 + jnp.tanh(0.7978845608028654 * (x + 0.044715 * x * x * x)))


def _log_sigmoid(x):
    return jnp.minimum(x, 0.0) - jnp.log1p(jnp.exp(-jnp.abs(x)))


def _split3(x):
    p1 = x.astype(BF16)
    r1 = x - p1.astype(F32)
    p2 = r1.astype(BF16)
    r2 = r1 - p2.astype(F32)
    return p1, p2, r2.astype(BF16)


def _pack_pairs(x):
    half = PACK_GROUP // 2
    words = []
    for g in range(x.shape[1] // PACK_GROUP):
        lo = pltpu.bitcast(x[:, g * PACK_GROUP:g * PACK_GROUP + half].astype(BF16).astype(F32), jnp.uint32)
        hi = pltpu.bitcast(x[:, g * PACK_GROUP + half:(g + 1) * PACK_GROUP].astype(BF16).astype(F32), jnp.uint32)
        words.append(hi | (lo >> 16))
    return words[0] if len(words) == 1 else jnp.concatenate(words, axis=1)


def _unpack_pairs(w, dtype):
    half = PACK_GROUP // 2
    parts = []
    for g in range(w.shape[1] // half):
        ww = w[:, g * half:(g + 1) * half]
        parts.append(pltpu.bitcast(ww << 16, F32).astype(dtype))
        parts.append(pltpu.bitcast(ww & jnp.uint32(0xFFFF0000), F32).astype(dtype))
    return jnp.concatenate(parts, axis=1)


def _dot(a, b):
    return jnp.dot(a, b, preferred_element_type=F32)


def _dot_t0(a, b):
    return lax.dot_general(a, b, (((0,), (0,)), ((), ())), preferred_element_type=F32)


def _dot_t1(a, b):
    return lax.dot_general(a, b, (((1,), (1,)), ((), ())), preferred_element_type=F32)


def _ada_kernel(c_ref, w_ref, b_ref, o_ref):
    s = _silu(c_ref[...])
    o_ref[0] = _dot(s.astype(BF16), w_ref[0].astype(BF16)) + b_ref[0]


def _ada_call(c_all, w_ada, b_ada):
    depth, d, n6 = w_ada.shape
    nc = c_all.shape[0]
    tn = 512
    return pl.pallas_call(
        _ada_kernel,
        grid=(depth, n6 // tn),
        in_specs=[pl.BlockSpec((nc, d), lambda l, j: (0, 0)),
                  pl.BlockSpec((1, d, tn), lambda l, j: (l, 0, j)),
                  pl.BlockSpec((1, 1, tn), lambda l, j: (l, 0, j))],
        out_specs=pl.BlockSpec((1, nc, tn), lambda l, j: (l, 0, j)),
        out_shape=jax.ShapeDtypeStruct((depth, nc, n6), F32),
        compiler_params=_cparams(2),
        name="adaln",
    )(c_all, w_ada, b_ada.reshape(depth, 1, n6))


def _norm_mod(x, g, sc, sh, lrep):
    ms = jnp.mean(x * x, axis=-1, keepdims=True)
    y = x * lax.rsqrt(ms + EPS) * g
    return y * (1.0 + _expand_rows(sc, lrep)) + _expand_rows(sh, lrep)


def _norm_kernel(x_ref, g_ref, sc_ref, sh_ref, *rest, lrep, aliased):
    o_ref = rest[-1]
    h = _norm_mod(x_ref[...], g_ref[...], sc_ref[...], sh_ref[...], lrep)
    o_ref[...] = h.astype(o_ref.dtype)


def _cols_to_lanes(cols, dtype):
    r = cols[0].shape[0]
    lane = lax.broadcasted_iota(I32, (r, len(cols)), 1)
    out = jnp.zeros((r, len(cols)), dtype)
    for j, c in enumerate(cols):
        out = jnp.where(lane == j, c.astype(dtype), out)
    return out


def _router_kernel(x_ref, g_ref, sc_ref, sh_ref, w_ref, b_ref, cin_ref, *rest, lrep):
    hp_ref, hb_ref, idx_ref, wt_ref, rk_ref, cout_ref, run_ref = rest[-7:]
    i = pl.program_id(0)

    @pl.when(i == 0)
    def _():
        run_ref[...] = cin_ref[0:1, :]

    h = _norm_mod(x_ref[...], g_ref[...], sc_ref[...], sh_ref[...], lrep)
    hb_ref[...] = h.astype(BF16)
    hp_ref[...] = _pack_pairs(h)
    w = w_ref[...]
    hh = h.astype(BF16)
    hl = (h - hh.astype(F32)).astype(BF16)
    wh = w.astype(BF16)
    wl = (w - wh.astype(F32)).astype(BF16)
    scores = _sigmoid(_dot(hh, wh) + _dot(hl, wh) + _dot(hh, wl))

    rows, n_exp = scores.shape
    per = n_exp // N_GROUPS
    neg = -jnp.inf
    big = n_exp + 1
    lane = lax.broadcasted_iota(I32, (rows, n_exp), 1)
    grp = lane // per
    choice = scores + b_ref[...]
    gs = jnp.zeros((rows, n_exp), F32)
    for g in range(N_GROUPS):
        ing = grp == g
        m = jnp.where(ing, choice, neg)
        m1 = jnp.max(m, axis=-1, keepdims=True)
        i1 = jnp.min(jnp.where(m == m1, lane, big), axis=-1, keepdims=True)
        m2 = jnp.max(jnp.where(lane == i1, neg, m), axis=-1, keepdims=True)
        gs = jnp.where(ing, m1 + m2, gs)
    masked = jnp.full((rows, n_exp), neg, F32)
    for _ in range(TOPK_GROUPS):
        mx = jnp.max(gs, axis=-1, keepdims=True)
        gi = jnp.min(jnp.where(gs == mx, grp, big), axis=-1, keepdims=True)
        sel = grp == gi
        masked = jnp.where(sel, choice, masked)
        gs = jnp.where(sel, neg, gs)
    idx_cols, w_cols, hits = [], [], []
    for _ in range(TOP_K):
        mx = jnp.max(masked, axis=-1, keepdims=True)
        ei = jnp.min(jnp.where(masked == mx, lane, big), axis=-1, keepdims=True)
        hit = lane == ei
        hits.append(hit)
        idx_cols.append(ei)
        w_cols.append(jnp.sum(jnp.where(hit, scores, 0.0), axis=-1, keepdims=True))
        masked = jnp.where(hit, neg, masked)
    wsum = w_cols[0]
    for c in w_cols[1:]:
        wsum = wsum + c
    scale = ROUTED_SCALE / wsum
    idx_ref[...] = _cols_to_lanes(idx_cols, I32)
    wt_ref[...] = _cols_to_lanes([c * scale for c in w_cols], F32)
    onehot = jnp.zeros((rows, n_exp), F32)
    for hit in hits:
        onehot = jnp.where(hit, 1.0, onehot)
    rr = lax.broadcasted_iota(I32, (rows, rows), 0)
    cc = lax.broadcasted_iota(I32, (rows, rows), 1)
    before = jnp.where(rr > cc, 1.0, 0.0).astype(BF16)
    cum = _dot(before, onehot.astype(BF16)) + run_ref[...]
    rk_ref[...] = _cols_to_lanes(
        [jnp.sum(jnp.where(hit, cum, 0.0), axis=-1, keepdims=True) for hit in hits], I32)
    run_ref[...] = run_ref[...] + jnp.sum(onehot, axis=0, keepdims=True)
    cout_ref[...] = jnp.broadcast_to(run_ref[...], cout_ref.shape)


def _router_call(reg, t_rows, x_all, g, sc, sh, w_router, b_router, cnt_in, prev):
    d = x_all.shape[1]
    n_exp = w_router.shape[1]
    row = lambda i: (reg.blk0 + i, 0)
    fixed = lambda i: (0, 0)
    in_specs = [pl.BlockSpec((reg.tile, d), row), pl.BlockSpec((1, d), fixed),
                pl.BlockSpec((reg.nseq, d), fixed), pl.BlockSpec((reg.nseq, d), fixed),
                pl.BlockSpec((d, n_exp), fixed), pl.BlockSpec((1, n_exp), fixed),
                pl.BlockSpec((SUBLANES, n_exp), fixed)]
    args = [x_all, g, sc, sh, w_router, b_router, cnt_in]
    outs = [(d // 2, jnp.uint32), (d, BF16), (TOP_K, I32), (TOP_K, F32), (TOP_K, I32)]
    aliases = {}
    if prev is not None:
        for k in range(len(outs)):
            aliases[len(args)] = k
            in_specs.append(pl.BlockSpec(memory_space=pl.ANY))
            args.append(prev[k])
    res = pl.pallas_call(
        functools.partial(_router_kernel, lrep=reg.lrep),
        grid=(reg.ntiles,),
        in_specs=in_specs,
        out_specs=[pl.BlockSpec((reg.tile, n), row) for n, _ in outs] + [pl.BlockSpec((SUBLANES, n_exp), fixed)],
        out_shape=[jax.ShapeDtypeStruct((t_rows, n), dt) for n, dt in outs]
        + [jax.ShapeDtypeStruct((SUBLANES, n_exp), F32)],
        scratch_shapes=[pltpu.VMEM((1, n_exp), F32)],
        input_output_aliases=aliases,
        compiler_params=_cparams(1),
        name="moe_router",
    )(*args)
    return list(res)


def _norm_gate_kernel(x_ref, g_ref, sc_ref, sh_ref, wa1_ref, wa2_ref, ba_ref, *rest, lrep, aliased):
    h_ref, lg_ref = rest[-2:]
    h = _norm_mod(x_ref[...], g_ref[...], sc_ref[...], sh_ref[...], lrep)
    hb = h.astype(BF16)
    h_ref[...] = hb
    a = _dot(hb, wa1_ref[...].astype(BF16))
    z = _dot(a.astype(BF16), wa2_ref[...].astype(BF16)) + ba_ref[...]
    lg_ref[...] = _log_sigmoid(z) * (1.0 / GLA_TAU)


def _norm_call(kind, reg, t_rows, x_all, g, sc, sh, extra, outs, prev):
    d = x_all.shape[1]
    kern = {"plain": _norm_kernel, "gate": _norm_gate_kernel}[kind]
    row = lambda i: (reg.blk0 + i, 0)
    fixed = lambda i: (0, 0)
    in_specs = [pl.BlockSpec((reg.tile, d), row), pl.BlockSpec((1, d), fixed),
                pl.BlockSpec((reg.nseq, d), fixed), pl.BlockSpec((reg.nseq, d), fixed)]
    args = [x_all, g, sc, sh]
    for e in extra:
        in_specs.append(pl.BlockSpec(e.shape, fixed))
        args.append(e)
    aliases = {}
    if prev is not None:
        for k, p in enumerate(prev):
            aliases[len(args)] = k
            in_specs.append(pl.BlockSpec(memory_space=pl.ANY))
            args.append(p)
    res = pl.pallas_call(
        functools.partial(kern, lrep=reg.lrep, aliased=prev is not None),
        grid=(reg.ntiles,),
        in_specs=in_specs,
        out_specs=[pl.BlockSpec((reg.tile, n), row) for n, _ in outs],
        out_shape=[jax.ShapeDtypeStruct((t_rows, n), dt) for n, dt in outs],
        input_output_aliases=aliases,
        compiler_params=_cparams(1),
        name="norm_" + kind,
    )(*args)
    return list(res)


def _s5_disc_kernel(lr_ref, li_ref, ldt_ref, br_ref, bi_ref, lam_ref, bbr_ref, bbi_ref, *, nsteps):
    lr = lr_ref[...]
    li = li_ref[...]
    dt = jnp.exp(ldt_ref[...])
    mag = jnp.exp(lr * dt)
    ar = mag * jnp.cos(li * dt)
    ai = mag * jnp.sin(li * dt)
    den = lr * lr + li * li
    fr = ((ar - 1.0) * lr + ai * li) / den
    fi = (ai * lr - (ar - 1.0) * li) / den
    lam_ref[0] = ar
    lam_ref[1] = ai
    mags = jnp.exp(nsteps * (lr * dt))
    lam_ref[2] = mags * jnp.cos(nsteps * (li * dt))
    lam_ref[3] = mags * jnp.sin(nsteps * (li * dt))
    for h in range(br_ref.shape[0]):
        br = br_ref[h]
        bi = bi_ref[h]
        bbr_ref[h] = fr * br - fi * bi
        bbi_ref[h] = fr * bi + fi * br


def _s5_disc_call(lam_re, lam_im, log_dt, b_re, b_im, nsteps):
    g, p = lam_re.shape
    hg = b_re.shape[2]
    ldt = jnp.broadcast_to(log_dt[:, None], (g, p))
    brt = jnp.transpose(b_re, (2, 0, 1))
    bit = jnp.transpose(b_im, (2, 0, 1))
    return pl.pallas_call(
        functools.partial(_s5_disc_kernel, nsteps=float(nsteps)),
        out_shape=[jax.ShapeDtypeStruct((4, g, p), F32),
                   jax.ShapeDtypeStruct((hg, g, p), F32),
                   jax.ShapeDtypeStruct((hg, g, p), F32)],
        name="s5_disc",
    )(lam_re, lam_im, ldt, brt, bit)


def _s5_block_weights(bbr, bbi, c_re, c_im):
    hg, g, p = bbr.shape
    gt = S5_KTILE // hg
    kt = g // gt
    eye = jnp.eye(gt, dtype=F32)

    def bd_in(b):
        b = b.reshape(hg, kt, gt, p)
        return jnp.einsum("hkgp,gq->kghqp", b, eye).reshape(kt, gt * hg, gt * p)

    def bd_out(c):
        c = c.reshape(kt, gt, hg, p)
        return jnp.einsum("kghp,gq->kqpgh", c, eye).reshape(kt, gt * p, gt * hg)

    bmat = jnp.concatenate([bd_in(bbr), bd_in(bbi)], axis=2).astype(BF16)
    cmat = jnp.concatenate([bd_out(c_re), -bd_out(c_im)], axis=1).astype(BF16)
    return bmat, cmat


def _s5_kernel(h_ref, b_ref, c_ref, lam_ref, d_ref, sre_ref, sim_ref, z_ref, ore_ref, oim_ref,
               bu_ref, st_ref, *, nsteps, chain):
    rb = pl.program_id(1)
    nc = sre_ref.shape[1]

    @pl.when(rb == 0)
    def _():
        st_ref[0] = sre_ref[...]
        st_ref[1] = sim_ref[...]

    u = h_ref[...]
    bu_ref[...] = _dot(u.astype(BF16), b_ref[0])
    rowid = lax.broadcasted_iota(I32, (SUBLANES, S5_COLS), 0)

    for cb in range(nc // S5_COLS):
        c_re = slice(cb * S5_COLS, (cb + 1) * S5_COLS)
        c_im = slice(nc + cb * S5_COLS, nc + (cb + 1) * S5_COLS)
        ar = jnp.broadcast_to(lam_ref[0, 0:1, c_re], (SUBLANES, S5_COLS))
        ai = jnp.broadcast_to(lam_ref[0, 1:2, c_re], (SUBLANES, S5_COLS))

        def step(s, carry, store):
            xr, xi = carry
            r0 = pl.multiple_of(s * SUBLANES, SUBLANES)
            br = bu_ref[pl.ds(r0, SUBLANES), c_re]
            bi = bu_ref[pl.ds(r0, SUBLANES), c_im]
            nxr = ar * xr - ai * xi + br
            nxi = ar * xi + ai * xr + bi
            if store:
                bu_ref[pl.ds(r0, SUBLANES), c_re] = nxr
                bu_ref[pl.ds(r0, SUBLANES), c_im] = nxi
            return nxr, nxi

        if chain:
            zero = jnp.zeros((SUBLANES, S5_COLS), F32)
            er, ei = lax.fori_loop(0, nsteps, functools.partial(step, store=False), (zero, zero))
            asr = lam_ref[0, 2:3, c_re]
            asi = lam_ref[0, 3:4, c_re]
            pr = st_ref[0, 0:1, c_re]
            pi = st_ref[1, 0:1, c_re]
            x0r, x0i = zero, zero
            for j in range(SUBLANES):
                x0r = jnp.where(rowid == j, jnp.broadcast_to(pr, (SUBLANES, S5_COLS)), x0r)
                x0i = jnp.where(rowid == j, jnp.broadcast_to(pi, (SUBLANES, S5_COLS)), x0i)
                nr = asr * pr - asi * pi + er[j:j + 1]
                ni = asr * pi + asi * pr + ei[j:j + 1]
                pr, pi = nr, ni
            st_ref[0, :, c_re] = jnp.broadcast_to(pr, (SUBLANES, S5_COLS))
            st_ref[1, :, c_re] = jnp.broadcast_to(pi, (SUBLANES, S5_COLS))
            lax.fori_loop(0, nsteps, functools.partial(step, store=True), (x0r, x0i))
        else:
            fr, fi = lax.fori_loop(0, nsteps, functools.partial(step, store=True),
                                   (st_ref[0, :, c_re], st_ref[1, :, c_re]))
            st_ref[0, :, c_re] = fr
            st_ref[1, :, c_re] = fi

    y = _dot(bu_ref[...].astype(BF16), c_ref[0]) + d_ref[...] * u
    z_ref[...] = _gelu_tanh(y).astype(z_ref.dtype)

    @pl.when(rb == pl.num_programs(1) - 1)
    def _():
        ore_ref[...] = st_ref[0]
        oim_ref[...] = st_ref[1]


def _s5_call(hp, bmat, cmat, lam, d_skip, s_re, s_im, nsteps, chain):
    rows, d = hp.shape
    kt = bmat.shape[0]
    nc = bmat.shape[2] // 2
    rblk = nsteps * SUBLANES
    return pl.pallas_call(
        functools.partial(_s5_kernel, nsteps=nsteps, chain=chain),
        grid=(kt, rows // rblk),
        in_specs=[pl.BlockSpec((rblk, S5_KTILE), lambda k, r: (r, k)),
                  pl.BlockSpec((1, S5_KTILE, 2 * nc), lambda k, r: (k, 0, 0)),
                  pl.BlockSpec((1, 2 * nc, S5_KTILE), lambda k, r: (k, 0, 0)),
                  pl.BlockSpec((1, SUBLANES, nc), lambda k, r: (k, 0, 0)),
                  pl.BlockSpec((1, S5_KTILE), lambda k, r: (0, k)),
                  pl.BlockSpec((SUBLANES, nc), lambda k, r: (0, k)),
                  pl.BlockSpec((SUBLANES, nc), lambda k, r: (0, k))],
        out_specs=[pl.BlockSpec((rblk, S5_KTILE), lambda k, r: (r, k)),
                   pl.BlockSpec((SUBLANES, nc), lambda k, r: (0, k)),
                   pl.BlockSpec((SUBLANES, nc), lambda k, r: (0, k))],
        out_shape=[jax.ShapeDtypeStruct((rows, d), BF16),
                   jax.ShapeDtypeStruct(s_re.shape, F32),
                   jax.ShapeDtypeStruct(s_im.shape, F32)],
        scratch_shapes=[pltpu.VMEM((rblk, 2 * nc), F32), pltpu.VMEM((2, SUBLANES, nc), F32)],
        compiler_params=_cparams(2),
        name="s5_scan",
    )(hp, bmat, cmat, lam, d_skip, s_re, s_im)


def _mm_kernel(*refs, mode, lrep, n_w, n_alias):
    lhs_ref = refs[0]
    w_refs = refs[1:1 + n_w]
    pos = 1 + n_w
    i = pl.program_id(1)
    wbf = refs[len(refs) - n_w:]
    o_ref = refs[len(refs) - n_w - 1]

    @pl.when(i == 0)
    def _():
        for w_ref, s_ref in zip(w_refs, wbf):
            s_ref[...] = w_ref[...].astype(BF16)

    lhs = lhs_ref[...]
    if mode == "glu":
        b_ref, zt_ref, x_ref, g_ref = refs[pos:pos + 4]
        t = _dot(lhs, wbf[0][...]) + b_ref[...]
        o = zt_ref[...].astype(F32) * _sigmoid(t)
        o_ref[...] = x_ref[...] + _expand_rows(g_ref[...], lrep) * o
    elif mode == "res":
        x_ref, g_ref = refs[pos:pos + 2]
        o_ref[...] = x_ref[...] + _expand_rows(g_ref[...], lrep) * _dot(lhs, wbf[0][...])
    elif mode == "plain":
        o_ref[...] = _dot(lhs, wbf[0][...]).astype(o_ref.dtype)
    elif mode == "swiglu":
        o_ref[...] = (_silu(_dot(lhs, wbf[0][...])) * _dot(lhs, wbf[1][...])).astype(o_ref.dtype)


def _mm_call(mode, reg, t_rows, lhs, w_list, w_colblk0, n_out, tn, out_dtype, extras, prev, name):
    k_dim = lhs.shape[1]
    nj = n_out // tn
    in_specs = [pl.BlockSpec((reg.tile, k_dim), lambda j, i: (reg.blk0 + i, 0))]
    args = [lhs]
    for w, c0 in zip(w_list, w_colblk0):
        in_specs.append(pl.BlockSpec((k_dim, tn), lambda j, i, c0=c0: (0, c0 + j)))
        args.append(w)
    for a, kind in extras:
        if kind == "col":
            in_specs.append(pl.BlockSpec((1, tn), lambda j, i: (0, j)))
        elif kind == "tile":
            in_specs.append(pl.BlockSpec((reg.tile, tn), lambda j, i: (reg.blk0 + i, j)))
        else:
            in_specs.append(pl.BlockSpec((reg.nseq, tn), lambda j, i: (0, j)))
        args.append(a)
    aliases = {}
    if prev is not None:
        aliases[len(args)] = 0
        in_specs.append(pl.BlockSpec(memory_space=pl.ANY))
        args.append(prev)
    return pl.pallas_call(
        functools.partial(_mm_kernel, mode=mode, lrep=reg.lrep, n_w=len(w_list), n_alias=len(aliases)),
        grid=(nj, reg.ntiles),
        in_specs=in_specs,
        out_specs=pl.BlockSpec((reg.tile, tn), lambda j, i: (reg.blk0 + i, j)),
        out_shape=jax.ShapeDtypeStruct((t_rows, n_out), out_dtype),
        scratch_shapes=[pltpu.VMEM((k_dim, tn), BF16) for _ in w_list],
        input_output_aliases=aliases,
        compiler_params=_cparams(2),
        name=name,
    )(*args)


def _gla_consts(chunk):
    nlev = int(np.log2(chunk))
    assert 1 << nlev == chunk
    tri = np.tril(np.ones((chunk, chunk), np.float32))
    r = np.arange(chunk)
    wall, masks = [tri], []
    for l in range(nlev):
        w = chunk >> (l + 1)
        blk = r // (2 * w)
        second = (r & w) != 0
        wall.append(tri[blk * 2 * w + w - 1])
        masks.append(((blk[:, None] == blk[None, :]) & second[:, None] & (~second)[None, :]).astype(np.float32))
    masks.append(np.eye(chunk, dtype=np.float32))
    return jnp.asarray(np.concatenate(wall, 0), BF16), jnp.asarray(np.stack(masks, 0), F32)


def _gla_kernel(q_ref, k_ref, v_ref, gate_ref, lg_ref, s0_ref, wall_ref, mask_ref, gn_ref,
                *rest, chunk, nheads, dk, dv):
    o_ref, sout_ref, s_ref = rest[-3:]
    c = pl.program_id(1)
    nlev = mask_ref.shape[0] - 1

    @pl.when(c == 0)
    def _():
        s_ref[...] = s0_ref[0]

    lg = lg_ref[...]
    p1, p2, p3 = _split3(lg)
    wall = wall_ref[...]
    bg = _dot(wall, p1) + _dot(wall, p2) + _dot(wall, p3)
    b = bg[0:chunk]
    q = q_ref[...] * (dk ** -0.5)
    k = k_ref[...]
    row = lax.broadcasted_iota(I32, q.shape, 0)
    qs, ks = [], []
    for l in range(nlev):
        w = chunk >> (l + 1)
        g = bg[(l + 1) * chunk:(l + 2) * chunk]
        second = (row & w) != 0
        e = jnp.exp(jnp.where(second, b - g, g - b))
        qk = jnp.where(second, q, k) * e
        qs.append(jnp.where(second, qk, 0.0).astype(BF16))
        ks.append(jnp.where(second, 0.0, qk).astype(BF16))
    qb = q.astype(BF16)
    kb = k.astype(BF16)
    q_in = (q * jnp.exp(b)).astype(BF16)
    k_dec = (k * jnp.exp(b[chunk - 1:chunk] - b)).astype(BF16)
    ones = jnp.ones((chunk, LANES), BF16)
    gn = gn_ref[...]
    for h in range(nheads):
        ck = slice(h * dk, (h + 1) * dk)
        cv = slice(h * dv, (h + 1) * dv)
        att = _dot_t1(qb[:, ck], kb[:, ck]) * mask_ref[nlev]
        for l in range(nlev):
            att = att + _dot_t1(qs[l][:, ck], ks[l][:, ck]) * mask_ref[l]
        vh = v_ref[:, cv].astype(BF16)
        s_h = s_ref[h]
        o = _dot(att.astype(BF16), vh) + _dot(q_in[:, ck], s_h.astype(BF16))
        dcol = jnp.exp(_dot_t0(p1[:, ck], ones) + _dot_t0(p2[:, ck], ones) + _dot_t0(p3[:, ck], ones))
        s_ref[h] = jnp.concatenate([dcol] * (dv // LANES), axis=1) * s_h + _dot_t0(k_dec[:, ck], vh)
        ms = jnp.mean(o * o, axis=-1, keepdims=True)
        on = o * lax.rsqrt(ms + EPS) * gn
        o_ref[:, cv] = (on * _silu(gate_ref[:, cv])).astype(o_ref.dtype)

    @pl.when(c == pl.num_programs(1) - 1)
    def _():
        sout_ref[0] = s_ref[...]


def _gla_call(reg_row0, nseq, seqlen, chunk, t_rows, proj, lg, s0, g_norm, prev):
    nheads, dk, dv = s0.shape[1:]
    hk = nheads * dk
    d = nheads * dv
    nch = seqlen // chunk
    rb0 = reg_row0 // chunk
    wall, masks = _gla_consts(chunk)
    rowblk = lambda b, c: rb0 + b * nch + c
    in_specs = [pl.BlockSpec((chunk, hk), lambda b, c: (rowblk(b, c), 0)),
                pl.BlockSpec((chunk, hk), lambda b, c: (rowblk(b, c), 1)),
                pl.BlockSpec((chunk, d), lambda b, c: (rowblk(b, c), 1)),
                pl.BlockSpec((chunk, d), lambda b, c: (rowblk(b, c), 2)),
                pl.BlockSpec((chunk, hk), lambda b, c: (rowblk(b, c), 0)),
                pl.BlockSpec((1, nheads, dk, dv), lambda b, c: (b, 0, 0, 0)),
                pl.BlockSpec(wall.shape, lambda b, c: (0, 0)),
                pl.BlockSpec(masks.shape, lambda b, c: (0, 0, 0)),
                pl.BlockSpec((1, dv), lambda b, c: (0, 0))]
    args = [proj, proj, proj, proj, lg, s0, wall, masks, g_norm]
    aliases = {}
    if prev is not None:
        aliases[len(args)] = 0
        in_specs.append(pl.BlockSpec(memory_space=pl.ANY))
        args.append(prev)
    return pl.pallas_call(
        functools.partial(_gla_kernel, chunk=chunk, nheads=nheads, dk=dk, dv=dv),
        grid=(nseq, nch),
        in_specs=in_specs,
        out_specs=[pl.BlockSpec((chunk, d), lambda b, c: (rowblk(b, c), 0)),
                   pl.BlockSpec((1, nheads, dk, dv), lambda b, c: (b, 0, 0, 0))],
        out_shape=[jax.ShapeDtypeStruct((t_rows, d), BF16),
                   jax.ShapeDtypeStruct(s0.shape, F32)],
        scratch_shapes=[pltpu.VMEM((nheads, dk, dv), F32)],
        input_output_aliases=aliases,
        compiler_params=_cparams(2),
        name="gla_chunk",
    )(*args)


def _dispatch_plan(idx, rank, counts, n_tokens):
    n_experts = counts.shape[0]
    rb = EXP_ROWBLK
    a = n_tokens * TOP_K
    blocks_e = (counts + rb - 1) // rb
    padded = blocks_e * rb
    pad_start = jnp.cumsum(padded) - padded
    hit = idx[:, :, None] == jnp.arange(n_experts, dtype=I32)[None, None, :]
    dest = jnp.sum(jnp.where(hit, pad_start[None, None, :], 0), axis=-1) + rank
    np_rows = (a + rb - 1) // rb * rb + rb * n_experts
    items_e = (blocks_e + EXP_NBLK - 1) // EXP_NBLK
    item_end = jnp.cumsum(items_e)
    item_start = item_end - items_e
    n_items = n_experts + (np_rows // rb) // EXP_NBLK
    ii = jnp.arange(n_items, dtype=I32)
    total = item_end[-1]
    e_of = jnp.minimum(jnp.sum(item_end[None, :] <= ii[:, None], axis=1), n_experts - 1).astype(I32)
    valid = ii < total
    local = ii - item_start[e_of]
    e_last = e_of[jnp.maximum(total - 1, 0)]
    ie = jnp.where(valid, e_of, e_last).astype(I32)
    rsb = jnp.where(valid, pad_start[e_of] // rb + local * EXP_NBLK, 0).astype(I32)
    nrb = jnp.where(valid, jnp.minimum(EXP_NBLK, blocks_e[e_of] - local * EXP_NBLK), 0).astype(I32)
    return ie, rsb, nrb, pad_start.astype(I32), dest.astype(I32), np_rows, n_items


def _dispatch_kernel(cnt_ref, pst_ref, dest_hbm, h_ref, xs_hbm, idx_ref, hbuf, zrow, sem_i, sem_d, sem_z,
                     *, ntiles, n_experts):
    i = pl.program_id(0)
    slot = i % 2
    tile = h_ref.shape[0]
    nidx = tile * TOP_K // LANES
    tok_per_row = LANES // TOP_K
    rb = EXP_ROWBLK

    def idx_copy(t, s):
        return pltpu.make_async_copy(dest_hbm.at[pl.ds(t * nidx, nidx)], idx_ref.at[s], sem_i.at[s])

    def wait_rows(s):
        for _ in range(TOP_K):
            pltpu.make_async_copy(hbuf.at[s], xs_hbm.at[pl.ds(0, tile)], sem_d.at[s]).wait()

    @pl.when(i == 0)
    def _():
        idx_copy(0, 0).start()

    idx_copy(i, slot).wait()

    @pl.when(i + 1 < ntiles)
    def _():
        idx_copy(jnp.minimum(i + 1, ntiles - 1), 1 - slot).start()

    hbuf[slot] = h_ref[...]

    def tok(t, carry):
        for j in range(TOP_K):
            p = idx_ref[slot, t // tok_per_row, (t % tok_per_row) * TOP_K + j]
            pltpu.make_async_copy(hbuf.at[slot, pl.ds(t, 1)], xs_hbm.at[pl.ds(p, 1)], sem_d.at[slot]).start()
        return carry
    lax.fori_loop(0, tile, tok, 0)

    @pl.when(i > 0)
    def _():
        wait_rows(1 - slot)

    @pl.when(i == ntiles - 1)
    def _():
        wait_rows(slot)
        zrow[...] = jnp.zeros(zrow.shape, zrow.dtype)

        def expert(e, carry):
            cnt = cnt_ref[e]
            base = pst_ref[e]
            end = (cnt + rb - 1) // rb * rb

            def zstart(r, c2):
                pltpu.make_async_copy(zrow.at[pl.ds(0, 1)], xs_hbm.at[pl.ds(base + r, 1)], sem_z).start()
                return c2

            def zwait(r, c2):
                pltpu.make_async_copy(zrow.at[pl.ds(0, 1)], xs_hbm.at[pl.ds(0, 1)], sem_z).wait()
                return c2
            lax.fori_loop(cnt, end, zstart, 0)
            lax.fori_loop(cnt, end, zwait, 0)
            return carry
        lax.fori_loop(0, n_experts, expert, 0)


def _dispatch_call(counts, pad_start, dest2d, h_packed, np_rows, tile):
    t_rows, dh = h_packed.shape
    ntiles = t_rows // tile
    n_experts = counts.shape[0]
    grid_spec = pltpu.PrefetchScalarGridSpec(
        num_scalar_prefetch=2,
        grid=(ntiles,),
        in_specs=[pl.BlockSpec(memory_space=pl.ANY),
                  pl.BlockSpec((tile, dh), lambda i, c, p: (i, 0))],
        out_specs=pl.BlockSpec(memory_space=pl.ANY),
        scratch_shapes=[pltpu.SMEM((2, tile * TOP_K // LANES, LANES), I32),
                        pltpu.VMEM((2, tile, dh), jnp.uint32),
                        pltpu.VMEM((SUBLANES, dh), jnp.uint32),
                        pltpu.SemaphoreType.DMA((2,)),
                        pltpu.SemaphoreType.DMA((2,)),
                        pltpu.SemaphoreType.DMA])
    return pl.pallas_call(
        functools.partial(_dispatch_kernel, ntiles=ntiles, n_experts=n_experts),
        grid_spec=grid_spec,
        out_shape=jax.ShapeDtypeStruct((np_rows, dh), jnp.uint32),
        compiler_params=_cparams(1),
        name="moe_dispatch",
    )(counts, pad_start, dest2d, h_packed)


def _experts_kernel(ie_ref, rsb_ref, nrb_ref, xs_hbm, wi_ref, wo_ref, y_hbm,
                    xt, acc, act, ybuf, wibf, wobf, sem_x, sem_y, ycnt_ref, *, n_items, n_k, n_b):
    i = pl.program_id(0)
    st = pl.program_id(1)
    nrb = nrb_ref[i]
    rb = EXP_ROWBLK
    tkw = xt.shape[2]
    tdw = ybuf.shape[2]
    f = act.shape[1]

    def x_copy(item, kt, bi):
        src = pl.multiple_of((rsb_ref[item] + bi) * rb, rb)
        return pltpu.make_async_copy(
            xs_hbm.at[pl.ds(src, rb), pl.ds(pl.multiple_of(kt * tkw, tkw), tkw)],
            xt.at[kt % 2, pl.ds(pl.multiple_of(bi * rb, rb), rb)], sem_x.at[kt % 2])

    def start_x(item, kt):
        def blk(bi, carry):
            x_copy(item, kt, bi).start()
            return carry
        lax.fori_loop(0, nrb_ref[item], blk, 0)

    def wait_x(item, kt):
        def blk(bi, carry):
            x_copy(item, kt, bi).wait()
            return carry
        lax.fori_loop(0, nrb_ref[item], blk, 0)

    def wait_out(s):
        def blk(bi, carry):
            pltpu.make_async_copy(ybuf.at[s, pl.ds(0, rb)], y_hbm.at[pl.ds(0, rb), pl.ds(0, tdw)],
                                  sem_y.at[s]).wait()
            return carry
        lax.fori_loop(0, ycnt_ref[s], blk, 0)
        ycnt_ref[s] = 0

    @pl.when((i == 0) & (st == 0))
    def _():
        ycnt_ref[0] = 0
        ycnt_ref[1] = 0
        xt[...] = jnp.zeros(xt.shape, xt.dtype)
        start_x(0, 0)

    @pl.when(st < n_k)
    def _():
        wait_x(i, st)

    @pl.when(st + 1 < n_k)
    def _():
        start_x(i, st + 1)

    @pl.when((st == n_k) & (i + 1 < n_items))
    def _():
        start_x(jnp.minimum(i + 1, n_items - 1), 0)

    @pl.when((st < n_k) & (nrb > 0))
    def _():
        wibf[...] = wi_ref[0, 0].astype(BF16)
        part = _dot(_unpack_pairs(xt[st % 2], BF16), wibf[...])

        @pl.when(st == 0)
        def _():
            acc[...] = part

        @pl.when(st > 0)
        def _():
            acc[...] = acc[...] + part

        @pl.when(st == n_k - 1)
        def _():
            a = acc[...]
            act[...] = (_silu(a[:, :f]) * a[:, f:]).astype(BF16)

    @pl.when((st >= n_k) & (nrb > 0))
    def _():
        dj = st - n_k
        ys = (i * n_b + dj) % 2
        wobf[...] = wo_ref[0, 0].astype(BF16)
        wait_out(ys)
        ybuf[ys] = _pack_pairs(_dot(act[...], wobf[...]))
        row0 = rsb_ref[i] * rb
        col0 = pl.multiple_of(dj * tdw, tdw)

        def blk(bi, carry):
            r0 = pl.multiple_of(bi * rb, rb)
            pltpu.make_async_copy(ybuf.at[ys, pl.ds(r0, rb)],
                                  y_hbm.at[pl.ds(pl.multiple_of(row0 + r0, rb), rb), pl.ds(col0, tdw)],
                                  sem_y.at[ys]).start()
            return carry
        lax.fori_loop(0, nrb, blk, 0)
        ycnt_ref[ys] = nrb

    @pl.when((i == n_items - 1) & (st == n_k + n_b - 1))
    def _():
        wait_out(0)
        wait_out(1)


def _experts_call(layer, ie, rsb, nrb, n_items, xs, w_in, w_out):
    _, _, d, f2 = w_in.shape
    f = f2 // 2
    np_rows, dh = xs.shape
    tk = min(EXP_TK, d)
    td = min(EXP_TD, d)
    n_k, n_b = d // tk, d // td
    rmax = EXP_NBLK * EXP_ROWBLK

    def kt(st, nr):
        return jnp.where((nr > 0) & (st < n_k), st, n_k - 1)

    def dj(st, nr):
        return jnp.where(nr > 0, jnp.maximum(st - n_k, 0), n_b - 1)

    grid_spec = pltpu.PrefetchScalarGridSpec(
        num_scalar_prefetch=3,
        grid=(n_items, n_k + n_b),
        in_specs=[pl.BlockSpec(memory_space=pl.ANY),
                  pl.BlockSpec((1, 1, tk, f2), lambda i, st, ie, rsb, nrb: (layer, ie[i], kt(st, nrb[i]), 0)),
                  pl.BlockSpec((1, 1, f, td), lambda i, st, ie, rsb, nrb: (layer, ie[i], 0, dj(st, nrb[i])))],
        out_specs=pl.BlockSpec(memory_space=pl.ANY),
        scratch_shapes=[pltpu.VMEM((2, rmax, tk // 2), jnp.uint32),
                        pltpu.VMEM((rmax, f2), F32),
                        pltpu.VMEM((rmax, f), BF16),
                        pltpu.VMEM((2, rmax, td // 2), jnp.uint32),
                        pltpu.VMEM((tk, f2), BF16),
                        pltpu.VMEM((f, td), BF16),
                        pltpu.SemaphoreType.DMA((2,)),
                        pltpu.SemaphoreType.DMA((2,)),
                        pltpu.SMEM((2,), I32)])
    return pl.pallas_call(
        functools.partial(_experts_kernel, n_items=n_items, n_k=n_k, n_b=n_b),
        grid_spec=grid_spec,
        out_shape=jax.ShapeDtypeStruct((np_rows, dh), jnp.uint32),
        compiler_params=_cparams(2),
        name="moe_experts",
    )(ie, rsb, nrb, xs, w_in, w_out)


def _combine_kernel(pos_hbm, w_ref, sh_ref, x_ref, g_ref, fg_ref, y_hbm, *rest,
                    lrep, blk0, ntiles, final):
    o_ref, idx_ref, gbuf, sem_i, sem_g = rest[-5:]
    i = pl.program_id(0)
    slot = i % 2
    nslot = 1 - slot
    tile = x_ref.shape[0]
    nidx = tile * TOP_K // LANES
    tok_per_row = LANES // TOP_K

    def idx_copy(t, s):
        return pltpu.make_async_copy(pos_hbm.at[pl.ds((blk0 + t) * nidx, nidx)], idx_ref.at[s], sem_i.at[s])

    def issue_gather(s):
        def tok(t, carry):
            for j in range(TOP_K):
                p = idx_ref[s, t // tok_per_row, (t % tok_per_row) * TOP_K + j]
                pltpu.make_async_copy(y_hbm.at[pl.ds(p, 1)], gbuf.at[s, j, pl.ds(t, 1)], sem_g.at[s]).start()
            return carry
        lax.fori_loop(0, tile, tok, 0)

    @pl.when(i == 0)
    def _():
        first = idx_copy(0, 0)
        first.start()
        first.wait()
        issue_gather(0)
        if ntiles > 1:
            idx_copy(1, 1).start()

    @pl.when(i + 1 < ntiles)
    def _():
        idx_copy(jnp.minimum(i + 1, ntiles - 1), nslot).wait()

    for j in range(TOP_K):
        pltpu.make_async_copy(y_hbm.at[pl.ds(0, tile)], gbuf.at[slot, j], sem_g.at[slot]).wait()

    @pl.when(i + 1 < ntiles)
    def _():
        issue_gather(nslot)

    @pl.when(i + 2 < ntiles)
    def _():
        idx_copy(jnp.minimum(i + 2, ntiles - 1), slot).start()

    acc = sh_ref[...]
    for j in range(TOP_K):
        acc = acc + w_ref[:, j:j + 1] * _unpack_pairs(gbuf[slot, j], F32)
    o = x_ref[...] + _expand_rows(g_ref[...], lrep, tile, i * tile) * acc
    if final:
        ms = jnp.mean(o * o, axis=-1, keepdims=True)
        o = o * lax.rsqrt(ms + EPS) * fg_ref[...]
    o_ref[...] = o


def _combine_call(reg, t_rows, pos2d, wts, shared, x_all, g, final_g, y_sorted, final, prev):
    d = x_all.shape[1]
    row = lambda i: (reg.blk0 + i, 0)
    fixed = lambda i: (0, 0)
    in_specs = [pl.BlockSpec(memory_space=pl.ANY),
                pl.BlockSpec((reg.tile, TOP_K), row),
                pl.BlockSpec((reg.tile, d), row),
                pl.BlockSpec((reg.tile, d), row),
                pl.BlockSpec((reg.nseq, d), fixed),
                pl.BlockSpec((1, d), fixed),
                pl.BlockSpec(memory_space=pl.ANY)]
    args = [pos2d, wts, shared, x_all, g, final_g, y_sorted]
    aliases = {}
    if prev is not None:
        aliases[len(args)] = 0
        in_specs.append(pl.BlockSpec(memory_space=pl.ANY))
        args.append(prev)
    return pl.pallas_call(
        functools.partial(_combine_kernel, lrep=reg.lrep, blk0=reg.blk0, ntiles=reg.ntiles, final=final),
        grid=(reg.ntiles,),
        in_specs=in_specs,
        out_specs=pl.BlockSpec((reg.tile, d), row),
        out_shape=jax.ShapeDtypeStruct((t_rows, d), F32),
        scratch_shapes=[pltpu.SMEM((2, reg.tile * TOP_K // LANES, LANES), I32),
                        pltpu.VMEM((2, TOP_K, reg.tile, d // 2), jnp.uint32),
                        pltpu.SemaphoreType.DMA((2,)),
                        pltpu.SemaphoreType.DMA((2,))],
        input_output_aliases=aliases,
        compiler_params=_cparams(1),
        name="moe_combine",
    )(*args)


def _both(fn, regs):
    out = fn(regs[0], None)
    return fn(regs[1], out)


def kernel(x_prompt, x_sample, state_s5_re, state_s5_im, state_gla, c_prompt, c_sample, w_ada, b_ada, norm_g, s5_lam_re, s5_lam_im, s5_log_dt, s5_b_re, s5_b_im, s5_c_re, s5_c_im, s5_d, s5_w_glu, s5_b_glu, gla_w_in, gla_w_a1, gla_w_a2, gla_b_a, gla_g_norm, gla_w_o, moe_w_router, moe_b_router, moe_w_in, moe_w_out, moe_ws_in, moe_ws_out, final_g):
    bp, lp, d = x_prompt.shape
    bs, ls, _ = x_sample.shape
    assert bp == 1 and bs == SUBLANES and ls % SUBLANES == 0
    depth = w_ada.shape[0]
    n_exp = moe_w_in.shape[1]
    f_sh = moe_ws_out.shape[1]
    srows = bs * ls
    t_rows = lp + srows
    s5_steps = 64
    s5_blk = s5_steps * SUBLANES
    gla_chunk = 64
    assert lp % PROMPT_TILE == 0 and lp % s5_blk == 0 and lp % srows == 0

    mm_tile = MM_TILE if lp % MM_TILE == 0 else PROMPT_TILE
    regs_mm = (_Region(0, lp, mm_tile, 1, mm_tile), _Region(lp, srows, srows, bs, ls))
    regs_nm = (_Region(0, lp, PROMPT_TILE // 2, 1, PROMPT_TILE // 2), _Region(lp, srows, srows, bs, ls))
    regs_cb = (_Region(0, lp, COMB_TILE, 1, COMB_TILE), _Region(lp, srows, COMB_TILE, bs, ls))

    x_all = jnp.concatenate([x_prompt.reshape(lp, d), x_sample.reshape(srows, d)], axis=0)
    c_all = jnp.concatenate([c_prompt, c_sample, jnp.zeros((2 * SUBLANES - 1 - bs, d), F32)], axis=0)
    mod = _ada_call(c_all, w_ada, b_ada)

    def mods(layer, k):
        m = mod[layer, :, k * d:(k + 1) * d]
        return (m[0:1], m[1:1 + bs])

    new_re_p, new_im_p, new_gla_p, new_re_s, new_im_s, new_gla_s = [], [], [], [], [], []
    for i in range(depth):
        sh1, sc1, g1, sh2, sc2, g2 = [mods(i, k) for k in range(6)]
        j = i // 2
        ng1 = norm_g[i, 0].reshape(1, d)
        ng2 = norm_g[i, 1].reshape(1, d)
        if i % 2 == 0:
            (h_all,) = _both(lambda r, prev: _norm_call(
                "plain", r, t_rows, x_all, ng1, sc1[r.nseq > 1], sh1[r.nseq > 1], [], [(d, F32)], prev), regs_nm)
            grp, pst = s5_lam_re.shape[1:]
            ncol = grp * pst
            lam, bbr, bbi = _s5_disc_call(s5_lam_re[j], s5_lam_im[j], s5_log_dt[j], s5_b_re[j], s5_b_im[j], s5_steps)
            bmat, cmat = _s5_block_weights(bbr, bbi, s5_c_re[j], s5_c_im[j])
            kt = bmat.shape[0]
            lamt = jnp.transpose(lam.reshape(4, kt, ncol // kt), (1, 0, 2))
            lamt = jnp.concatenate([lamt, jnp.zeros_like(lamt)], axis=1)
            dsk = s5_d[j].reshape(1, d)
            hp = h_all[:lp].reshape(lp // s5_blk, SUBLANES, s5_steps, d).transpose(0, 2, 1, 3).reshape(lp, d)
            hs = h_all[lp:].reshape(bs, ls, d).transpose(1, 0, 2).reshape(srows, d)
            zeros_st = jnp.zeros((SUBLANES, ncol), F32)
            zp, pre, pim = _s5_call(hp, bmat, cmat, lamt, dsk, zeros_st, zeros_st, s5_steps, True)
            zs, sre, sim = _s5_call(hs, bmat, cmat, lamt, dsk, state_s5_re[j].reshape(bs, ncol),
                                    state_s5_im[j].reshape(bs, ncol), ls, False)
            z_all = jnp.concatenate(
                [zp.reshape(lp // s5_blk, s5_steps, SUBLANES, d).transpose(0, 2, 1, 3).reshape(lp, d),
                 zs.reshape(ls, bs, d).transpose(1, 0, 2).reshape(srows, d)], axis=0)
            new_re_p.append(pre[0].reshape(1, grp, pst))
            new_im_p.append(pim[0].reshape(1, grp, pst))
            new_re_s.append(sre.reshape(bs, grp, pst))
            new_im_s.append(sim.reshape(bs, grp, pst))
            bglu = s5_b_glu[j].reshape(1, d)
            x_all = _both(lambda r, prev: _mm_call(
                "glu", r, t_rows, z_all, [s5_w_glu[j]], [0], d, 512, F32,
                [(bglu, "col"), (z_all, "tile"), (x_all, "tile"), (g1[r.nseq > 1], "seq")], prev, "s5_glu"), regs_mm)
        else:
            hk = gla_w_a2.shape[2]
            assert 2 * hk == d
            h_bf, lg = _both(lambda r, prev: _norm_call(
                "gate", r, t_rows, x_all, ng1, sc1[r.nseq > 1], sh1[r.nseq > 1],
                [gla_w_a1[j], gla_w_a2[j], gla_b_a[j].reshape(1, hk)], [(d, BF16), (hk, F32)], prev), regs_nm)
            proj = _both(lambda r, prev: _mm_call(
                "plain", r, t_rows, h_bf, [gla_w_in[j]], [0], 3 * d, 512, F32, [], prev, "gla_proj"), regs_mm)
            gn = gla_g_norm[j].reshape(1, -1)
            s0p = jnp.zeros((1,) + state_gla.shape[2:], F32)
            o_all, gla_p = _gla_call(0, 1, lp, gla_chunk, t_rows, proj, lg, s0p, gn, None)
            o_all, gla_s = _gla_call(lp, bs, ls, ls, t_rows, proj, lg, state_gla[j], gn, o_all)
            new_gla_p.append(gla_p)
            new_gla_s.append(gla_s)
            x_all = _both(lambda r, prev: _mm_call(
                "res", r, t_rows, o_all, [gla_w_o[j]], [0], d, 512, F32,
                [(x_all, "tile"), (g1[r.nseq > 1], "seq")], prev, "gla_out"), regs_mm)

        brt = moe_b_router[i].reshape(1, n_exp)
        r_p = _router_call(regs_nm[0], t_rows, x_all, ng2, sc2[0], sh2[0], moe_w_router[i], brt,
                           jnp.zeros((SUBLANES, n_exp), F32), None)
        h_pk, h_bf, idx, wts, rank, cnt = _router_call(
            regs_nm[1], t_rows, x_all, ng2, sc2[1], sh2[1], moe_w_router[i], brt, r_p[5], r_p[:5])
        counts = cnt[0].astype(I32)
        ie, rsb, nrb, pad_start, dest, np_rows, n_items = _dispatch_plan(idx, rank, counts, t_rows)
        pos2d = dest.reshape(-1, LANES)
        xs = _dispatch_call(counts, pad_start, pos2d, h_pk, np_rows, DISP_TILE)
        y_sorted = _experts_call(i, ie, rsb, nrb, n_items, xs, moe_w_in, moe_w_out)
        act = _both(lambda r, prev: _mm_call(
            "swiglu", r, t_rows, h_bf, [moe_ws_in[i], moe_ws_in[i]], [0, f_sh // 256], f_sh, 256, BF16,
            [], prev, "shared_in"), regs_mm)
        shared = _both(lambda r, prev: _mm_call(
            "plain", r, t_rows, act, [moe_ws_out[i]], [0], d, 512, F32, [], prev, "shared_out"), regs_mm)
        last = i == depth - 1
        fg = final_g.reshape(1, d)
        x_all = _both(lambda r, prev: _combine_call(
            r, t_rows, pos2d, wts, shared, x_all, g2[r.nseq > 1], fg, y_sorted, last, prev), regs_cb)

    y_prompt = x_all[:lp].reshape(bp, lp, d)
    y_sample = x_all[lp:].reshape(bs, ls, d)
    return (y_prompt, y_sample, jnp.stack(new_re_p), jnp.stack(new_im_p), jnp.stack(new_gla_p),
            jnp.stack(new_re_s), jnp.stack(new_im_s), jnp.stack(new_gla_s))
```

```python
import functools

import numpy as np
import jax
import jax.numpy as jnp
from jax import lax
from jax.experimental import pallas as pl
from jax.experimental.pallas import tpu as pltpu

F32 = jnp.float32
BF16 = jnp.bfloat16
I32 = jnp.int32

EPS = 1e-6
GLA_TAU = 16.0
TOP_K = 8
N_GROUPS = 8
TOPK_GROUPS = 4
ROUTED_SCALE = 2.5

LANES = 128
SUBLANES = 8
MXU_DIM = 256
VMEM_LIMIT = 56 << 20

PROMPT_TILE = 512
MM_TILE = 1024
S5_KTILE = 256
S5_COLS = 512
EXP_ROWBLK = 128
EXP_NBLK = 18
EXP_CHUNK_BLKS = 2
EXP_TK = 1024
EXP_TD = 1024
PACK_GROUP = 512
DISP_TILE = 128
COMB_TILE = 128


def _cparams(n_axes, vmem=VMEM_LIMIT):
    return pltpu.CompilerParams(dimension_semantics=("arbitrary",) * n_axes, vmem_limit_bytes=vmem)


class _Region:
    def __init__(self, row0, rows, tile, nseq, lrep):
        self.row0, self.rows, self.tile, self.nseq, self.lrep = row0, rows, tile, nseq, lrep
        self.blk0 = row0 // tile
        self.ntiles = rows // tile
        assert row0 % tile == 0 and rows % tile == 0


def _expand_rows(m, lrep, rows=None, row0=0):
    nseq, n = m.shape
    if nseq == 1:
        return m
    rows = nseq * lrep if rows is None else rows
    r = lax.broadcasted_iota(I32, (rows, n), 0) + row0
    out = jnp.broadcast_to(m[0:1, :], (rows, n))
    for b in range(1, nseq):
        out = jnp.where(r >= b * lrep, jnp.broadcast_to(m[b:b + 1, :], (rows, n)), out)
    return out


def _sigmoid(x):
    return 1.0 / (1.0 + jnp.exp(-x))


def _silu(x):
    return x * _sigmoid(x)


def _gelu_tanh(x):
    return 0.5 * x * (1.0 + jnp.tanh(0.7978845608028654 * (x + 0.044715 * x * x * x)))


def _log_sigmoid(x):
    return jnp.minimum(x, 0.0) - jnp.log1p(jnp.exp(-jnp.abs(x)))


def _split3(x):
    p1 = x.astype(BF16)
    r1 = x - p1.astype(F32)
    p2 = r1.astype(BF16)
    r2 = r1 - p2.astype(F32)
    return p1, p2, r2.astype(BF16)


def _pack_pairs(x):
    half = PACK_GROUP // 2
    words = []
    for g in range(x.shape[1] // PACK_GROUP):
        lo = pltpu.bitcast(x[:, g * PACK_GROUP:g * PACK_GROUP + half].astype(BF16).astype(F32), jnp.uint32)
        hi = pltpu.bitcast(x[:, g * PACK_GROUP + half:(g + 1) * PACK_GROUP].astype(BF16).astype(F32), jnp.uint32)
        words.append(hi | (lo >> 16))
    return words[0] if len(words) == 1 else jnp.concatenate(words, axis=1)


def _unpack_pairs(w, dtype):
    half = PACK_GROUP // 2
    parts = []
    for g in range(w.shape[1] // half):
        ww = w[:, g * half:(g + 1) * half]
        parts.append(pltpu.bitcast(ww << 16, F32).astype(dtype))
        parts.append(pltpu.bitcast(ww & jnp.uint32(0xFFFF0000), F32).astype(dtype))
    return jnp.concatenate(parts, axis=1)


def _dot(a, b):
    return jnp.dot(a, b, preferred_element_type=F32)


def _dot_t0(a, b):
    return lax.dot_general(a, b, (((0,), (0,)), ((), ())), preferred_element_type=F32)


def _dot_t1(a, b):
    return lax.dot_general(a, b, (((1,), (1,)), ((), ())), preferred_element_type=F32)


def _ada_kernel(c_ref, w_ref, b_ref, o_ref):
    s = _silu(c_ref[...])
    o_ref[0] = _dot(s.astype(BF16), w_ref[0].astype(BF16)) + b_ref[0]


def _ada_call(c_all, w_ada, b_ada):
    depth, d, n6 = w_ada.shape
    nc = c_all.shape[0]
    tn = 512
    return pl.pallas_call(
        _ada_kernel,
        grid=(depth, n6 // tn),
        in_specs=[pl.BlockSpec((nc, d), lambda l, j: (0, 0)),
                  pl.BlockSpec((1, d, tn), lambda l, j: (l, 0, j)),
                  pl.BlockSpec((1, 1, tn), lambda l, j: (l, 0, j))],
        out_specs=pl.BlockSpec((1, nc, tn), lambda l, j: (l, 0, j)),
        out_shape=jax.ShapeDtypeStruct((depth, nc, n6), F32),
        compiler_params=_cparams(2),
        name="adaln",
    )(c_all, w_ada, b_ada.reshape(depth, 1, n6))


def _norm_mod(x, g, sc, sh, lrep):
    ms = jnp.mean(x * x, axis=-1, keepdims=True)
    y = x * lax.rsqrt(ms + EPS) * g
    return y * (1.0 + _expand_rows(sc, lrep)) + _expand_rows(sh, lrep)


def _norm_kernel(x_ref, g_ref, sc_ref, sh_ref, *rest, lrep, aliased):
    o_ref = rest[-1]
    h = _norm_mod(x_ref[...], g_ref[...], sc_ref[...], sh_ref[...], lrep)
    o_ref[...] = h.astype(o_ref.dtype)


def _cols_to_lanes(cols, dtype):
    r = cols[0].shape[0]
    lane = lax.broadcasted_iota(I32, (r, len(cols)), 1)
    out = jnp.zeros((r, len(cols)), dtype)
    for j, c in enumerate(cols):
        out = jnp.where(lane == j, c.astype(dtype), out)
    return out


def _router_kernel(x_ref, g_ref, sc_ref, sh_ref, w_ref, b_ref, cin_ref, *rest, lrep):
    hp_ref, hb_ref, idx_ref, wt_ref, rk_ref, cout_ref, run_ref = rest[-7:]
    i = pl.program_id(0)

    @pl.when(i == 0)
    def _():
        run_ref[...] = cin_ref[0:1, :]

    h = _norm_mod(x_ref[...], g_ref[...], sc_ref[...], sh_ref[...], lrep)
    hb_ref[...] = h.astype(BF16)
    hp_ref[...] = _pack_pairs(h)
    w = w_ref[...]
    hh = h.astype(BF16)
    hl = (h - hh.astype(F32)).astype(BF16)
    wh = w.astype(BF16)
    wl = (w - wh.astype(F32)).astype(BF16)
    scores = _sigmoid(_dot(hh, wh) + _dot(hl, wh) + _dot(hh, wl))

    rows, n_exp = scores.shape
    per = n_exp // N_GROUPS
    neg = -jnp.inf
    big = n_exp + 1
    lane = lax.broadcasted_iota(I32, (rows, n_exp), 1)
    grp = lane // per
    choice = scores + b_ref[...]
    gs = jnp.zeros((rows, n_exp), F32)
    for g in range(N_GROUPS):
        ing = grp == g
        m = jnp.where(ing, choice, neg)
        m1 = jnp.max(m, axis=-1, keepdims=True)
        i1 = jnp.min(jnp.where(m == m1, lane, big), axis=-1, keepdims=True)
        m2 = jnp.max(jnp.where(lane == i1, neg, m), axis=-1, keepdims=True)
        gs = jnp.where(ing, m1 + m2, gs)
    masked = jnp.full((rows, n_exp), neg, F32)
    for _ in range(TOPK_GROUPS):
        mx = jnp.max(gs, axis=-1, keepdims=True)
        gi = jnp.min(jnp.where(gs == mx, grp, big), axis=-1, keepdims=True)
        sel = grp == gi
        masked = jnp.where(sel, choice, masked)
        gs = jnp.where(sel, neg, gs)
    idx_cols, w_cols, hits = [], [], []
    for _ in range(TOP_K):
        mx = jnp.max(masked, axis=-1, keepdims=True)
        ei = jnp.min(jnp.where(masked == mx, lane, big), axis=-1, keepdims=True)
        hit = lane == ei
        hits.append(hit)
        idx_cols.append(ei)
        w_cols.append(jnp.sum(jnp.where(hit, scores, 0.0), axis=-1, keepdims=True))
        masked = jnp.where(hit, neg, masked)
    wsum = w_cols[0]
    for c in w_cols[1:]:
        wsum = wsum + c
    scale = ROUTED_SCALE / wsum
    idx_ref[...] = _cols_to_lanes(idx_cols, I32)
    wt_ref[...] = _cols_to_lanes([c * scale for c in w_cols], F32)
    onehot = jnp.zeros((rows, n_exp), F32)
    for hit in hits:
        onehot = jnp.where(hit, 1.0, onehot)
    rr = lax.broadcasted_iota(I32, (rows, rows), 0)
    cc = lax.broadcasted_iota(I32, (rows, rows), 1)
    before = jnp.where(rr > cc, 1.0, 0.0).astype(BF16)
    cum = _dot(before, onehot.astype(BF16)) + run_ref[...]
    rk_ref[...] = _cols_to_lanes(
        [jnp.sum(jnp.where(hit, cum, 0.0), axis=-1, keepdims=True) for hit in hits], I32)
    run_ref[...] = run_ref[...] + jnp.sum(onehot, axis=0, keepdims=True)
    cout_ref[...] = jnp.broadcast_to(run_ref[...], cout_ref.shape)


def _router_call(reg, t_rows, x_all, g, sc, sh, w_router, b_router, cnt_in, prev):
    d = x_all.shape[1]
    n_exp = w_router.shape[1]
    row = lambda i: (reg.blk0 + i, 0)
    fixed = lambda i: (0, 0)
    in_specs = [pl.BlockSpec((reg.tile, d), row), pl.BlockSpec((1, d), fixed),
                pl.BlockSpec((reg.nseq, d), fixed), pl.BlockSpec((reg.nseq, d), fixed),
                pl.BlockSpec((d, n_exp), fixed), pl.BlockSpec((1, n_exp), fixed),
                pl.BlockSpec((SUBLANES, n_exp), fixed)]
    args = [x_all, g, sc, sh, w_router, b_router, cnt_in]
    outs = [(d // 2, jnp.uint32), (d, BF16), (TOP_K, I32), (TOP_K, F32), (TOP_K, I32)]
    aliases = {}
    if prev is not None:
        for k in range(len(outs)):
            aliases[len(args)] = k
            in_specs.append(pl.BlockSpec(memory_space=pl.ANY))
            args.append(prev[k])
    res = pl.pallas_call(
        functools.partial(_router_kernel, lrep=reg.lrep),
        grid=(reg.ntiles,),
        in_specs=in_specs,
        out_specs=[pl.BlockSpec((reg.tile, n), row) for n, _ in outs] + [pl.BlockSpec((SUBLANES, n_exp), fixed)],
        out_shape=[jax.ShapeDtypeStruct((t_rows, n), dt) for n, dt in outs]
        + [jax.ShapeDtypeStruct((SUBLANES, n_exp), F32)],
        scratch_shapes=[pltpu.VMEM((1, n_exp), F32)],
        input_output_aliases=aliases,
        compiler_params=_cparams(1),
        name="moe_router",
    )(*args)
    return list(res)


def _norm_gate_kernel(x_ref, g_ref, sc_ref, sh_ref, wa1_ref, wa2_ref, ba_ref, *rest, lrep, aliased):
    h_ref, lg_ref = rest[-2:]
    h = _norm_mod(x_ref[...], g_ref[...], sc_ref[...], sh_ref[...], lrep)
    hb = h.astype(BF16)
    h_ref[...] = hb
    a = _dot(hb, wa1_ref[...].astype(BF16))
    z = _dot(a.astype(BF16), wa2_ref[...].astype(BF16)) + ba_ref[...]
    lg_ref[...] = _log_sigmoid(z) * (1.0 / GLA_TAU)


def _norm_call(kind, reg, t_rows, x_all, g, sc, sh, extra, outs, prev):
    d = x_all.shape[1]
    kern = {"plain": _norm_kernel, "gate": _norm_gate_kernel}[kind]
    row = lambda i: (reg.blk0 + i, 0)
    fixed = lambda i: (0, 0)
    in_specs = [pl.BlockSpec((reg.tile, d), row), pl.BlockSpec((1, d), fixed),
                pl.BlockSpec((reg.nseq, d), fixed), pl.BlockSpec((reg.nseq, d), fixed)]
    args = [x_all, g, sc, sh]
    for e in extra:
        in_specs.append(pl.BlockSpec(e.shape, fixed))
        args.append(e)
    aliases = {}
    if prev is not None:
        for k, p in enumerate(prev):
            aliases[len(args)] = k
            in_specs.append(pl.BlockSpec(memory_space=pl.ANY))
            args.append(p)
    res = pl.pallas_call(
        functools.partial(kern, lrep=reg.lrep, aliased=prev is not None),
        grid=(reg.ntiles,),
        in_specs=in_specs,
        out_specs=[pl.BlockSpec((reg.tile, n), row) for n, _ in outs],
        out_shape=[jax.ShapeDtypeStruct((t_rows, n), dt) for n, dt in outs],
        input_output_aliases=aliases,
        compiler_params=_cparams(1),
        name="norm_" + kind,
    )(*args)
    return list(res)


def _s5_disc_kernel(lr_ref, li_ref, ldt_ref, br_ref, bi_ref, lam_ref, bbr_ref, bbi_ref, *, nsteps):
    lr = lr_ref[...]
    li = li_ref[...]
    dt = jnp.exp(ldt_ref[...])
    mag = jnp.exp(lr * dt)
    ar = mag * jnp.cos(li * dt)
    ai = mag * jnp.sin(li * dt)
    den = lr * lr + li * li
    fr = ((ar - 1.0) * lr + ai * li) / den
    fi = (ai * lr - (ar - 1.0) * li) / den
    lam_ref[0] = ar
    lam_ref[1] = ai
    mags = jnp.exp(nsteps * (lr * dt))
    lam_ref[2] = mags * jnp.cos(nsteps * (li * dt))
    lam_ref[3] = mags * jnp.sin(nsteps * (li * dt))
    for h in range(br_ref.shape[0]):
        br = br_ref[h]
        bi = bi_ref[h]
        bbr_ref[h] = fr * br - fi * bi
        bbi_ref[h] = fr * bi + fi * br


def _s5_disc_call(lam_re, lam_im, log_dt, b_re, b_im, nsteps):
    g, p = lam_re.shape
    hg = b_re.shape[2]
    ldt = jnp.broadcast_to(log_dt[:, None], (g, p))
    brt = jnp.transpose(b_re, (2, 0, 1))
    bit = jnp.transpose(b_im, (2, 0, 1))
    return pl.pallas_call(
        functools.partial(_s5_disc_kernel, nsteps=float(nsteps)),
        out_shape=[jax.ShapeDtypeStruct((4, g, p), F32),
                   jax.ShapeDtypeStruct((hg, g, p), F32),
                   jax.ShapeDtypeStruct((hg, g, p), F32)],
        name="s5_disc",
    )(lam_re, lam_im, ldt, brt, bit)


def _s5_block_weights(bbr, bbi, c_re, c_im):
    hg, g, p = bbr.shape
    gt = S5_KTILE // hg
    kt = g // gt
    eye = jnp.eye(gt, dtype=F32)

    def bd_in(b):
        b = b.reshape(hg, kt, gt, p)
        return jnp.einsum("hkgp,gq->kghqp", b, eye).reshape(kt, gt * hg, gt * p)

    def bd_out(c):
        c = c.reshape(kt, gt, hg, p)
        return jnp.einsum("kghp,gq->kqpgh", c, eye).reshape(kt, gt * p, gt * hg)

    bmat = jnp.concatenate([bd_in(bbr), bd_in(bbi)], axis=2).astype(BF16)
    cmat = jnp.concatenate([bd_out(c_re), -bd_out(c_im)], axis=1).astype(BF16)
    return bmat, cmat


def _s5_kernel(h_ref, b_ref, c_ref, lam_ref, d_ref, sre_ref, sim_ref, z_ref, ore_ref, oim_ref,
               bu_ref, st_ref, *, nsteps, chain):
    rb = pl.program_id(1)
    nc = sre_ref.shape[1]

    @pl.when(rb == 0)
    def _():
        st_ref[0] = sre_ref[...]
        st_ref[1] = sim_ref[...]

    u = h_ref[...]
    bu_ref[...] = _dot(u.astype(BF16), b_ref[0])
    rowid = lax.broadcasted_iota(I32, (SUBLANES, S5_COLS), 0)

    for cb in range(nc // S5_COLS):
        c_re = slice(cb * S5_COLS, (cb + 1) * S5_COLS)
        c_im = slice(nc + cb * S5_COLS, nc + (cb + 1) * S5_COLS)
        ar = jnp.broadcast_to(lam_ref[0, 0:1, c_re], (SUBLANES, S5_COLS))
        ai = jnp.broadcast_to(lam_ref[0, 1:2, c_re], (SUBLANES, S5_COLS))

        def step(s, carry, store):
            xr, xi = carry
            r0 = pl.multiple_of(s * SUBLANES, SUBLANES)
            br = bu_ref[pl.ds(r0, SUBLANES), c_re]
            bi = bu_ref[pl.ds(r0, SUBLANES), c_im]
            nxr = ar * xr - ai * xi + br
            nxi = ar * xi + ai * xr + bi
            if store:
                bu_ref[pl.ds(r0, SUBLANES), c_re] = nxr
                bu_ref[pl.ds(r0, SUBLANES), c_im] = nxi
            return nxr, nxi

        if chain:
            zero = jnp.zeros((SUBLANES, S5_COLS), F32)
            er, ei = lax.fori_loop(0, nsteps, functools.partial(step, store=False), (zero, zero))
            asr = lam_ref[0, 2:3, c_re]
            asi = lam_ref[0, 3:4, c_re]
            pr = st_ref[0, 0:1, c_re]
            pi = st_ref[1, 0:1, c_re]
            x0r, x0i = zero, zero
            for j in range(SUBLANES):
                x0r = jnp.where(rowid == j, jnp.broadcast_to(pr, (SUBLANES, S5_COLS)), x0r)
                x0i = jnp.where(rowid == j, jnp.broadcast_to(pi, (SUBLANES, S5_COLS)), x0i)
                nr = asr * pr - asi * pi + er[j:j + 1]
                ni = asr * pi + asi * pr + ei[j:j + 1]
                pr, pi = nr, ni
            st_ref[0, :, c_re] = jnp.broadcast_to(pr, (SUBLANES, S5_COLS))
            st_ref[1, :, c_re] = jnp.broadcast_to(pi, (SUBLANES, S5_COLS))
            lax.fori_loop(0, nsteps, functools.partial(step, store=True), (x0r, x0i))
        else:
            fr, fi = lax.fori_loop(0, nsteps, functools.partial(step, store=True),
                                   (st_ref[0, :, c_re], st_ref[1, :, c_re]))
            st_ref[0, :, c_re] = fr
            st_ref[1, :, c_re] = fi

    y = _dot(bu_ref[...].astype(BF16), c_ref[0]) + d_ref[...] * u
    z_ref[...] = _gelu_tanh(y).astype(z_ref.dtype)

    @pl.when(rb == pl.num_programs(1) - 1)
    def _():
        ore_ref[...] = st_ref[0]
        oim_ref[...] = st_ref[1]


def _s5_call(hp, bmat, cmat, lam, d_skip, s_re, s_im, nsteps, chain):
    rows, d = hp.shape
    kt = bmat.shape[0]
    nc = bmat.shape[2] // 2
    rblk = nsteps * SUBLANES
    return pl.pallas_call(
        functools.partial(_s5_kernel, nsteps=nsteps, chain=chain),
        grid=(kt, rows // rblk),
        in_specs=[pl.BlockSpec((rblk, S5_KTILE), lambda k, r: (r, k)),
                  pl.BlockSpec((1, S5_KTILE, 2 * nc), lambda k, r: (k, 0, 0)),
                  pl.BlockSpec((1, 2 * nc, S5_KTILE), lambda k, r: (k, 0, 0)),
                  pl.BlockSpec((1, SUBLANES, nc), lambda k, r: (k, 0, 0)),
                  pl.BlockSpec((1, S5_KTILE), lambda k, r: (0, k)),
                  pl.BlockSpec((SUBLANES, nc), lambda k, r: (0, k)),
                  pl.BlockSpec((SUBLANES, nc), lambda k, r: (0, k))],
        out_specs=[pl.BlockSpec((rblk, S5_KTILE), lambda k, r: (r, k)),
                   pl.BlockSpec((SUBLANES, nc), lambda k, r: (0, k)),
                   pl.BlockSpec((SUBLANES, nc), lambda k, r: (0, k))],
        out_shape=[jax.ShapeDtypeStruct((rows, d), BF16),
                   jax.ShapeDtypeStruct(s_re.shape, F32),
                   jax.ShapeDtypeStruct(s_im.shape, F32)],
        scratch_shapes=[pltpu.VMEM((rblk, 2 * nc), F32), pltpu.VMEM((2, SUBLANES, nc), F32)],
        compiler_params=_cparams(2),
        name="s5_scan",
    )(hp, bmat, cmat, lam, d_skip, s_re, s_im)


def _mm_kernel(*refs, mode, lrep, n_w, n_alias):
    lhs_ref = refs[0]
    w_refs = refs[1:1 + n_w]
    pos = 1 + n_w
    i = pl.program_id(1)
    wbf = refs[len(refs) - n_w:]
    o_ref = refs[len(refs) - n_w - 1]

    @pl.when(i == 0)
    def _():
        for w_ref, s_ref in zip(w_refs, wbf):
            s_ref[...] = w_ref[...].astype(BF16)

    lhs = lhs_ref[...]
    if mode == "glu":
        b_ref, zt_ref, x_ref, g_ref = refs[pos:pos + 4]
        t = _dot(lhs, wbf[0][...]) + b_ref[...]
        o = zt_ref[...].astype(F32) * _sigmoid(t)
        o_ref[...] = x_ref[...] + _expand_rows(g_ref[...], lrep) * o
    elif mode == "res":
        x_ref, g_ref = refs[pos:pos + 2]
        o_ref[...] = x_ref[...] + _expand_rows(g_ref[...], lrep) * _dot(lhs, wbf[0][...])
    elif mode == "plain":
        o_ref[...] = _dot(lhs, wbf[0][...]).astype(o_ref.dtype)
    elif mode == "swiglu":
        o_ref[...] = (_silu(_dot(lhs, wbf[0][...])) * _dot(lhs, wbf[1][...])).astype(o_ref.dtype)


def _mm_call(mode, reg, t_rows, lhs, w_list, w_colblk0, n_out, tn, out_dtype, extras, prev, name):
    k_dim = lhs.shape[1]
    nj = n_out // tn
    in_specs = [pl.BlockSpec((reg.tile, k_dim), lambda j, i: (reg.blk0 + i, 0))]
    args = [lhs]
    for w, c0 in zip(w_list, w_colblk0):
        in_specs.append(pl.BlockSpec((k_dim, tn), lambda j, i, c0=c0: (0, c0 + j)))
        args.append(w)
    for a, kind in extras:
        if kind == "col":
            in_specs.append(pl.BlockSpec((1, tn), lambda j, i: (0, j)))
        elif kind == "tile":
            in_specs.append(pl.BlockSpec((reg.tile, tn), lambda j, i: (reg.blk0 + i, j)))
        else:
            in_specs.append(pl.BlockSpec((reg.nseq, tn), lambda j, i: (0, j)))
        args.append(a)
    aliases = {}
    if prev is not None:
        aliases[len(args)] = 0
        in_specs.append(pl.BlockSpec(memory_space=pl.ANY))
        args.append(prev)
    return pl.pallas_call(
        functools.partial(_mm_kernel, mode=mode, lrep=reg.lrep, n_w=len(w_list), n_alias=len(aliases)),
        grid=(nj, reg.ntiles),
        in_specs=in_specs,
        out_specs=pl.BlockSpec((reg.tile, tn), lambda j, i: (reg.blk0 + i, j)),
        out_shape=jax.ShapeDtypeStruct((t_rows, n_out), out_dtype),
        scratch_shapes=[pltpu.VMEM((k_dim, tn), BF16) for _ in w_list],
        input_output_aliases=aliases,
        compiler_params=_cparams(2),
        name=name,
    )(*args)


def _gla_consts(chunk):
    nlev = int(np.log2(chunk))
    assert 1 << nlev == chunk
    tri = np.tril(np.ones((chunk, chunk), np.float32))
    r = np.arange(chunk)
    wall, masks = [tri], []
    for l in range(nlev):
        w = chunk >> (l + 1)
        blk = r // (2 * w)
        second = (r & w) != 0
        wall.append(tri[blk * 2 * w + w - 1])
        masks.append(((blk[:, None] == blk[None, :]) & second[:, None] & (~second)[None, :]).astype(np.float32))
    masks.append(np.eye(chunk, dtype=np.float32))
    return jnp.asarray(np.concatenate(wall, 0), BF16), jnp.asarray(np.stack(masks, 0), F32)


def _gla_kernel(q_ref, k_ref, v_ref, gate_ref, lg_ref, s0_ref, wall_ref, mask_ref, gn_ref,
                *rest, chunk, nheads, dk, dv):
    o_ref, sout_ref, s_ref = rest[-3:]
    c = pl.program_id(1)
    nlev = mask_ref.shape[0] - 1

    @pl.when(c == 0)
    def _():
        s_ref[...] = s0_ref[0]

    lg = lg_ref[...]
    p1, p2, p3 = _split3(lg)
    wall = wall_ref[...]
    bg = _dot(wall, p1) + _dot(wall, p2) + _dot(wall, p3)
    b = bg[0:chunk]
    q = q_ref[...] * (dk ** -0.5)
    k = k_ref[...]
    row = lax.broadcasted_iota(I32, q.shape, 0)
    qs, ks = [], []
    for l in range(nlev):
        w = chunk >> (l + 1)
        g = bg[(l + 1) * chunk:(l + 2) * chunk]
        second = (row & w) != 0
        e = jnp.exp(jnp.where(second, b - g, g - b))
        qk = jnp.where(second, q, k) * e
        qs.append(jnp.where(second, qk, 0.0).astype(BF16))
        ks.append(jnp.where(second, 0.0, qk).astype(BF16))
    qb = q.astype(BF16)
    kb = k.astype(BF16)
    q_in = (q * jnp.exp(b)).astype(BF16)
    k_dec = (k * jnp.exp(b[chunk - 1:chunk] - b)).astype(BF16)
    ones = jnp.ones((chunk, LANES), BF16)
    gn = gn_ref[...]
    for h in range(nheads):
        ck = slice(h * dk, (h + 1) * dk)
        cv = slice(h * dv, (h + 1) * dv)
        att = _dot_t1(qb[:, ck], kb[:, ck]) * mask_ref[nlev]
        for l in range(nlev):
            att = att + _dot_t1(qs[l][:, ck], ks[l][:, ck]) * mask_ref[l]
        vh = v_ref[:, cv].astype(BF16)
        s_h = s_ref[h]
        o = _dot(att.astype(BF16), vh) + _dot(q_in[:, ck], s_h.astype(BF16))
        dcol = jnp.exp(_dot_t0(p1[:, ck], ones) + _dot_t0(p2[:, ck], ones) + _dot_t0(p3[:, ck], ones))
        s_ref[h] = jnp.concatenate([dcol] * (dv // LANES), axis=1) * s_h + _dot_t0(k_dec[:, ck], vh)
        ms = jnp.mean(o * o, axis=-1, keepdims=True)
        on = o * lax.rsqrt(ms + EPS) * gn
        o_ref[:, cv] = (on * _silu(gate_ref[:, cv])).astype(o_ref.dtype)

    @pl.when(c == pl.num_programs(1) - 1)
    def _():
        sout_ref[0] = s_ref[...]


def _gla_call(reg_row0, nseq, seqlen, chunk, t_rows, proj, lg, s0, g_norm, prev):
    nheads, dk, dv = s0.shape[1:]
    hk = nheads * dk
    d = nheads * dv
    nch = seqlen // chunk
    rb0 = reg_row0 // chunk
    wall, masks = _gla_consts(chunk)
    rowblk = lambda b, c: rb0 + b * nch + c
    in_specs = [pl.BlockSpec((chunk, hk), lambda b, c: (rowblk(b, c), 0)),
                pl.BlockSpec((chunk, hk), lambda b, c: (rowblk(b, c), 1)),
                pl.BlockSpec((chunk, d), lambda b, c: (rowblk(b, c), 1)),
                pl.BlockSpec((chunk, d), lambda b, c: (rowblk(b, c), 2)),
                pl.BlockSpec((chunk, hk), lambda b, c: (rowblk(b, c), 0)),
                pl.BlockSpec((1, nheads, dk, dv), lambda b, c: (b, 0, 0, 0)),
                pl.BlockSpec(wall.shape, lambda b, c: (0, 0)),
                pl.BlockSpec(masks.shape, lambda b, c: (0, 0, 0)),
                pl.BlockSpec((1, dv), lambda b, c: (0, 0))]
    args = [proj, proj, proj, proj, lg, s0, wall, masks, g_norm]
    aliases = {}
    if prev is not None:
        aliases[len(args)] = 0
        in_specs.append(pl.BlockSpec(memory_space=pl.ANY))
        args.append(prev)
    return pl.pallas_call(
        functools.partial(_gla_kernel, chunk=chunk, nheads=nheads, dk=dk, dv=dv),
        grid=(nseq, nch),
        in_specs=in_specs,
        out_specs=[pl.BlockSpec((chunk, d), lambda b, c: (rowblk(b, c), 0)),
                   pl.BlockSpec((1, nheads, dk, dv), lambda b, c: (b, 0, 0, 0))],
        out_shape=[jax.ShapeDtypeStruct((t_rows, d), BF16),
                   jax.ShapeDtypeStruct(s0.shape, F32)],
        scratch_shapes=[pltpu.VMEM((nheads, dk, dv), F32)],
        input_output_aliases=aliases,
        compiler_params=_cparams(2),
        name="gla_chunk",
    )(*args)


def _dispatch_plan(idx, rank, counts, n_tokens):
    n_experts = counts.shape[0]
    rb = EXP_ROWBLK
    a = n_tokens * TOP_K
    blocks_e = (counts + rb - 1) // rb
    padded = blocks_e * rb
    pad_start = jnp.cumsum(padded) - padded
    hit = idx[:, :, None] == jnp.arange(n_experts, dtype=I32)[None, None, :]
    dest = jnp.sum(jnp.where(hit, pad_start[None, None, :], 0), axis=-1) + rank
    np_rows = (a + rb - 1) // rb * rb + rb * n_experts
    items_e = (blocks_e + EXP_NBLK - 1) // EXP_NBLK
    item_end = jnp.cumsum(items_e)
    item_start = item_end - items_e
    n_items = n_experts + (np_rows // rb) // EXP_NBLK
    ii = jnp.arange(n_items, dtype=I32)
    total = item_end[-1]
    e_of = jnp.minimum(jnp.sum(item_end[None, :] <= ii[:, None], axis=1), n_experts - 1).astype(I32)
    valid = ii < total
    local = ii - item_start[e_of]
    e_last = e_of[jnp.maximum(total - 1, 0)]
    ie = jnp.where(valid, e_of, e_last).astype(I32)
    rsb = jnp.where(valid, pad_start[e_of] // rb + local * EXP_NBLK, 0).astype(I32)
    nrb = jnp.where(valid, jnp.minimum(EXP_NBLK, blocks_e[e_of] - local * EXP_NBLK), 0).astype(I32)
    return ie, rsb, nrb, pad_start.astype(I32), dest.astype(I32), np_rows, n_items


def _dispatch_kernel(cnt_ref, pst_ref, dest_hbm, h_ref, xs_hbm, idx_ref, hbuf, zrow, sem_i, sem_d, sem_z,
                     *, ntiles, n_experts):
    i = pl.program_id(0)
    slot = i % 2
    tile = h_ref.shape[0]
    nidx = tile * TOP_K // LANES
    tok_per_row = LANES // TOP_K
    rb = EXP_ROWBLK

    def idx_copy(t, s):
        return pltpu.make_async_copy(dest_hbm.at[pl.ds(t * nidx, nidx)], idx_ref.at[s], sem_i.at[s])

    def wait_rows(s):
        for _ in range(TOP_K):
            pltpu.make_async_copy(hbuf.at[s], xs_hbm.at[pl.ds(0, tile)], sem_d.at[s]).wait()

    @pl.when(i == 0)
    def _():
        idx_copy(0, 0).start()

    idx_copy(i, slot).wait()

    @pl.when(i + 1 < ntiles)
    def _():
        idx_copy(jnp.minimum(i + 1, ntiles - 1), 1 - slot).start()

    hbuf[slot] = h_ref[...]

    def tok(t, carry):
        for j in range(TOP_K):
            p = idx_ref[slot, t // tok_per_row, (t % tok_per_row) * TOP_K + j]
            pltpu.make_async_copy(hbuf.at[slot, pl.ds(t, 1)], xs_hbm.at[pl.ds(p, 1)], sem_d.at[slot]).start()
        return carry
    lax.fori_loop(0, tile, tok, 0)

    @pl.when(i > 0)
    def _():
        wait_rows(1 - slot)

    @pl.when(i == ntiles - 1)
    def _():
        wait_rows(slot)
        zrow[...] = jnp.zeros(zrow.shape, zrow.dtype)

        def expert(e, carry):
            cnt = cnt_ref[e]
            base = pst_ref[e]
            end = (cnt + rb - 1) // rb * rb

            def zstart(r, c2):
                pltpu.make_async_copy(zrow.at[pl.ds(0, 1)], xs_hbm.at[pl.ds(base + r, 1)], sem_z).start()
                return c2

            def zwait(r, c2):
                pltpu.make_async_copy(zrow.at[pl.ds(0, 1)], xs_hbm.at[pl.ds(0, 1)], sem_z).wait()
                return c2
            lax.fori_loop(cnt, end, zstart, 0)
            lax.fori_loop(cnt, end, zwait, 0)
            return carry
        lax.fori_loop(0, n_experts, expert, 0)


def _dispatch_call(counts, pad_start, dest2d, h_packed, np_rows, tile):
    t_rows, dh = h_packed.shape
    ntiles = t_rows // tile
    n_experts = counts.shape[0]
    grid_spec = pltpu.PrefetchScalarGridSpec(
        num_scalar_prefetch=2,
        grid=(ntiles,),
        in_specs=[pl.BlockSpec(memory_space=pl.ANY),
                  pl.BlockSpec((tile, dh), lambda i, c, p: (i, 0))],
        out_specs=pl.BlockSpec(memory_space=pl.ANY),
        scratch_shapes=[pltpu.SMEM((2, tile * TOP_K // LANES, LANES), I32),
                        pltpu.VMEM((2, tile, dh), jnp.uint32),
                        pltpu.VMEM((SUBLANES, dh), jnp.uint32),
                        pltpu.SemaphoreType.DMA((2,)),
                        pltpu.SemaphoreType.DMA((2,)),
                        pltpu.SemaphoreType.DMA])
    return pl.pallas_call(
        functools.partial(_dispatch_kernel, ntiles=ntiles, n_experts=n_experts),
        grid_spec=grid_spec,
        out_shape=jax.ShapeDtypeStruct((np_rows, dh), jnp.uint32),
        compiler_params=_cparams(1),
        name="moe_dispatch",
    )(counts, pad_start, dest2d, h_packed)


def _experts_kernel(ie_ref, rsb_ref, nrb_ref, xs_hbm, wi_ref, wo_ref, y_hbm,
                    xt, acc, act, ybuf, wibf, wobf, sem_x, sem_y, ycnt_ref, *, n_items, n_k, n_b):
    i = pl.program_id(0)
    st = pl.program_id(1)
    nrb = nrb_ref[i]
    rb = EXP_ROWBLK
    tkw = xt.shape[2]
    tdw = ybuf.shape[2]
    f = act.shape[1]

    def x_copy(item, kt, bi):
        src = pl.multiple_of((rsb_ref[item] + bi) * rb, rb)
        return pltpu.make_async_copy(
            xs_hbm.at[pl.ds(src, rb), pl.ds(pl.multiple_of(kt * tkw, tkw), tkw)],
            xt.at[kt % 2, pl.ds(pl.multiple_of(bi * rb, rb), rb)], sem_x.at[kt % 2])

    def start_x(item, kt):
        def blk(bi, carry):
            x_copy(item, kt, bi).start()
            return carry
        lax.fori_loop(0, nrb_ref[item], blk, 0)

    def wait_x(item, kt):
        def blk(bi, carry):
            x_copy(item, kt, bi).wait()
            return carry
        lax.fori_loop(0, nrb_ref[item], blk, 0)

    def wait_out(s):
        def blk(bi, carry):
            pltpu.make_async_copy(ybuf.at[s, pl.ds(0, rb)], y_hbm.at[pl.ds(0, rb), pl.ds(0, tdw)],
                                  sem_y.at[s]).wait()
            return carry
        lax.fori_loop(0, ycnt_ref[s], blk, 0)
        ycnt_ref[s] = 0

    @pl.when((i == 0) & (st == 0))
    def _():
        ycnt_ref[0] = 0
        ycnt_ref[1] = 0
        xt[...] = jnp.zeros(xt.shape, xt.dtype)
        start_x(0, 0)

    @pl.when(st < n_k)
    def _():
        wait_x(i, st)

    @pl.when(st + 1 < n_k)
    def _():
        start_x(i, st + 1)

    @pl.when((st == n_k) & (i + 1 < n_items))
    def _():
        start_x(jnp.minimum(i + 1, n_items - 1), 0)

    ch = EXP_CHUNK_BLKS * rb
    nch = (nrb + EXP_CHUNK_BLKS - 1) // EXP_CHUNK_BLKS

    def chunk_rows(ci):
        return pl.ds(pl.multiple_of(ci * ch, ch), ch)

    @pl.when((st < n_k) & (nrb > 0))
    def _():
        wibf[...] = wi_ref[0, 0].astype(BF16)
        slot = st % 2

        def partial_sum(ci):
            rows = chunk_rows(ci)
            return _dot(_unpack_pairs(xt[slot, rows, :], BF16), wibf[...])

        if n_k > 1:
            @pl.when(st == 0)
            def _():
                def first(ci, carry):
                    acc[chunk_rows(ci), :] = partial_sum(ci)
                    return carry
                lax.fori_loop(0, nch, first, 0)

            @pl.when((st > 0) & (st < n_k - 1))
            def _():
                def middle(ci, carry):
                    rows = chunk_rows(ci)
                    acc[rows, :] = acc[rows, :] + partial_sum(ci)
                    return carry
                lax.fori_loop(0, nch, middle, 0)

        @pl.when(st == n_k - 1)
        def _():
            def last(ci, carry):
                rows = chunk_rows(ci)
                a = partial_sum(ci)
                if n_k > 1:
                    a = a + acc[rows, :]
                act[rows, :] = (_silu(a[:, :f]) * a[:, f:]).astype(BF16)
                return carry
            lax.fori_loop(0, nch, last, 0)

    @pl.when((st >= n_k) & (nrb > 0))
    def _():
        dj = st - n_k
        ys = (i * n_b + dj) % 2
        wobf[...] = wo_ref[0, 0].astype(BF16)
        wait_out(ys)
        row0 = rsb_ref[i] * rb
        col0 = pl.multiple_of(dj * tdw, tdw)

        def chunk(ci, carry):
            rows = chunk_rows(ci)
            ybuf[ys, rows, :] = _pack_pairs(_dot(act[rows, :], wobf[...]))
            for b in range(EXP_CHUNK_BLKS):
                bi = ci * EXP_CHUNK_BLKS + b

                @pl.when(bi < nrb)
                def _():
                    r0 = pl.multiple_of(bi * rb, rb)
                    pltpu.make_async_copy(
                        ybuf.at[ys, pl.ds(r0, rb)],
                        y_hbm.at[pl.ds(pl.multiple_of(row0 + r0, rb), rb), pl.ds(col0, tdw)],
                        sem_y.at[ys]).start()
            return carry
        lax.fori_loop(0, nch, chunk, 0)
        ycnt_ref[ys] = nrb

    @pl.when((i == n_items - 1) & (st == n_k + n_b - 1))
    def _():
        wait_out(0)
        wait_out(1)


def _experts_call(layer, ie, rsb, nrb, n_items, xs, w_in, w_out):
    _, _, d, f2 = w_in.shape
    f = f2 // 2
    np_rows, dh = xs.shape
    tk = min(EXP_TK, d)
    td = min(EXP_TD, d)
    n_k, n_b = d // tk, d // td
    rmax = EXP_NBLK * EXP_ROWBLK

    def kt(st, nr):
        return jnp.where((nr > 0) & (st < n_k), st, n_k - 1)

    def dj(st, nr):
        return jnp.where(nr > 0, jnp.maximum(st - n_k, 0), n_b - 1)

    grid_spec = pltpu.PrefetchScalarGridSpec(
        num_scalar_prefetch=3,
        grid=(n_items, n_k + n_b),
        in_specs=[pl.BlockSpec(memory_space=pl.ANY),
                  pl.BlockSpec((1, 1, tk, f2), lambda i, st, ie, rsb, nrb: (layer, ie[i], kt(st, nrb[i]), 0)),
                  pl.BlockSpec((1, 1, f, td), lambda i, st, ie, rsb, nrb: (layer, ie[i], 0, dj(st, nrb[i])))],
        out_specs=pl.BlockSpec(memory_space=pl.ANY),
        scratch_shapes=[pltpu.VMEM((2, rmax, tk // 2), jnp.uint32),
                        pltpu.VMEM((rmax, f2), F32),
                        pltpu.VMEM((rmax, f), BF16),
                        pltpu.VMEM((2, rmax, td // 2), jnp.uint32),
                        pltpu.VMEM((tk, f2), BF16),
                        pltpu.VMEM((f, td), BF16),
                        pltpu.SemaphoreType.DMA((2,)),
                        pltpu.SemaphoreType.DMA((2,)),
                        pltpu.SMEM((2,), I32)])
    return pl.pallas_call(
        functools.partial(_experts_kernel, n_items=n_items, n_k=n_k, n_b=n_b),
        grid_spec=grid_spec,
        out_shape=jax.ShapeDtypeStruct((np_rows, dh), jnp.uint32),
        compiler_params=_cparams(2),
        name="moe_experts",
    )(ie, rsb, nrb, xs, w_in, w_out)


def _combine_kernel(pos_hbm, w_ref, sh_ref, x_ref, g_ref, fg_ref, y_hbm, *rest,
                    lrep, blk0, ntiles, final):
    o_ref, idx_ref, gbuf, sem_i, sem_g = rest[-5:]
    i = pl.program_id(0)
    slot = i % 2
    nslot = 1 - slot
    tile = x_ref.shape[0]
    nidx = tile * TOP_K // LANES
    tok_per_row = LANES // TOP_K

    def idx_copy(t, s):
        return pltpu.make_async_copy(pos_hbm.at[pl.ds((blk0 + t) * nidx, nidx)], idx_ref.at[s], sem_i.at[s])

    def issue_gather(s):
        def tok(t, carry):
            for j in range(TOP_K):
                p = idx_ref[s, t // tok_per_row, (t % tok_per_row) * TOP_K + j]
                pltpu.make_async_copy(y_hbm.at[pl.ds(p, 1)], gbuf.at[s, j, pl.ds(t, 1)], sem_g.at[s]).start()
            return carry
        lax.fori_loop(0, tile, tok, 0)

    @pl.when(i == 0)
    def _():
        first = idx_copy(0, 0)
        first.start()
        first.wait()
        issue_gather(0)
        if ntiles > 1:
            idx_copy(1, 1).start()

    @pl.when(i + 1 < ntiles)
    def _():
        idx_copy(jnp.minimum(i + 1, ntiles - 1), nslot).wait()

    for j in range(TOP_K):
        pltpu.make_async_copy(y_hbm.at[pl.ds(0, tile)], gbuf.at[slot, j], sem_g.at[slot]).wait()

    @pl.when(i + 1 < ntiles)
    def _():
        issue_gather(nslot)

    @pl.when(i + 2 < ntiles)
    def _():
        idx_copy(jnp.minimum(i + 2, ntiles - 1), slot).start()

    acc = sh_ref[...]
    for j in range(TOP_K):
        acc = acc + w_ref[:, j:j + 1] * _unpack_pairs(gbuf[slot, j], F32)
    o = x_ref[...] + _expand_rows(g_ref[...], lrep, tile, i * tile) * acc
    if final:
        ms = jnp.mean(o * o, axis=-1, keepdims=True)
        o = o * lax.rsqrt(ms + EPS) * fg_ref[...]
    o_ref[...] = o


def _combine_call(reg, t_rows, pos2d, wts, shared, x_all, g, final_g, y_sorted, final, prev):
    d = x_all.shape[1]
    row = lambda i: (reg.blk0 + i, 0)
    fixed = lambda i: (0, 0)
    in_specs = [pl.BlockSpec(memory_space=pl.ANY),
                pl.BlockSpec((reg.tile, TOP_K), row),
                pl.BlockSpec((reg.tile, d), row),
                pl.BlockSpec((reg.tile, d), row),
                pl.BlockSpec((reg.nseq, d), fixed),
                pl.BlockSpec((1, d), fixed),
                pl.BlockSpec(memory_space=pl.ANY)]
    args = [pos2d, wts, shared, x_all, g, final_g, y_sorted]
    aliases = {}
    if prev is not None:
        aliases[len(args)] = 0
        in_specs.append(pl.BlockSpec(memory_space=pl.ANY))
        args.append(prev)
    if final:
        out_spec = pl.BlockSpec((reg.tile, d), lambda i: (i, 0))
        out_rows = reg.rows
    else:
        out_spec = pl.BlockSpec((reg.tile, d), row)
        out_rows = t_rows
    return pl.pallas_call(
        functools.partial(_combine_kernel, lrep=reg.lrep, blk0=reg.blk0, ntiles=reg.ntiles, final=final),
        grid=(reg.ntiles,),
        in_specs=in_specs,
        out_specs=out_spec,
        out_shape=jax.ShapeDtypeStruct((out_rows, d), F32),
        scratch_shapes=[pltpu.SMEM((2, reg.tile * TOP_K // LANES, LANES), I32),
                        pltpu.VMEM((2, TOP_K, reg.tile, d // 2), jnp.uint32),
                        pltpu.SemaphoreType.DMA((2,)),
                        pltpu.SemaphoreType.DMA((2,))],
        input_output_aliases=aliases,
        compiler_params=_cparams(1),
        name="moe_combine",
    )(*args)


def _both(fn, regs):
    out = fn(regs[0], None)
    return fn(regs[1], out)


def kernel(x_prompt, x_sample, state_s5_re, state_s5_im, state_gla, c_prompt, c_sample, w_ada, b_ada, norm_g, s5_lam_re, s5_lam_im, s5_log_dt, s5_b_re, s5_b_im, s5_c_re, s5_c_im, s5_d, s5_w_glu, s5_b_glu, gla_w_in, gla_w_a1, gla_w_a2, gla_b_a, gla_g_norm, gla_w_o, moe_w_router, moe_b_router, moe_w_in, moe_w_out, moe_ws_in, moe_ws_out, final_g):
    bp, lp, d = x_prompt.shape
    bs, ls, _ = x_sample.shape
    assert bp == 1 and bs == SUBLANES and ls % SUBLANES == 0
    depth = w_ada.shape[0]
    n_exp = moe_w_in.shape[1]
    f_sh = moe_ws_out.shape[1]
    srows = bs * ls
    t_rows = lp + srows
    s5_steps = 64
    s5_blk = s5_steps * SUBLANES
    gla_chunk = 64
    assert lp % PROMPT_TILE == 0 and lp % s5_blk == 0 and lp % srows == 0

    mm_tile = MM_TILE if lp % MM_TILE == 0 else PROMPT_TILE
    regs_mm = (_Region(0, lp, mm_tile, 1, mm_tile), _Region(lp, srows, srows, bs, ls))
    regs_nm = (_Region(0, lp, PROMPT_TILE // 2, 1, PROMPT_TILE // 2), _Region(lp, srows, srows, bs, ls))
    regs_cb = (_Region(0, lp, COMB_TILE, 1, COMB_TILE), _Region(lp, srows, COMB_TILE, bs, ls))

    x_all = jnp.concatenate([x_prompt.reshape(lp, d), x_sample.reshape(srows, d)], axis=0)
    c_all = jnp.concatenate([c_prompt, c_sample, jnp.zeros((2 * SUBLANES - 1 - bs, d), F32)], axis=0)
    mod = _ada_call(c_all, w_ada, b_ada)

    def mods(layer, k):
        m = mod[layer, :, k * d:(k + 1) * d]
        return (m[0:1], m[1:1 + bs])

    new_re_p, new_im_p, new_gla_p, new_re_s, new_im_s, new_gla_s = [], [], [], [], [], []
    for i in range(depth):
        sh1, sc1, g1, sh2, sc2, g2 = [mods(i, k) for k in range(6)]
        j = i // 2
        ng1 = norm_g[i, 0].reshape(1, d)
        ng2 = norm_g[i, 1].reshape(1, d)
        if i % 2 == 0:
            (h_all,) = _both(lambda r, prev: _norm_call(
                "plain", r, t_rows, x_all, ng1, sc1[r.nseq > 1], sh1[r.nseq > 1], [], [(d, F32)], prev), regs_nm)
            grp, pst = s5_lam_re.shape[1:]
            ncol = grp * pst
            lam, bbr, bbi = _s5_disc_call(s5_lam_re[j], s5_lam_im[j], s5_log_dt[j], s5_b_re[j], s5_b_im[j], s5_steps)
            bmat, cmat = _s5_block_weights(bbr, bbi, s5_c_re[j], s5_c_im[j])
            kt = bmat.shape[0]
            lamt = jnp.transpose(lam.reshape(4, kt, ncol // kt), (1, 0, 2))
            lamt = jnp.concatenate([lamt, jnp.zeros_like(lamt)], axis=1)
            dsk = s5_d[j].reshape(1, d)
            hp = h_all[:lp].reshape(lp // s5_blk, SUBLANES, s5_steps, d).transpose(0, 2, 1, 3).reshape(lp, d)
            hs = h_all[lp:].reshape(bs, ls, d).transpose(1, 0, 2).reshape(srows, d)
            zeros_st = jnp.zeros((SUBLANES, ncol), F32)
            zp, pre, pim = _s5_call(hp, bmat, cmat, lamt, dsk, zeros_st, zeros_st, s5_steps, True)
            zs, sre, sim = _s5_call(hs, bmat, cmat, lamt, dsk, state_s5_re[j].reshape(bs, ncol),
                                    state_s5_im[j].reshape(bs, ncol), ls, False)
            z_all = jnp.concatenate(
                [zp.reshape(lp // s5_blk, s5_steps, SUBLANES, d).transpose(0, 2, 1, 3).reshape(lp, d),
                 zs.reshape(ls, bs, d).transpose(1, 0, 2).reshape(srows, d)], axis=0)
            new_re_p.append(pre[0].reshape(1, grp, pst))
            new_im_p.append(pim[0].reshape(1, grp, pst))
            new_re_s.append(sre.reshape(bs, grp, pst))
            new_im_s.append(sim.reshape(bs, grp, pst))
            bglu = s5_b_glu[j].reshape(1, d)
            x_all = _both(lambda r, prev: _mm_call(
                "glu", r, t_rows, z_all, [s5_w_glu[j]], [0], d, 512, F32,
                [(bglu, "col"), (z_all, "tile"), (x_all, "tile"), (g1[r.nseq > 1], "seq")], prev, "s5_glu"), regs_mm)
        else:
            hk = gla_w_a2.shape[2]
            assert 2 * hk == d
            h_bf, lg = _both(lambda r, prev: _norm_call(
                "gate", r, t_rows, x_all, ng1, sc1[r.nseq > 1], sh1[r.nseq > 1],
                [gla_w_a1[j], gla_w_a2[j], gla_b_a[j].reshape(1, hk)], [(d, BF16), (hk, F32)], prev), regs_nm)
            proj = _both(lambda r, prev: _mm_call(
                "plain", r, t_rows, h_bf, [gla_w_in[j]], [0], 3 * d, 512, F32, [], prev, "gla_proj"), regs_mm)
            gn = gla_g_norm[j].reshape(1, -1)
            s0p = jnp.zeros((1,) + state_gla.shape[2:], F32)
            o_all, gla_p = _gla_call(0, 1, lp, gla_chunk, t_rows, proj, lg, s0p, gn, None)
            o_all, gla_s = _gla_call(lp, bs, ls, ls, t_rows, proj, lg, state_gla[j], gn, o_all)
            new_gla_p.append(gla_p)
            new_gla_s.append(gla_s)
            x_all = _both(lambda r, prev: _mm_call(
                "res", r, t_rows, o_all, [gla_w_o[j]], [0], d, 512, F32,
                [(x_all, "tile"), (g1[r.nseq > 1], "seq")], prev, "gla_out"), regs_mm)

        brt = moe_b_router[i].reshape(1, n_exp)
        r_p = _router_call(regs_nm[0], t_rows, x_all, ng2, sc2[0], sh2[0], moe_w_router[i], brt,
                           jnp.zeros((SUBLANES, n_exp), F32), None)
        h_pk, h_bf, idx, wts, rank, cnt = _router_call(
            regs_nm[1], t_rows, x_all, ng2, sc2[1], sh2[1], moe_w_router[i], brt, r_p[5], r_p[:5])
        counts = cnt[0].astype(I32)
        ie, rsb, nrb, pad_start, dest, np_rows, n_items = _dispatch_plan(idx, rank, counts, t_rows)
        pos2d = dest.reshape(-1, LANES)
        xs = _dispatch_call(counts, pad_start, pos2d, h_pk, np_rows, DISP_TILE)
        y_sorted = _experts_call(i, ie, rsb, nrb, n_items, xs, moe_w_in, moe_w_out)
        act = _both(lambda r, prev: _mm_call(
            "swiglu", r, t_rows, h_bf, [moe_ws_in[i], moe_ws_in[i]], [0, f_sh // 256], f_sh, 256, BF16,
            [], prev, "shared_in"), regs_mm)
        shared = _both(lambda r, prev: _mm_call(
            "plain", r, t_rows, act, [moe_ws_out[i]], [0], d, 512, F32, [], prev, "shared_out"), regs_mm)
        last = i == depth - 1
        fg = final_g.reshape(1, d)
        if last:
            y_prompt, y_sample = [_combine_call(r, t_rows, pos2d, wts, shared, x_all, g2[r.nseq > 1], fg,
                                                y_sorted, True, None) for r in regs_cb]
        else:
            x_all = _both(lambda r, prev: _combine_call(
                r, t_rows, pos2d, wts, shared, x_all, g2[r.nseq > 1], fg, y_sorted, False, prev), regs_cb)

    y_prompt = y_prompt.reshape(bp, lp, d)
    y_sample = y_sample.reshape(bs, ls, d)
    return (y_prompt, y_sample, jnp.stack(new_re_p), jnp.stack(new_im_p), jnp.stack(new_gla_p),
            jnp.stack(new_re_s), jnp.stack(new_im_s), jnp.stack(new_gla_s))
```

```python
import functools

import numpy as np
import jax
import jax.numpy as jnp
from jax import lax
from jax.experimental import pallas as pl
from jax.experimental.pallas import tpu as pltpu

F32 = jnp.float32
BF16 = jnp.bfloat16
I32 = jnp.int32

EPS = 1e-6
GLA_TAU = 16.0
TOP_K = 8
N_GROUPS = 8
TOPK_GROUPS = 4
ROUTED_SCALE = 2.5

LANES = 128
SUBLANES = 8
MXU_DIM = 256
VMEM_LIMIT = 56 << 20

PROMPT_TILE = 512
MM_TILE = 1024
S5_KTILE = 256
S5_COLS = 512
EXP_ROWBLK = 128
EXP_NBLK = 18
EXP_CHUNK_BLKS = 2
EXP_TK = 1024
EXP_TD = 1024
PACK_GROUP = 512
DISP_TILE = 128
COMB_TILE = 128


def _cparams(n_axes, vmem=VMEM_LIMIT):
    return pltpu.CompilerParams(dimension_semantics=("arbitrary",) * n_axes, vmem_limit_bytes=vmem)


class _Region:
    def __init__(self, row0, rows, tile, nseq, lrep):
        self.row0, self.rows, self.tile, self.nseq, self.lrep = row0, rows, tile, nseq, lrep
        self.blk0 = row0 // tile
        self.ntiles = rows // tile
        assert row0 % tile == 0 and rows % tile == 0


def _expand_rows(m, lrep, rows=None, row0=0):
    nseq, n = m.shape
    if nseq == 1:
        return m
    rows = nseq * lrep if rows is None else rows
    r = lax.broadcasted_iota(I32, (rows, n), 0) + row0
    out = jnp.broadcast_to(m[0:1, :], (rows, n))
    for b in range(1, nseq):
        out = jnp.where(r >= b * lrep, jnp.broadcast_to(m[b:b + 1, :], (rows, n)), out)
    return out


def _sigmoid(x):
    return 1.0 / (1.0 + jnp.exp(-x))


def _silu(x):
    return x * _sigmoid(x)


def _gelu_tanh(x):
    return 0.5 * x * (1.0 + jnp.tanh(0.7978845608028654 * (x + 0.044715 * x * x * x)))


def _log_sigmoid(x):
    return jnp.minimum(x, 0.0) - jnp.log1p(jnp.exp(-jnp.abs(x)))


def _split3(x):
    p1 = x.astype(BF16)
    r1 = x - p1.astype(F32)
    p2 = r1.astype(BF16)
    r2 = r1 - p2.astype(F32)
    return p1, p2, r2.astype(BF16)


def _pack_pairs(x):
    half = PACK_GROUP // 2
    words = []
    for g in range(x.shape[1] // PACK_GROUP):
        lo = pltpu.bitcast(x[:, g * PACK_GROUP:g * PACK_GROUP + half].astype(BF16).astype(F32), jnp.uint32)
        hi = pltpu.bitcast(x[:, g * PACK_GROUP + half:(g + 1) * PACK_GROUP].astype(BF16).astype(F32), jnp.uint32)
        words.append(hi | (lo >> 16))
    return words[0] if len(words) == 1 else jnp.concatenate(words, axis=1)


def _unpack_pairs(w, dtype):
    half = PACK_GROUP // 2
    parts = []
    for g in range(w.shape[1] // half):
        ww = w[:, g * half:(g + 1) * half]
        parts.append(pltpu.bitcast(ww << 16, F32).astype(dtype))
        parts.append(pltpu.bitcast(ww & jnp.uint32(0xFFFF0000), F32).astype(dtype))
    return jnp.concatenate(parts, axis=1)


def _dot(a, b):
    return jnp.dot(a, b, preferred_element_type=F32)


def _dot_t0(a, b):
    return lax.dot_general(a, b, (((0,), (0,)), ((), ())), preferred_element_type=F32)


def _dot_t1(a, b):
    return lax.dot_general(a, b, (((1,), (1,)), ((), ())), preferred_element_type=F32)


def _ada_kernel(c_ref, w_ref, b_ref, o_ref):
    s = _silu(c_ref[...])
    o_ref[0] = _dot(s.astype(BF16), w_ref[0].astype(BF16)) + b_ref[0]


def _ada_call(c_all, w_ada, b_ada):
    depth, d, n6 = w_ada.shape
    nc = c_all.shape[0]
    tn = 512
    return pl.pallas_call(
        _ada_kernel,
        grid=(depth, n6 // tn),
        in_specs=[pl.BlockSpec((nc, d), lambda l, j: (0, 0)),
                  pl.BlockSpec((1, d, tn), lambda l, j: (l, 0, j)),
                  pl.BlockSpec((1, 1, tn), lambda l, j: (l, 0, j))],
        out_specs=pl.BlockSpec((1, nc, tn), lambda l, j: (l, 0, j)),
        out_shape=jax.ShapeDtypeStruct((depth, nc, n6), F32),
        compiler_params=_cparams(2),
        name="adaln",
    )(c_all, w_ada, b_ada.reshape(depth, 1, n6))


def _norm_mod(x, g, sc, sh, lrep):
    ms = jnp.mean(x * x, axis=-1, keepdims=True)
    y = x * lax.rsqrt(ms + EPS) * g
    return y * (1.0 + _expand_rows(sc, lrep)) + _expand_rows(sh, lrep)


def _norm_kernel(x_ref, g_ref, sc_ref, sh_ref, *rest, lrep, aliased):
    o_ref = rest[-1]
    h = _norm_mod(x_ref[...], g_ref[...], sc_ref[...], sh_ref[...], lrep)
    o_ref[...] = h.astype(o_ref.dtype)


def _cols_to_lanes(cols, dtype):
    r = cols[0].shape[0]
    lane = lax.broadcasted_iota(I32, (r, len(cols)), 1)
    out = jnp.zeros((r, len(cols)), dtype)
    for j, c in enumerate(cols):
        out = jnp.where(lane == j, c.astype(dtype), out)
    return out


def _router_kernel(x_ref, g_ref, sc_ref, sh_ref, w_ref, b_ref, cin_ref, *rest, lrep):
    hp_ref, hb_ref, idx_ref, wt_ref, rk_ref, cout_ref, run_ref = rest[-7:]
    i = pl.program_id(0)

    @pl.when(i == 0)
    def _():
        run_ref[...] = cin_ref[0:1, :]

    h = _norm_mod(x_ref[...], g_ref[...], sc_ref[...], sh_ref[...], lrep)
    hb_ref[...] = h.astype(BF16)
    hp_ref[...] = _pack_pairs(h)
    w = w_ref[...]
    hh = h.astype(BF16)
    hl = (h - hh.astype(F32)).astype(BF16)
    wh = w.astype(BF16)
    wl = (w - wh.astype(F32)).astype(BF16)
    scores = _sigmoid(_dot(hh, wh) + _dot(hl, wh) + _dot(hh, wl))

    rows, n_exp = scores.shape
    per = n_exp // N_GROUPS
    neg = -jnp.inf
    big = n_exp + 1
    lane_i = lax.broadcasted_iota(I32, (rows, n_exp), 1)
    lane = lane_i.astype(F32)
    grp_i = lane_i // per
    grp = grp_i.astype(F32)
    choice = scores + b_ref[...]
    gs = jnp.zeros((rows, n_exp), F32)
    for g in range(N_GROUPS):
        ing = grp_i == g
        m = jnp.where(ing, choice, neg)
        m1 = jnp.max(m, axis=-1, keepdims=True)
        top = m == m1
        ntop = jnp.sum(jnp.where(top, 1.0, 0.0), axis=-1, keepdims=True)
        below = jnp.max(jnp.where(top, neg, m), axis=-1, keepdims=True)
        m2 = jnp.where(ntop > 1.5, m1, below)
        gs = jnp.where(ing, m1 + m2, gs)
    masked = jnp.full((rows, n_exp), neg, F32)
    for _ in range(TOPK_GROUPS):
        mx = jnp.max(gs, axis=-1, keepdims=True)
        gi = jnp.min(jnp.where(gs == mx, grp, big), axis=-1, keepdims=True)
        sel = grp == gi
        masked = jnp.where(sel, choice, masked)
        gs = jnp.where(sel, neg, gs)
    idx_cols, w_cols, hits = [], [], []
    for _ in range(TOP_K):
        mx = jnp.max(masked, axis=-1, keepdims=True)
        ei = jnp.min(jnp.where(masked == mx, lane, big), axis=-1, keepdims=True)
        hit = lane == ei
        hits.append(hit)
        idx_cols.append(ei)
        w_cols.append(jnp.sum(jnp.where(hit, scores, 0.0), axis=-1, keepdims=True))
        masked = jnp.where(hit, neg, masked)
    wsum = w_cols[0]
    for c in w_cols[1:]:
        wsum = wsum + c
    scale = ROUTED_SCALE / wsum
    idx_ref[...] = _cols_to_lanes(idx_cols, I32)
    wt_ref[...] = _cols_to_lanes([c * scale for c in w_cols], F32)
    onehot = jnp.zeros((rows, n_exp), F32)
    for hit in hits:
        onehot = jnp.where(hit, 1.0, onehot)
    rr = lax.broadcasted_iota(I32, (rows, rows), 0)
    cc = lax.broadcasted_iota(I32, (rows, rows), 1)
    before = jnp.where(rr > cc, 1.0, 0.0).astype(BF16)
    cum = _dot(before, onehot.astype(BF16)) + run_ref[...]
    rk_ref[...] = _cols_to_lanes(
        [jnp.sum(jnp.where(hit, cum, 0.0), axis=-1, keepdims=True) for hit in hits], I32)
    run_ref[...] = run_ref[...] + jnp.sum(onehot, axis=0, keepdims=True)
    cout_ref[...] = jnp.broadcast_to(run_ref[...], cout_ref.shape)


def _router_call(reg, t_rows, x_all, g, sc, sh, w_router, b_router, cnt_in, prev):
    d = x_all.shape[1]
    n_exp = w_router.shape[1]
    row = lambda i: (reg.blk0 + i, 0)
    fixed = lambda i: (0, 0)
    in_specs = [pl.BlockSpec((reg.tile, d), row), pl.BlockSpec((1, d), fixed),
                pl.BlockSpec((reg.nseq, d), fixed), pl.BlockSpec((reg.nseq, d), fixed),
                pl.BlockSpec((d, n_exp), fixed), pl.BlockSpec((1, n_exp), fixed),
                pl.BlockSpec((SUBLANES, n_exp), fixed)]
    args = [x_all, g, sc, sh, w_router, b_router, cnt_in]
    outs = [(d // 2, jnp.uint32), (d, BF16), (TOP_K, I32), (TOP_K, F32), (TOP_K, I32)]
    aliases = {}
    if prev is not None:
        for k in range(len(outs)):
            aliases[len(args)] = k
            in_specs.append(pl.BlockSpec(memory_space=pl.ANY))
            args.append(prev[k])
    res = pl.pallas_call(
        functools.partial(_router_kernel, lrep=reg.lrep),
        grid=(reg.ntiles,),
        in_specs=in_specs,
        out_specs=[pl.BlockSpec((reg.tile, n), row) for n, _ in outs] + [pl.BlockSpec((SUBLANES, n_exp), fixed)],
        out_shape=[jax.ShapeDtypeStruct((t_rows, n), dt) for n, dt in outs]
        + [jax.ShapeDtypeStruct((SUBLANES, n_exp), F32)],
        scratch_shapes=[pltpu.VMEM((1, n_exp), F32)],
        input_output_aliases=aliases,
        compiler_params=_cparams(1),
        name="moe_router",
    )(*args)
    return list(res)


def _norm_gate_kernel(x_ref, g_ref, sc_ref, sh_ref, wa1_ref, wa2_ref, ba_ref, *rest, lrep, aliased):
    h_ref, lg_ref = rest[-2:]
    h = _norm_mod(x_ref[...], g_ref[...], sc_ref[...], sh_ref[...], lrep)
    hb = h.astype(BF16)
    h_ref[...] = hb
    a = _dot(hb, wa1_ref[...].astype(BF16))
    z = _dot(a.astype(BF16), wa2_ref[...].astype(BF16)) + ba_ref[...]
    lg_ref[...] = _log_sigmoid(z) * (1.0 / GLA_TAU)


def _norm_call(kind, reg, t_rows, x_all, g, sc, sh, extra, outs, prev):
    d = x_all.shape[1]
    kern = {"plain": _norm_kernel, "gate": _norm_gate_kernel}[kind]
    row = lambda i: (reg.blk0 + i, 0)
    fixed = lambda i: (0, 0)
    in_specs = [pl.BlockSpec((reg.tile, d), row), pl.BlockSpec((1, d), fixed),
                pl.BlockSpec((reg.nseq, d), fixed), pl.BlockSpec((reg.nseq, d), fixed)]
    args = [x_all, g, sc, sh]
    for e in extra:
        in_specs.append(pl.BlockSpec(e.shape, fixed))
        args.append(e)
    aliases = {}
    if prev is not None:
        for k, p in enumerate(prev):
            aliases[len(args)] = k
            in_specs.append(pl.BlockSpec(memory_space=pl.ANY))
            args.append(p)
    res = pl.pallas_call(
        functools.partial(kern, lrep=reg.lrep, aliased=prev is not None),
        grid=(reg.ntiles,),
        in_specs=in_specs,
        out_specs=[pl.BlockSpec((reg.tile, n), row) for n, _ in outs],
        out_shape=[jax.ShapeDtypeStruct((t_rows, n), dt) for n, dt in outs],
        input_output_aliases=aliases,
        compiler_params=_cparams(1),
        name="norm_" + kind,
    )(*args)
    return list(res)


def _s5_disc_kernel(lr_ref, li_ref, ldt_ref, br_ref, bi_ref, lam_ref, bbr_ref, bbi_ref, *, nsteps):
    lr = lr_ref[...]
    li = li_ref[...]
    dt = jnp.exp(ldt_ref[...])
    mag = jnp.exp(lr * dt)
    ar = mag * jnp.cos(li * dt)
    ai = mag * jnp.sin(li * dt)
    den = lr * lr + li * li
    fr = ((ar - 1.0) * lr + ai * li) / den
    fi = (ai * lr - (ar - 1.0) * li) / den
    lam_ref[0] = ar
    lam_ref[1] = ai
    mags = jnp.exp(nsteps * (lr * dt))
    lam_ref[2] = mags * jnp.cos(nsteps * (li * dt))
    lam_ref[3] = mags * jnp.sin(nsteps * (li * dt))
    for h in range(br_ref.shape[0]):
        br = br_ref[h]
        bi = bi_ref[h]
        bbr_ref[h] = fr * br - fi * bi
        bbi_ref[h] = fr * bi + fi * br


def _s5_disc_call(lam_re, lam_im, log_dt, b_re, b_im, nsteps):
    g, p = lam_re.shape
    hg = b_re.shape[2]
    ldt = jnp.broadcast_to(log_dt[:, None], (g, p))
    brt = jnp.transpose(b_re, (2, 0, 1))
    bit = jnp.transpose(b_im, (2, 0, 1))
    return pl.pallas_call(
        functools.partial(_s5_disc_kernel, nsteps=float(nsteps)),
        out_shape=[jax.ShapeDtypeStruct((4, g, p), F32),
                   jax.ShapeDtypeStruct((hg, g, p), F32),
                   jax.ShapeDtypeStruct((hg, g, p), F32)],
        name="s5_disc",
    )(lam_re, lam_im, ldt, brt, bit)


def _s5_block_weights(bbr, bbi, c_re, c_im):
    hg, g, p = bbr.shape
    gt = S5_KTILE // hg
    kt = g // gt
    eye = jnp.eye(gt, dtype=F32)

    def bd_in(b):
        b = b.reshape(hg, kt, gt, p)
        return jnp.einsum("hkgp,gq->kghqp", b, eye).reshape(kt, gt * hg, gt * p)

    def bd_out(c):
        c = c.reshape(kt, gt, hg, p)
        return jnp.einsum("kghp,gq->kqpgh", c, eye).reshape(kt, gt * p, gt * hg)

    bmat = jnp.concatenate([bd_in(bbr), bd_in(bbi)], axis=2).astype(BF16)
    cmat = jnp.concatenate([bd_out(c_re), -bd_out(c_im)], axis=1).astype(BF16)
    return bmat, cmat


def _s5_kernel(h_ref, b_ref, c_ref, lam_ref, d_ref, sre_ref, sim_ref, z_ref, ore_ref, oim_ref,
               bu_ref, st_ref, *, nsteps, chain):
    rb = pl.program_id(1)
    nc = sre_ref.shape[1]

    @pl.when(rb == 0)
    def _():
        st_ref[0] = sre_ref[...]
        st_ref[1] = sim_ref[...]

    u = h_ref[...]
    bu_ref[...] = _dot(u.astype(BF16), b_ref[0])
    rowid = lax.broadcasted_iota(I32, (SUBLANES, S5_COLS), 0)

    for cb in range(nc // S5_COLS):
        c_re = slice(cb * S5_COLS, (cb + 1) * S5_COLS)
        c_im = slice(nc + cb * S5_COLS, nc + (cb + 1) * S5_COLS)
        ar = jnp.broadcast_to(lam_ref[0, 0:1, c_re], (SUBLANES, S5_COLS))
        ai = jnp.broadcast_to(lam_ref[0, 1:2, c_re], (SUBLANES, S5_COLS))

        def step(s, carry, store):
            xr, xi = carry
            r0 = pl.multiple_of(s * SUBLANES, SUBLANES)
            br = bu_ref[pl.ds(r0, SUBLANES), c_re]
            bi = bu_ref[pl.ds(r0, SUBLANES), c_im]
            nxr = ar * xr - ai * xi + br
            nxi = ar * xi + ai * xr + bi
            if store:
                bu_ref[pl.ds(r0, SUBLANES), c_re] = nxr
                bu_ref[pl.ds(r0, SUBLANES), c_im] = nxi
            return nxr, nxi

        if chain:
            zero = jnp.zeros((SUBLANES, S5_COLS), F32)
            er, ei = lax.fori_loop(0, nsteps, functools.partial(step, store=False), (zero, zero))
            asr = lam_ref[0, 2:3, c_re]
            asi = lam_ref[0, 3:4, c_re]
            pr = st_ref[0, 0:1, c_re]
            pi = st_ref[1, 0:1, c_re]
            x0r, x0i = zero, zero
            for j in range(SUBLANES):
                x0r = jnp.where(rowid == j, jnp.broadcast_to(pr, (SUBLANES, S5_COLS)), x0r)
                x0i = jnp.where(rowid == j, jnp.broadcast_to(pi, (SUBLANES, S5_COLS)), x0i)
                nr = asr * pr - asi * pi + er[j:j + 1]
                ni = asr * pi + asi * pr + ei[j:j + 1]
                pr, pi = nr, ni
            st_ref[0, :, c_re] = jnp.broadcast_to(pr, (SUBLANES, S5_COLS))
            st_ref[1, :, c_re] = jnp.broadcast_to(pi, (SUBLANES, S5_COLS))
            lax.fori_loop(0, nsteps, functools.partial(step, store=True), (x0r, x0i))
        else:
            fr, fi = lax.fori_loop(0, nsteps, functools.partial(step, store=True),
                                   (st_ref[0, :, c_re], st_ref[1, :, c_re]))
            st_ref[0, :, c_re] = fr
            st_ref[1, :, c_re] = fi

    y = _dot(bu_ref[...].astype(BF16), c_ref[0]) + d_ref[...] * u
    z_ref[...] = _gelu_tanh(y).astype(z_ref.dtype)

    @pl.when(rb == pl.num_programs(1) - 1)
    def _():
        ore_ref[...] = st_ref[0]
        oim_ref[...] = st_ref[1]


def _s5_call(hp, bmat, cmat, lam, d_skip, s_re, s_im, nsteps, chain):
    rows, d = hp.shape
    kt = bmat.shape[0]
    nc = bmat.shape[2] // 2
    rblk = nsteps * SUBLANES
    return pl.pallas_call(
        functools.partial(_s5_kernel, nsteps=nsteps, chain=chain),
        grid=(kt, rows // rblk),
        in_specs=[pl.BlockSpec((rblk, S5_KTILE), lambda k, r: (r, k)),
                  pl.BlockSpec((1, S5_KTILE, 2 * nc), lambda k, r: (k, 0, 0)),
                  pl.BlockSpec((1, 2 * nc, S5_KTILE), lambda k, r: (k, 0, 0)),
                  pl.BlockSpec((1, SUBLANES, nc), lambda k, r: (k, 0, 0)),
                  pl.BlockSpec((1, S5_KTILE), lambda k, r: (0, k)),
                  pl.BlockSpec((SUBLANES, nc), lambda k, r: (0, k)),
                  pl.BlockSpec((SUBLANES, nc), lambda k, r: (0, k))],
        out_specs=[pl.BlockSpec((rblk, S5_KTILE), lambda k, r: (r, k)),
                   pl.BlockSpec((SUBLANES, nc), lambda k, r: (0, k)),
                   pl.BlockSpec((SUBLANES, nc), lambda k, r: (0, k))],
        out_shape=[jax.ShapeDtypeStruct((rows, d), BF16),
                   jax.ShapeDtypeStruct(s_re.shape, F32),
                   jax.ShapeDtypeStruct(s_im.shape, F32)],
        scratch_shapes=[pltpu.VMEM((rblk, 2 * nc), F32), pltpu.VMEM((2, SUBLANES, nc), F32)],
        compiler_params=_cparams(2),
        name="s5_scan",
    )(hp, bmat, cmat, lam, d_skip, s_re, s_im)


def _mm_kernel(*refs, mode, lrep, n_w, n_alias):
    lhs_ref = refs[0]
    w_refs = refs[1:1 + n_w]
    pos = 1 + n_w
    i = pl.program_id(1)
    wbf = refs[len(refs) - n_w:]
    o_ref = refs[len(refs) - n_w - 1]

    @pl.when(i == 0)
    def _():
        for w_ref, s_ref in zip(w_refs, wbf):
            s_ref[...] = w_ref[...].astype(BF16)

    lhs = lhs_ref[...]
    if mode == "glu":
        b_ref, zt_ref, x_ref, g_ref = refs[pos:pos + 4]
        t = _dot(lhs, wbf[0][...]) + b_ref[...]
        o = zt_ref[...].astype(F32) * _sigmoid(t)
        o_ref[...] = x_ref[...] + _expand_rows(g_ref[...], lrep) * o
    elif mode == "res":
        x_ref, g_ref = refs[pos:pos + 2]
        o_ref[...] = x_ref[...] + _expand_rows(g_ref[...], lrep) * _dot(lhs, wbf[0][...])
    elif mode == "plain":
        o_ref[...] = _dot(lhs, wbf[0][...]).astype(o_ref.dtype)
    elif mode == "swiglu":
        o_ref[...] = (_silu(_dot(lhs, wbf[0][...])) * _dot(lhs, wbf[1][...])).astype(o_ref.dtype)


def _mm_call(mode, reg, t_rows, lhs, w_list, w_colblk0, n_out, tn, out_dtype, extras, prev, name):
    k_dim = lhs.shape[1]
    nj = n_out // tn
    in_specs = [pl.BlockSpec((reg.tile, k_dim), lambda j, i: (reg.blk0 + i, 0))]
    args = [lhs]
    for w, c0 in zip(w_list, w_colblk0):
        in_specs.append(pl.BlockSpec((k_dim, tn), lambda j, i, c0=c0: (0, c0 + j)))
        args.append(w)
    for a, kind in extras:
        if kind == "col":
            in_specs.append(pl.BlockSpec((1, tn), lambda j, i: (0, j)))
        elif kind == "tile":
            in_specs.append(pl.BlockSpec((reg.tile, tn), lambda j, i: (reg.blk0 + i, j)))
        else:
            in_specs.append(pl.BlockSpec((reg.nseq, tn), lambda j, i: (0, j)))
        args.append(a)
    aliases = {}
    if prev is not None:
        aliases[len(args)] = 0
        in_specs.append(pl.BlockSpec(memory_space=pl.ANY))
        args.append(prev)
    return pl.pallas_call(
        functools.partial(_mm_kernel, mode=mode, lrep=reg.lrep, n_w=len(w_list), n_alias=len(aliases)),
        grid=(nj, reg.ntiles),
        in_specs=in_specs,
        out_specs=pl.BlockSpec((reg.tile, tn), lambda j, i: (reg.blk0 + i, j)),
        out_shape=jax.ShapeDtypeStruct((t_rows, n_out), out_dtype),
        scratch_shapes=[pltpu.VMEM((k_dim, tn), BF16) for _ in w_list],
        input_output_aliases=aliases,
        compiler_params=_cparams(2),
        name=name,
    )(*args)


def _gla_consts(chunk):
    nlev = int(np.log2(chunk))
    assert 1 << nlev == chunk
    tri = np.tril(np.ones((chunk, chunk), np.float32))
    r = np.arange(chunk)
    wall, masks = [tri], []
    for l in range(nlev):
        w = chunk >> (l + 1)
        blk = r // (2 * w)
        second = (r & w) != 0
        wall.append(tri[blk * 2 * w + w - 1])
        masks.append(((blk[:, None] == blk[None, :]) & second[:, None] & (~second)[None, :]).astype(np.float32))
    masks.append(np.eye(chunk, dtype=np.float32))
    return jnp.asarray(np.concatenate(wall, 0), BF16), jnp.asarray(np.stack(masks, 0), F32)


def _gla_kernel(q_ref, k_ref, v_ref, gate_ref, lg_ref, s0_ref, wall_ref, mask_ref, gn_ref,
                *rest, chunk, nheads, dk, dv):
    o_ref, sout_ref, s_ref = rest[-3:]
    c = pl.program_id(1)
    nlev = mask_ref.shape[0] - 1

    @pl.when(c == 0)
    def _():
        s_ref[...] = s0_ref[0]

    lg = lg_ref[...]
    p1, p2, p3 = _split3(lg)
    wall = wall_ref[...]
    bg = _dot(wall, p1) + _dot(wall, p2) + _dot(wall, p3)
    b = bg[0:chunk]
    q = q_ref[...] * (dk ** -0.5)
    k = k_ref[...]
    row = lax.broadcasted_iota(I32, q.shape, 0)
    qs, ks = [], []
    for l in range(nlev):
        w = chunk >> (l + 1)
        g = bg[(l + 1) * chunk:(l + 2) * chunk]
        second = (row & w) != 0
        e = jnp.exp(jnp.where(second, b - g, g - b))
        qk = jnp.where(second, q, k) * e
        qs.append(jnp.where(second, qk, 0.0).astype(BF16))
        ks.append(jnp.where(second, 0.0, qk).astype(BF16))
    qb = q.astype(BF16)
    kb = k.astype(BF16)
    q_in = (q * jnp.exp(b)).astype(BF16)
    k_dec = (k * jnp.exp(b[chunk - 1:chunk] - b)).astype(BF16)
    ones = jnp.ones((chunk, LANES), BF16)
    gn = gn_ref[...]
    for h in range(nheads):
        ck = slice(h * dk, (h + 1) * dk)
        cv = slice(h * dv, (h + 1) * dv)
        att = _dot_t1(qb[:, ck], kb[:, ck]) * mask_ref[nlev]
        for l in range(nlev):
            att = att + _dot_t1(qs[l][:, ck], ks[l][:, ck]) * mask_ref[l]
        vh = v_ref[:, cv].astype(BF16)
        s_h = s_ref[h]
        o = _dot(att.astype(BF16), vh) + _dot(q_in[:, ck], s_h.astype(BF16))
        dcol = jnp.exp(_dot_t0(p1[:, ck], ones) + _dot_t0(p2[:, ck], ones) + _dot_t0(p3[:, ck], ones))
        s_ref[h] = jnp.concatenate([dcol] * (dv // LANES), axis=1) * s_h + _dot_t0(k_dec[:, ck], vh)
        ms = jnp.mean(o * o, axis=-1, keepdims=True)
        on = o * lax.rsqrt(ms + EPS) * gn
        o_ref[:, cv] = (on * _silu(gate_ref[:, cv])).astype(o_ref.dtype)

    @pl.when(c == pl.num_programs(1) - 1)
    def _():
        sout_ref[0] = s_ref[...]


def _gla_call(reg_row0, nseq, seqlen, chunk, t_rows, proj, lg, s0, g_norm, prev):
    nheads, dk, dv = s0.shape[1:]
    hk = nheads * dk
    d = nheads * dv
    nch = seqlen // chunk
    rb0 = reg_row0 // chunk
    wall, masks = _gla_consts(chunk)
    rowblk = lambda b, c: rb0 + b * nch + c
    in_specs = [pl.BlockSpec((chunk, hk), lambda b, c: (rowblk(b, c), 0)),
                pl.BlockSpec((chunk, hk), lambda b, c: (rowblk(b, c), 1)),
                pl.BlockSpec((chunk, d), lambda b, c: (rowblk(b, c), 1)),
                pl.BlockSpec((chunk, d), lambda b, c: (rowblk(b, c), 2)),
                pl.BlockSpec((chunk, hk), lambda b, c: (rowblk(b, c), 0)),
                pl.BlockSpec((1, nheads, dk, dv), lambda b, c: (b, 0, 0, 0)),
                pl.BlockSpec(wall.shape, lambda b, c: (0, 0)),
                pl.BlockSpec(masks.shape, lambda b, c: (0, 0, 0)),
                pl.BlockSpec((1, dv), lambda b, c: (0, 0))]
    args = [proj, proj, proj, proj, lg, s0, wall, masks, g_norm]
    aliases = {}
    if prev is not None:
        aliases[len(args)] = 0
        in_specs.append(pl.BlockSpec(memory_space=pl.ANY))
        args.append(prev)
    return pl.pallas_call(
        functools.partial(_gla_kernel, chunk=chunk, nheads=nheads, dk=dk, dv=dv),
        grid=(nseq, nch),
        in_specs=in_specs,
        out_specs=[pl.BlockSpec((chunk, d), lambda b, c: (rowblk(b, c), 0)),
                   pl.BlockSpec((1, nheads, dk, dv), lambda b, c: (b, 0, 0, 0))],
        out_shape=[jax.ShapeDtypeStruct((t_rows, d), BF16),
                   jax.ShapeDtypeStruct(s0.shape, F32)],
        scratch_shapes=[pltpu.VMEM((nheads, dk, dv), F32)],
        input_output_aliases=aliases,
        compiler_params=_cparams(2),
        name="gla_chunk",
    )(*args)


def _dispatch_plan(idx, rank, counts, n_tokens):
    n_experts = counts.shape[0]
    rb = EXP_ROWBLK
    a = n_tokens * TOP_K
    blocks_e = (counts + rb - 1) // rb
    padded = blocks_e * rb
    pad_start = jnp.cumsum(padded) - padded
    hit = idx[:, :, None] == jnp.arange(n_experts, dtype=I32)[None, None, :]
    dest = jnp.sum(jnp.where(hit, pad_start[None, None, :], 0), axis=-1) + rank
    np_rows = (a + rb - 1) // rb * rb + rb * n_experts
    items_e = (blocks_e + EXP_NBLK - 1) // EXP_NBLK
    item_end = jnp.cumsum(items_e)
    item_start = item_end - items_e
    n_items = n_experts + (np_rows // rb) // EXP_NBLK
    ii = jnp.arange(n_items, dtype=I32)
    total = item_end[-1]
    e_of = jnp.minimum(jnp.sum(item_end[None, :] <= ii[:, None], axis=1), n_experts - 1).astype(I32)
    valid = ii < total
    local = ii - item_start[e_of]
    e_last = e_of[jnp.maximum(total - 1, 0)]
    ie = jnp.where(valid, e_of, e_last).astype(I32)
    rsb = jnp.where(valid, pad_start[e_of] // rb + local * EXP_NBLK, 0).astype(I32)
    nrb = jnp.where(valid, jnp.minimum(EXP_NBLK, blocks_e[e_of] - local * EXP_NBLK), 0).astype(I32)
    return ie, rsb, nrb, pad_start.astype(I32), dest.astype(I32), np_rows, n_items


def _dispatch_kernel(cnt_ref, pst_ref, dest_hbm, h_ref, xs_hbm, idx_ref, hbuf, zrow, sem_i, sem_d, sem_z,
                     *, ntiles, n_experts):
    i = pl.program_id(0)
    slot = i % 2
    tile = h_ref.shape[0]
    nidx = tile * TOP_K
    rb = EXP_ROWBLK

    def idx_copy(t, s):
        return pltpu.make_async_copy(dest_hbm.at[pl.ds(pl.multiple_of(t * nidx, nidx), nidx)],
                                     idx_ref.at[pl.ds(pl.multiple_of(s * nidx, nidx), nidx)], sem_i.at[s])

    def wait_rows(s):
        for _ in range(TOP_K):
            pltpu.make_async_copy(hbuf.at[s], xs_hbm.at[pl.ds(0, tile)], sem_d.at[s]).wait()

    @pl.when(i == 0)
    def _():
        idx_copy(0, 0).start()

    idx_copy(i, slot).wait()

    @pl.when(i + 1 < ntiles)
    def _():
        idx_copy(jnp.minimum(i + 1, ntiles - 1), 1 - slot).start()

    hbuf[slot] = h_ref[...]

    def tok(t, carry):
        base = slot * nidx + t * TOP_K
        for j in range(TOP_K):
            p = idx_ref[base + j]
            pltpu.make_async_copy(hbuf.at[slot, pl.ds(t, 1)], xs_hbm.at[pl.ds(p, 1)], sem_d.at[slot]).start()
        return carry
    lax.fori_loop(0, tile, tok, 0)

    @pl.when(i > 0)
    def _():
        wait_rows(1 - slot)

    @pl.when(i == ntiles - 1)
    def _():
        wait_rows(slot)
        zrow[...] = jnp.zeros(zrow.shape, zrow.dtype)

        def expert(e, carry):
            cnt = cnt_ref[e]
            base = pst_ref[e]
            end = (cnt + rb - 1) // rb * rb

            def zstart(r, c2):
                pltpu.make_async_copy(zrow.at[pl.ds(0, 1)], xs_hbm.at[pl.ds(base + r, 1)], sem_z).start()
                return c2

            def zwait(r, c2):
                pltpu.make_async_copy(zrow.at[pl.ds(0, 1)], xs_hbm.at[pl.ds(0, 1)], sem_z).wait()
                return c2
            lax.fori_loop(cnt, end, zstart, 0)
            lax.fori_loop(cnt, end, zwait, 0)
            return carry
        lax.fori_loop(0, n_experts, expert, 0)


def _dispatch_call(counts, pad_start, dest_flat, h_packed, np_rows, tile):
    t_rows, dh = h_packed.shape
    ntiles = t_rows // tile
    n_experts = counts.shape[0]
    grid_spec = pltpu.PrefetchScalarGridSpec(
        num_scalar_prefetch=2,
        grid=(ntiles,),
        in_specs=[pl.BlockSpec(memory_space=pl.ANY),
                  pl.BlockSpec((tile, dh), lambda i, c, p: (i, 0))],
        out_specs=pl.BlockSpec(memory_space=pl.ANY),
        scratch_shapes=[pltpu.SMEM((2 * tile * TOP_K,), I32),
                        pltpu.VMEM((2, tile, dh), jnp.uint32),
                        pltpu.VMEM((SUBLANES, dh), jnp.uint32),
                        pltpu.SemaphoreType.DMA((2,)),
                        pltpu.SemaphoreType.DMA((2,)),
                        pltpu.SemaphoreType.DMA])
    return pl.pallas_call(
        functools.partial(_dispatch_kernel, ntiles=ntiles, n_experts=n_experts),
        grid_spec=grid_spec,
        out_shape=jax.ShapeDtypeStruct((np_rows, dh), jnp.uint32),
        compiler_params=_cparams(1),
        name="moe_dispatch",
    )(counts, pad_start, dest_flat, h_packed)


def _experts_kernel(ie_ref, rsb_ref, nrb_ref, xs_hbm, wi_ref, wo_ref, y_hbm,
                    xt, acc, act, ybuf, wibf, wobf, sem_x, sem_y, ycnt_ref, *, n_items, n_k, n_b):
    i = pl.program_id(0)
    st = pl.program_id(1)
    nrb = nrb_ref[i]
    rb = EXP_ROWBLK
    tkw = xt.shape[2]
    tdw = ybuf.shape[2]
    f = act.shape[1]

    def x_copy(item, kt, bi):
        src = pl.multiple_of((rsb_ref[item] + bi) * rb, rb)
        return pltpu.make_async_copy(
            xs_hbm.at[pl.ds(src, rb), pl.ds(pl.multiple_of(kt * tkw, tkw), tkw)],
            xt.at[kt % 2, pl.ds(pl.multiple_of(bi * rb, rb), rb)], sem_x.at[kt % 2])

    def start_x(item, kt):
        def blk(bi, carry):
            x_copy(item, kt, bi).start()
            return carry
        lax.fori_loop(0, nrb_ref[item], blk, 0)

    def wait_x(item, kt):
        def blk(bi, carry):
            x_copy(item, kt, bi).wait()
            return carry
        lax.fori_loop(0, nrb_ref[item], blk, 0)

    def wait_out(s):
        def blk(bi, carry):
            pltpu.make_async_copy(ybuf.at[s, pl.ds(0, rb)], y_hbm.at[pl.ds(0, rb), pl.ds(0, tdw)],
                                  sem_y.at[s]).wait()
            return carry
        lax.fori_loop(0, ycnt_ref[s], blk, 0)
        ycnt_ref[s] = 0

    @pl.when((i == 0) & (st == 0))
    def _():
        ycnt_ref[0] = 0
        ycnt_ref[1] = 0
        xt[...] = jnp.zeros(xt.shape, xt.dtype)
        start_x(0, 0)

    @pl.when(st < n_k)
    def _():
        wait_x(i, st)

    @pl.when(st + 1 < n_k)
    def _():
        start_x(i, st + 1)

    @pl.when((st == n_k) & (i + 1 < n_items))
    def _():
        start_x(jnp.minimum(i + 1, n_items - 1), 0)

    ch = EXP_CHUNK_BLKS * rb
    nch = (nrb + EXP_CHUNK_BLKS - 1) // EXP_CHUNK_BLKS

    def chunk_rows(ci):
        return pl.ds(pl.multiple_of(ci * ch, ch), ch)

    @pl.when((st < n_k) & (nrb > 0))
    def _():
        wibf[...] = wi_ref[0, 0].astype(BF16)
        slot = st % 2

        def partial_sum(ci):
            rows = chunk_rows(ci)
            return _dot(_unpack_pairs(xt[slot, rows, :], BF16), wibf[...])

        if n_k > 1:
            @pl.when(st == 0)
            def _():
                def first(ci, carry):
                    acc[chunk_rows(ci), :] = partial_sum(ci)
                    return carry
                lax.fori_loop(0, nch, first, 0)

            @pl.when((st > 0) & (st < n_k - 1))
            def _():
                def middle(ci, carry):
                    rows = chunk_rows(ci)
                    acc[rows, :] = acc[rows, :] + partial_sum(ci)
                    return carry
                lax.fori_loop(0, nch, middle, 0)

        @pl.when(st == n_k - 1)
        def _():
            def last(ci, carry):
                rows = chunk_rows(ci)
                a = partial_sum(ci)
                if n_k > 1:
                    a = a + acc[rows, :]
                act[rows, :] = (_silu(a[:, :f]) * a[:, f:]).astype(BF16)
                return carry
            lax.fori_loop(0, nch, last, 0)

    @pl.when((st >= n_k) & (nrb > 0))
    def _():
        dj = st - n_k
        ys = (i * n_b + dj) % 2
        wobf[...] = wo_ref[0, 0].astype(BF16)
        wait_out(ys)
        row0 = rsb_ref[i] * rb
        col0 = pl.multiple_of(dj * tdw, tdw)

        def chunk(ci, carry):
            rows = chunk_rows(ci)
            ybuf[ys, rows, :] = _pack_pairs(_dot(act[rows, :], wobf[...]))
            for b in range(EXP_CHUNK_BLKS):
                bi = ci * EXP_CHUNK_BLKS + b

                @pl.when(bi < nrb)
                def _():
                    r0 = pl.multiple_of(bi * rb, rb)
                    pltpu.make_async_copy(
                        ybuf.at[ys, pl.ds(r0, rb)],
                        y_hbm.at[pl.ds(pl.multiple_of(row0 + r0, rb), rb), pl.ds(col0, tdw)],
                        sem_y.at[ys]).start()
            return carry
        lax.fori_loop(0, nch, chunk, 0)
        ycnt_ref[ys] = nrb

    @pl.when((i == n_items - 1) & (st == n_k + n_b - 1))
    def _():
        wait_out(0)
        wait_out(1)


def _experts_call(layer, ie, rsb, nrb, n_items, xs, w_in, w_out):
    _, _, d, f2 = w_in.shape
    f = f2 // 2
    np_rows, dh = xs.shape
    tk = min(EXP_TK, d)
    td = min(EXP_TD, d)
    n_k, n_b = d // tk, d // td
    rmax = EXP_NBLK * EXP_ROWBLK

    def kt(st, nr):
        return jnp.where((nr > 0) & (st < n_k), st, n_k - 1)

    def dj(st, nr):
        return jnp.where(nr > 0, jnp.maximum(st - n_k, 0), n_b - 1)

    grid_spec = pltpu.PrefetchScalarGridSpec(
        num_scalar_prefetch=3,
        grid=(n_items, n_k + n_b),
        in_specs=[pl.BlockSpec(memory_space=pl.ANY),
                  pl.BlockSpec((1, 1, tk, f2), lambda i, st, ie, rsb, nrb: (layer, ie[i], kt(st, nrb[i]), 0)),
                  pl.BlockSpec((1, 1, f, td), lambda i, st, ie, rsb, nrb: (layer, ie[i], 0, dj(st, nrb[i])))],
        out_specs=pl.BlockSpec(memory_space=pl.ANY),
        scratch_shapes=[pltpu.VMEM((2, rmax, tk // 2), jnp.uint32),
                        pltpu.VMEM((rmax, f2), F32),
                        pltpu.VMEM((rmax, f), BF16),
                        pltpu.VMEM((2, rmax, td // 2), jnp.uint32),
                        pltpu.VMEM((tk, f2), BF16),
                        pltpu.VMEM((f, td), BF16),
                        pltpu.SemaphoreType.DMA((2,)),
                        pltpu.SemaphoreType.DMA((2,)),
                        pltpu.SMEM((2,), I32)])
    return pl.pallas_call(
        functools.partial(_experts_kernel, n_items=n_items, n_k=n_k, n_b=n_b),
        grid_spec=grid_spec,
        out_shape=jax.ShapeDtypeStruct((np_rows, dh), jnp.uint32),
        compiler_params=_cparams(2),
        name="moe_experts",
    )(ie, rsb, nrb, xs, w_in, w_out)


def _combine_kernel(pos_hbm, w_ref, sh_ref, x_ref, g_ref, fg_ref, y_hbm, *rest,
                    lrep, blk0, ntiles, final):
    o_ref, idx_ref, gbuf, sem_i, sem_g = rest[-5:]
    i = pl.program_id(0)
    slot = i % 2
    nslot = 1 - slot
    tile = x_ref.shape[0]
    nidx = tile * TOP_K

    def idx_copy(t, s):
        return pltpu.make_async_copy(pos_hbm.at[pl.ds(pl.multiple_of((blk0 + t) * nidx, nidx), nidx)],
                                     idx_ref.at[pl.ds(pl.multiple_of(s * nidx, nidx), nidx)], sem_i.at[s])

    def issue_gather(s):
        def tok(t, carry):
            base = s * nidx + t * TOP_K
            for j in range(TOP_K):
                p = idx_ref[base + j]
                pltpu.make_async_copy(y_hbm.at[pl.ds(p, 1)], gbuf.at[s, j, pl.ds(t, 1)], sem_g.at[s]).start()
            return carry
        lax.fori_loop(0, tile, tok, 0)

    @pl.when(i == 0)
    def _():
        first = idx_copy(0, 0)
        first.start()
        first.wait()
        issue_gather(0)
        if ntiles > 1:
            idx_copy(1, 1).start()

    @pl.when(i + 1 < ntiles)
    def _():
        idx_copy(jnp.minimum(i + 1, ntiles - 1), nslot).wait()

    for j in range(TOP_K):
        pltpu.make_async_copy(y_hbm.at[pl.ds(0, tile)], gbuf.at[slot, j], sem_g.at[slot]).wait()

    @pl.when(i + 1 < ntiles)
    def _():
        issue_gather(nslot)

    @pl.when(i + 2 < ntiles)
    def _():
        idx_copy(jnp.minimum(i + 2, ntiles - 1), slot).start()

    acc = sh_ref[...]
    for j in range(TOP_K):
        acc = acc + w_ref[:, j:j + 1] * _unpack_pairs(gbuf[slot, j], F32)
    o = x_ref[...] + _expand_rows(g_ref[...], lrep, tile, i * tile) * acc
    if final:
        ms = jnp.mean(o * o, axis=-1, keepdims=True)
        o = o * lax.rsqrt(ms + EPS) * fg_ref[...]
    o_ref[...] = o


def _combine_call(reg, t_rows, pos_flat, wts, shared, x_all, g, final_g, y_sorted, final, prev):
    d = x_all.shape[1]
    row = lambda i: (reg.blk0 + i, 0)
    fixed = lambda i: (0, 0)
    in_specs = [pl.BlockSpec(memory_space=pl.ANY),
                pl.BlockSpec((reg.tile, TOP_K), row),
                pl.BlockSpec((reg.tile, d), row),
                pl.BlockSpec((reg.tile, d), row),
                pl.BlockSpec((reg.nseq, d), fixed),
                pl.BlockSpec((1, d), fixed),
                pl.BlockSpec(memory_space=pl.ANY)]
    args = [pos_flat, wts, shared, x_all, g, final_g, y_sorted]
    aliases = {}
    if prev is not None:
        aliases[len(args)] = 0
        in_specs.append(pl.BlockSpec(memory_space=pl.ANY))
        args.append(prev)
    if final:
        out_spec = pl.BlockSpec((reg.tile, d), lambda i: (i, 0))
        out_rows = reg.rows
    else:
        out_spec = pl.BlockSpec((reg.tile, d), row)
        out_rows = t_rows
    return pl.pallas_call(
        functools.partial(_combine_kernel, lrep=reg.lrep, blk0=reg.blk0, ntiles=reg.ntiles, final=final),
        grid=(reg.ntiles,),
        in_specs=in_specs,
        out_specs=out_spec,
        out_shape=jax.ShapeDtypeStruct((out_rows, d), F32),
        scratch_shapes=[pltpu.SMEM((2 * reg.tile * TOP_K,), I32),
                        pltpu.VMEM((2, TOP_K, reg.tile, d // 2), jnp.uint32),
                        pltpu.SemaphoreType.DMA((2,)),
                        pltpu.SemaphoreType.DMA((2,))],
        input_output_aliases=aliases,
        compiler_params=_cparams(1),
        name="moe_combine",
    )(*args)


def _both(fn, regs):
    out = fn(regs[0], None)
    return fn(regs[1], out)


def kernel(x_prompt, x_sample, state_s5_re, state_s5_im, state_gla, c_prompt, c_sample, w_ada, b_ada, norm_g, s5_lam_re, s5_lam_im, s5_log_dt, s5_b_re, s5_b_im, s5_c_re, s5_c_im, s5_d, s5_w_glu, s5_b_glu, gla_w_in, gla_w_a1, gla_w_a2, gla_b_a, gla_g_norm, gla_w_o, moe_w_router, moe_b_router, moe_w_in, moe_w_out, moe_ws_in, moe_ws_out, final_g):
    bp, lp, d = x_prompt.shape
    bs, ls, _ = x_sample.shape
    assert bp == 1 and bs == SUBLANES and ls % SUBLANES == 0
    depth = w_ada.shape[0]
    n_exp = moe_w_in.shape[1]
    f_sh = moe_ws_out.shape[1]
    srows = bs * ls
    t_rows = lp + srows
    s5_steps = 64
    s5_blk = s5_steps * SUBLANES
    gla_chunk = 64
    assert lp % PROMPT_TILE == 0 and lp % s5_blk == 0 and lp % srows == 0

    mm_tile = MM_TILE if lp % MM_TILE == 0 else PROMPT_TILE
    regs_mm = (_Region(0, lp, mm_tile, 1, mm_tile), _Region(lp, srows, srows, bs, ls))
    regs_nm = (_Region(0, lp, PROMPT_TILE // 2, 1, PROMPT_TILE // 2), _Region(lp, srows, srows, bs, ls))
    regs_cb = (_Region(0, lp, COMB_TILE, 1, COMB_TILE), _Region(lp, srows, COMB_TILE, bs, ls))

    x_all = jnp.concatenate([x_prompt.reshape(lp, d), x_sample.reshape(srows, d)], axis=0)
    c_all = jnp.concatenate([c_prompt, c_sample, jnp.zeros((2 * SUBLANES - 1 - bs, d), F32)], axis=0)
    mod = _ada_call(c_all, w_ada, b_ada)

    def mods(layer, k):
        m = mod[layer, :, k * d:(k + 1) * d]
        return (m[0:1], m[1:1 + bs])

    new_re_p, new_im_p, new_gla_p, new_re_s, new_im_s, new_gla_s = [], [], [], [], [], []
    for i in range(depth):
        sh1, sc1, g1, sh2, sc2, g2 = [mods(i, k) for k in range(6)]
        j = i // 2
        ng1 = norm_g[i, 0].reshape(1, d)
        ng2 = norm_g[i, 1].reshape(1, d)
        if i % 2 == 0:
            (h_all,) = _both(lambda r, prev: _norm_call(
                "plain", r, t_rows, x_all, ng1, sc1[r.nseq > 1], sh1[r.nseq > 1], [], [(d, F32)], prev), regs_nm)
            grp, pst = s5_lam_re.shape[1:]
            ncol = grp * pst
            lam, bbr, bbi = _s5_disc_call(s5_lam_re[j], s5_lam_im[j], s5_log_dt[j], s5_b_re[j], s5_b_im[j], s5_steps)
            bmat, cmat = _s5_block_weights(bbr, bbi, s5_c_re[j], s5_c_im[j])
            kt = bmat.shape[0]
            lamt = jnp.transpose(lam.reshape(4, kt, ncol // kt), (1, 0, 2))
            lamt = jnp.concatenate([lamt, jnp.zeros_like(lamt)], axis=1)
            dsk = s5_d[j].reshape(1, d)
            hp = h_all[:lp].reshape(lp // s5_blk, SUBLANES, s5_steps, d).transpose(0, 2, 1, 3).reshape(lp, d)
            hs = h_all[lp:].reshape(bs, ls, d).transpose(1, 0, 2).reshape(srows, d)
            zeros_st = jnp.zeros((SUBLANES, ncol), F32)
            zp, pre, pim = _s5_call(hp, bmat, cmat, lamt, dsk, zeros_st, zeros_st, s5_steps, True)
            zs, sre, sim = _s5_call(hs, bmat, cmat, lamt, dsk, state_s5_re[j].reshape(bs, ncol),
                                    state_s5_im[j].reshape(bs, ncol), ls, False)
            z_all = jnp.concatenate(
                [zp.reshape(lp // s5_blk, s5_steps, SUBLANES, d).transpose(0, 2, 1, 3).reshape(lp, d),
                 zs.reshape(ls, bs, d).transpose(1, 0, 2).reshape(srows, d)], axis=0)
            new_re_p.append(pre[0].reshape(1, grp, pst))
            new_im_p.append(pim[0].reshape(1, grp, pst))
            new_re_s.append(sre.reshape(bs, grp, pst))
            new_im_s.append(sim.reshape(bs, grp, pst))
            bglu = s5_b_glu[j].reshape(1, d)
            x_all = _both(lambda r, prev: _mm_call(
                "glu", r, t_rows, z_all, [s5_w_glu[j]], [0], d, 512, F32,
                [(bglu, "col"), (z_all, "tile"), (x_all, "tile"), (g1[r.nseq > 1], "seq")], prev, "s5_glu"), regs_mm)
        else:
            hk = gla_w_a2.shape[2]
            assert 2 * hk == d
            h_bf, lg = _both(lambda r, prev: _norm_call(
                "gate", r, t_rows, x_all, ng1, sc1[r.nseq > 1], sh1[r.nseq > 1],
                [gla_w_a1[j], gla_w_a2[j], gla_b_a[j].reshape(1, hk)], [(d, BF16), (hk, F32)], prev), regs_nm)
            proj = _both(lambda r, prev: _mm_call(
                "plain", r, t_rows, h_bf, [gla_w_in[j]], [0], 3 * d, 512, F32, [], prev, "gla_proj"), regs_mm)
            gn = gla_g_norm[j].reshape(1, -1)
            s0p = jnp.zeros((1,) + state_gla.shape[2:], F32)
            o_all, gla_p = _gla_call(0, 1, lp, gla_chunk, t_rows, proj, lg, s0p, gn, None)
            o_all, gla_s = _gla_call(lp, bs, ls, ls, t_rows, proj, lg, state_gla[j], gn, o_all)
            new_gla_p.append(gla_p)
            new_gla_s.append(gla_s)
            x_all = _both(lambda r, prev: _mm_call(
                "res", r, t_rows, o_all, [gla_w_o[j]], [0], d, 512, F32,
                [(x_all, "tile"), (g1[r.nseq > 1], "seq")], prev, "gla_out"), regs_mm)

        brt = moe_b_router[i].reshape(1, n_exp)
        r_p = _router_call(regs_nm[0], t_rows, x_all, ng2, sc2[0], sh2[0], moe_w_router[i], brt,
                           jnp.zeros((SUBLANES, n_exp), F32), None)
        h_pk, h_bf, idx, wts, rank, cnt = _router_call(
            regs_nm[1], t_rows, x_all, ng2, sc2[1], sh2[1], moe_w_router[i], brt, r_p[5], r_p[:5])
        counts = cnt[0].astype(I32)
        ie, rsb, nrb, pad_start, dest, np_rows, n_items = _dispatch_plan(idx, rank, counts, t_rows)
        pos_flat = dest.reshape(-1)
        xs = _dispatch_call(counts, pad_start, pos_flat, h_pk, np_rows, DISP_TILE)
        y_sorted = _experts_call(i, ie, rsb, nrb, n_items, xs, moe_w_in, moe_w_out)
        act = _both(lambda r, prev: _mm_call(
            "swiglu", r, t_rows, h_bf, [moe_ws_in[i], moe_ws_in[i]], [0, f_sh // 256], f_sh, 256, BF16,
            [], prev, "shared_in"), regs_mm)
        shared = _both(lambda r, prev: _mm_call(
            "plain", r, t_rows, act, [moe_ws_out[i]], [0], d, 512, F32, [], prev, "shared_out"), regs_mm)
        last = i == depth - 1
        fg = final_g.reshape(1, d)
        if last:
            y_prompt, y_sample = [_combine_call(r, t_rows, pos_flat, wts, shared, x_all, g2[r.nseq > 1], fg,
                                                y_sorted, True, None) for r in regs_cb]
        else:
            x_all = _both(lambda r, prev: _combine_call(
                r, t_rows, pos_flat, wts, shared, x_all, g2[r.nseq > 1], fg, y_sorted, False, prev), regs_cb)

    y_prompt = y_prompt.reshape(bp, lp, d)
    y_sample = y_sample.reshape(bs, ls, d)
    return (y_prompt, y_sample, jnp.stack(new_re_p), jnp.stack(new_im_p), jnp.stack(new_gla_p),
            jnp.stack(new_re_s), jnp.stack(new_im_s), jnp.stack(new_gla_s))
```

```python
import functools

import numpy as np
import jax
import jax.numpy as jnp
from jax import lax
from jax.experimental import pallas as pl
from jax.experimental.pallas import tpu as pltpu

F32 = jnp.float32
BF16 = jnp.bfloat16
I32 = jnp.int32

EPS = 1e-6
GLA_TAU = 16.0
TOP_K = 8
N_GROUPS = 8
TOPK_GROUPS = 4
ROUTED_SCALE = 2.5

LANES = 128
SUBLANES = 8
MXU_DIM = 256
VMEM_LIMIT = 56 << 20

PROMPT_TILE = 512
MM_TILE = 1024
S5_KTILE = 256
S5_COLS = 512
EXP_ROWBLK = 128
EXP_NBLK = 18
EXP_CHUNK_BLKS = 4
EXP_TK = 1024
EXP_TD = 1024
PACK_GROUP = 512
DISP_TILE = 128
COMB_TILE = 128


def _cparams(n_axes, vmem=VMEM_LIMIT):
    return pltpu.CompilerParams(dimension_semantics=("arbitrary",) * n_axes, vmem_limit_bytes=vmem)


class _Region:
    def __init__(self, row0, rows, tile, nseq, lrep):
        self.row0, self.rows, self.tile, self.nseq, self.lrep = row0, rows, tile, nseq, lrep
        self.blk0 = row0 // tile
        self.ntiles = rows // tile
        assert row0 % tile == 0 and rows % tile == 0


def _expand_rows(m, lrep, rows=None, row0=0):
    nseq, n = m.shape
    if nseq == 1:
        return m
    rows = nseq * lrep if rows is None else rows
    r = lax.broadcasted_iota(I32, (rows, n), 0) + row0
    out = jnp.broadcast_to(m[0:1, :], (rows, n))
    for b in range(1, nseq):
        out = jnp.where(r >= b * lrep, jnp.broadcast_to(m[b:b + 1, :], (rows, n)), out)
    return out


def _sigmoid(x):
    return 1.0 / (1.0 + jnp.exp(-x))


def _silu(x):
    return x * _sigmoid(x)


def _gelu_tanh(x):
    return 0.5 * x * (1.0 + jnp.tanh(0.7978845608028654 * (x + 0.044715 * x * x * x)))


def _log_sigmoid(x):
    return jnp.minimum(x, 0.0) - jnp.log1p(jnp.exp(-jnp.abs(x)))


def _split3(x):
    p1 = x.astype(BF16)
    r1 = x - p1.astype(F32)
    p2 = r1.astype(BF16)
    r2 = r1 - p2.astype(F32)
    return p1, p2, r2.astype(BF16)


def _pack_pairs(x):
    half = PACK_GROUP // 2
    words = []
    for g in range(x.shape[1] // PACK_GROUP):
        lo = pltpu.bitcast(x[:, g * PACK_GROUP:g * PACK_GROUP + half].astype(BF16).astype(F32), jnp.uint32)
        hi = pltpu.bitcast(x[:, g * PACK_GROUP + half:(g + 1) * PACK_GROUP].astype(BF16).astype(F32), jnp.uint32)
        words.append(hi | (lo >> 16))
    return words[0] if len(words) == 1 else jnp.concatenate(words, axis=1)


def _unpack_pairs(w, dtype):
    half = PACK_GROUP // 2
    parts = []
    for g in range(w.shape[1] // half):
        ww = w[:, g * half:(g + 1) * half]
        parts.append(pltpu.bitcast(ww << 16, F32).astype(dtype))
        parts.append(pltpu.bitcast(ww & jnp.uint32(0xFFFF0000), F32).astype(dtype))
    return jnp.concatenate(parts, axis=1)


def _dot(a, b):
    return jnp.dot(a, b, preferred_element_type=F32)


def _dot_t0(a, b):
    return lax.dot_general(a, b, (((0,), (0,)), ((), ())), preferred_element_type=F32)


def _dot_t1(a, b):
    return lax.dot_general(a, b, (((1,), (1,)), ((), ())), preferred_element_type=F32)


def _ada_kernel(c_ref, w_ref, b_ref, o_ref):
    s = _silu(c_ref[...])
    o_ref[0] = _dot(s.astype(BF16), w_ref[0].astype(BF16)) + b_ref[0]


def _ada_call(c_all, w_ada, b_ada):
    depth, d, n6 = w_ada.shape
    nc = c_all.shape[0]
    tn = 512
    return pl.pallas_call(
        _ada_kernel,
        grid=(depth, n6 // tn),
        in_specs=[pl.BlockSpec((nc, d), lambda l, j: (0, 0)),
                  pl.BlockSpec((1, d, tn), lambda l, j: (l, 0, j)),
                  pl.BlockSpec((1, 1, tn), lambda l, j: (l, 0, j))],
        out_specs=pl.BlockSpec((1, nc, tn), lambda l, j: (l, 0, j)),
        out_shape=jax.ShapeDtypeStruct((depth, nc, n6), F32),
        compiler_params=_cparams(2),
        name="adaln",
    )(c_all, w_ada, b_ada.reshape(depth, 1, n6))


def _norm_mod(x, g, sc, sh, lrep):
    ms = jnp.mean(x * x, axis=-1, keepdims=True)
    y = x * lax.rsqrt(ms + EPS) * g
    return y * (1.0 + _expand_rows(sc, lrep)) + _expand_rows(sh, lrep)


def _norm_kernel(x_ref, g_ref, sc_ref, sh_ref, *rest, lrep, aliased):
    o_ref = rest[-1]
    h = _norm_mod(x_ref[...], g_ref[...], sc_ref[...], sh_ref[...], lrep)
    o_ref[...] = h.astype(o_ref.dtype)


def _cols_to_lanes(cols, dtype):
    r = cols[0].shape[0]
    lane = lax.broadcasted_iota(I32, (r, len(cols)), 1)
    out = jnp.zeros((r, len(cols)), dtype)
    for j, c in enumerate(cols):
        out = jnp.where(lane == j, c.astype(dtype), out)
    return out


def _router_kernel(x_ref, g_ref, sc_ref, sh_ref, w_ref, b_ref, cin_ref, *rest, lrep):
    hp_ref, hb_ref, idx_ref, wt_ref, rk_ref, cout_ref, run_ref = rest[-7:]
    i = pl.program_id(0)

    @pl.when(i == 0)
    def _():
        run_ref[...] = cin_ref[0:1, :]

    h = _norm_mod(x_ref[...], g_ref[...], sc_ref[...], sh_ref[...], lrep)
    hb_ref[...] = h.astype(BF16)
    hp_ref[...] = _pack_pairs(h)
    w = w_ref[...]
    hh = h.astype(BF16)
    hl = (h - hh.astype(F32)).astype(BF16)
    wh = w.astype(BF16)
    wl = (w - wh.astype(F32)).astype(BF16)
    scores = _sigmoid(_dot(hh, wh) + _dot(hl, wh) + _dot(hh, wl))

    rows, n_exp = scores.shape
    per = n_exp // N_GROUPS
    neg = -jnp.inf
    big = n_exp + 1
    lane_i = lax.broadcasted_iota(I32, (rows, n_exp), 1)
    lane = lane_i.astype(F32)
    grp_i = lane_i // per
    grp = grp_i.astype(F32)
    choice = scores + b_ref[...]
    gs = jnp.zeros((rows, n_exp), F32)
    for g in range(N_GROUPS):
        ing = grp_i == g
        m = jnp.where(ing, choice, neg)
        m1 = jnp.max(m, axis=-1, keepdims=True)
        top = m == m1
        ntop = jnp.sum(jnp.where(top, 1.0, 0.0), axis=-1, keepdims=True)
        below = jnp.max(jnp.where(top, neg, m), axis=-1, keepdims=True)
        m2 = jnp.where(ntop > 1.5, m1, below)
        gs = jnp.where(ing, m1 + m2, gs)
    masked = jnp.full((rows, n_exp), neg, F32)
    for _ in range(TOPK_GROUPS):
        mx = jnp.max(gs, axis=-1, keepdims=True)
        gi = jnp.min(jnp.where(gs == mx, grp, big), axis=-1, keepdims=True)
        sel = grp == gi
        masked = jnp.where(sel, choice, masked)
        gs = jnp.where(sel, neg, gs)
    idx_cols, w_cols, hits = [], [], []
    for _ in range(TOP_K):
        mx = jnp.max(masked, axis=-1, keepdims=True)
        ei = jnp.min(jnp.where(masked == mx, lane, big), axis=-1, keepdims=True)
        hit = lane == ei
        hits.append(hit)
        idx_cols.append(ei)
        w_cols.append(jnp.sum(jnp.where(hit, scores, 0.0), axis=-1, keepdims=True))
        masked = jnp.where(hit, neg, masked)
    wsum = w_cols[0]
    for c in w_cols[1:]:
        wsum = wsum + c
    scale = ROUTED_SCALE / wsum
    idx_ref[...] = _cols_to_lanes(idx_cols, I32)
    wt_ref[...] = _cols_to_lanes([c * scale for c in w_cols], F32)
    onehot = jnp.zeros((rows, n_exp), F32)
    for hit in hits:
        onehot = jnp.where(hit, 1.0, onehot)
    rr = lax.broadcasted_iota(I32, (rows, rows), 0)
    cc = lax.broadcasted_iota(I32, (rows, rows), 1)
    before = jnp.where(rr > cc, 1.0, 0.0).astype(BF16)
    cum = _dot(before, onehot.astype(BF16)) + run_ref[...]
    rk_ref[...] = _cols_to_lanes(
        [jnp.sum(jnp.where(hit, cum, 0.0), axis=-1, keepdims=True) for hit in hits], I32)
    run_ref[...] = run_ref[...] + jnp.sum(onehot, axis=0, keepdims=True)
    cout_ref[...] = jnp.broadcast_to(run_ref[...], cout_ref.shape)


def _router_call(reg, t_rows, x_all, g, sc, sh, w_router, b_router, cnt_in, prev):
    d = x_all.shape[1]
    n_exp = w_router.shape[1]
    row = lambda i: (reg.blk0 + i, 0)
    fixed = lambda i: (0, 0)
    in_specs = [pl.BlockSpec((reg.tile, d), row), pl.BlockSpec((1, d), fixed),
                pl.BlockSpec((reg.nseq, d), fixed), pl.BlockSpec((reg.nseq, d), fixed),
                pl.BlockSpec((d, n_exp), fixed), pl.BlockSpec((1, n_exp), fixed),
                pl.BlockSpec((SUBLANES, n_exp), fixed)]
    args = [x_all, g, sc, sh, w_router, b_router, cnt_in]
    outs = [(d // 2, jnp.uint32), (d, BF16), (TOP_K, I32), (TOP_K, F32), (TOP_K, I32)]
    aliases = {}
    if prev is not None:
        for k in range(len(outs)):
            aliases[len(args)] = k
            in_specs.append(pl.BlockSpec(memory_space=pl.ANY))
            args.append(prev[k])
    res = pl.pallas_call(
        functools.partial(_router_kernel, lrep=reg.lrep),
        grid=(reg.ntiles,),
        in_specs=in_specs,
        out_specs=[pl.BlockSpec((reg.tile, n), row) for n, _ in outs] + [pl.BlockSpec((SUBLANES, n_exp), fixed)],
        out_shape=[jax.ShapeDtypeStruct((t_rows, n), dt) for n, dt in outs]
        + [jax.ShapeDtypeStruct((SUBLANES, n_exp), F32)],
        scratch_shapes=[pltpu.VMEM((1, n_exp), F32)],
        input_output_aliases=aliases,
        compiler_params=_cparams(1),
        name="moe_router",
    )(*args)
    return list(res)


def _norm_gate_kernel(x_ref, g_ref, sc_ref, sh_ref, wa1_ref, wa2_ref, ba_ref, *rest, lrep, aliased):
    h_ref, lg_ref = rest[-2:]
    h = _norm_mod(x_ref[...], g_ref[...], sc_ref[...], sh_ref[...], lrep)
    hb = h.astype(BF16)
    h_ref[...] = hb
    a = _dot(hb, wa1_ref[...].astype(BF16))
    z = _dot(a.astype(BF16), wa2_ref[...].astype(BF16)) + ba_ref[...]
    lg_ref[...] = _log_sigmoid(z) * (1.0 / GLA_TAU)


def _norm_call(kind, reg, t_rows, x_all, g, sc, sh, extra, outs, prev):
    d = x_all.shape[1]
    kern = {"plain": _norm_kernel, "gate": _norm_gate_kernel}[kind]
    row = lambda i: (reg.blk0 + i, 0)
    fixed = lambda i: (0, 0)
    in_specs = [pl.BlockSpec((reg.tile, d), row), pl.BlockSpec((1, d), fixed),
                pl.BlockSpec((reg.nseq, d), fixed), pl.BlockSpec((reg.nseq, d), fixed)]
    args = [x_all, g, sc, sh]
    for e in extra:
        in_specs.append(pl.BlockSpec(e.shape, fixed))
        args.append(e)
    aliases = {}
    if prev is not None:
        for k, p in enumerate(prev):
            aliases[len(args)] = k
            in_specs.append(pl.BlockSpec(memory_space=pl.ANY))
            args.append(p)
    res = pl.pallas_call(
        functools.partial(kern, lrep=reg.lrep, aliased=prev is not None),
        grid=(reg.ntiles,),
        in_specs=in_specs,
        out_specs=[pl.BlockSpec((reg.tile, n), row) for n, _ in outs],
        out_shape=[jax.ShapeDtypeStruct((t_rows, n), dt) for n, dt in outs],
        input_output_aliases=aliases,
        compiler_params=_cparams(1),
        name="norm_" + kind,
    )(*args)
    return list(res)


def _s5_disc_kernel(lr_ref, li_ref, ldt_ref, br_ref, bi_ref, lam_ref, bbr_ref, bbi_ref, *, nsteps):
    lr = lr_ref[...]
    li = li_ref[...]
    dt = jnp.exp(ldt_ref[...])
    mag = jnp.exp(lr * dt)
    ar = mag * jnp.cos(li * dt)
    ai = mag * jnp.sin(li * dt)
    den = lr * lr + li * li
    fr = ((ar - 1.0) * lr + ai * li) / den
    fi = (ai * lr - (ar - 1.0) * li) / den
    lam_ref[0] = ar
    lam_ref[1] = ai
    mags = jnp.exp(nsteps * (lr * dt))
    lam_ref[2] = mags * jnp.cos(nsteps * (li * dt))
    lam_ref[3] = mags * jnp.sin(nsteps * (li * dt))
    for h in range(br_ref.shape[0]):
        br = br_ref[h]
        bi = bi_ref[h]
        bbr_ref[h] = fr * br - fi * bi
        bbi_ref[h] = fr * bi + fi * br


def _s5_disc_call(lam_re, lam_im, log_dt, b_re, b_im, nsteps):
    g, p = lam_re.shape
    hg = b_re.shape[2]
    ldt = jnp.broadcast_to(log_dt[:, None], (g, p))
    brt = jnp.transpose(b_re, (2, 0, 1))
    bit = jnp.transpose(b_im, (2, 0, 1))
    return pl.pallas_call(
        functools.partial(_s5_disc_kernel, nsteps=float(nsteps)),
        out_shape=[jax.ShapeDtypeStruct((4, g, p), F32),
                   jax.ShapeDtypeStruct((hg, g, p), F32),
                   jax.ShapeDtypeStruct((hg, g, p), F32)],
        name="s5_disc",
    )(lam_re, lam_im, ldt, brt, bit)


def _s5_block_weights(bbr, bbi, c_re, c_im):
    hg, g, p = bbr.shape
    gt = S5_KTILE // hg
    kt = g // gt
    eye = jnp.eye(gt, dtype=F32)

    def bd_in(b):
        b = b.reshape(hg, kt, gt, p)
        return jnp.einsum("hkgp,gq->kghqp", b, eye).reshape(kt, gt * hg, gt * p)

    def bd_out(c):
        c = c.reshape(kt, gt, hg, p)
        return jnp.einsum("kghp,gq->kqpgh", c, eye).reshape(kt, gt * p, gt * hg)

    bmat = jnp.concatenate([bd_in(bbr), bd_in(bbi)], axis=2).astype(BF16)
    cmat = jnp.concatenate([bd_out(c_re), -bd_out(c_im)], axis=1).astype(BF16)
    return bmat, cmat


def _s5_perm(nsteps):
    r = np.arange(nsteps * SUBLANES)
    p = np.zeros((r.size, r.size), np.float32)
    p[(r % nsteps) * SUBLANES + r // nsteps, r] = 1.0
    return jnp.asarray(p, BF16), jnp.asarray(p.T, BF16)


def _s5_kernel(h_ref, p_ref, pt_ref, b_ref, c_ref, lam_ref, d_ref, sre_ref, sim_ref, *rest, nsteps, chain):
    z_ref, ore_ref, oim_ref, bu_ref, st_ref = rest[-5:]
    rb = pl.program_id(1)
    nc = sre_ref.shape[1]

    @pl.when(rb == 0)
    def _():
        st_ref[0] = sre_ref[...]
        st_ref[1] = sim_ref[...]

    perm = p_ref[...]
    h1, h2, h3 = _split3(h_ref[...])
    u = _dot(perm, h1) + _dot(perm, h2) + _dot(perm, h3)
    bu_ref[...] = _dot(u.astype(BF16), b_ref[0])
    rowid = lax.broadcasted_iota(I32, (SUBLANES, S5_COLS), 0)

    for cb in range(nc // S5_COLS):
        c_re = slice(cb * S5_COLS, (cb + 1) * S5_COLS)
        c_im = slice(nc + cb * S5_COLS, nc + (cb + 1) * S5_COLS)
        ar = jnp.broadcast_to(lam_ref[0, 0:1, c_re], (SUBLANES, S5_COLS))
        ai = jnp.broadcast_to(lam_ref[0, 1:2, c_re], (SUBLANES, S5_COLS))

        def step(s, carry, store):
            xr, xi = carry
            r0 = pl.multiple_of(s * SUBLANES, SUBLANES)
            br = bu_ref[pl.ds(r0, SUBLANES), c_re]
            bi = bu_ref[pl.ds(r0, SUBLANES), c_im]
            nxr = ar * xr - ai * xi + br
            nxi = ar * xi + ai * xr + bi
            if store:
                bu_ref[pl.ds(r0, SUBLANES), c_re] = nxr
                bu_ref[pl.ds(r0, SUBLANES), c_im] = nxi
            return nxr, nxi

        if chain:
            zero = jnp.zeros((SUBLANES, S5_COLS), F32)
            er, ei = lax.fori_loop(0, nsteps, functools.partial(step, store=False), (zero, zero))
            asr = lam_ref[0, 2:3, c_re]
            asi = lam_ref[0, 3:4, c_re]
            pr = st_ref[0, 0:1, c_re]
            pi = st_ref[1, 0:1, c_re]
            x0r, x0i = zero, zero
            for j in range(SUBLANES):
                x0r = jnp.where(rowid == j, jnp.broadcast_to(pr, (SUBLANES, S5_COLS)), x0r)
                x0i = jnp.where(rowid == j, jnp.broadcast_to(pi, (SUBLANES, S5_COLS)), x0i)
                nr = asr * pr - asi * pi + er[j:j + 1]
                ni = asr * pi + asi * pr + ei[j:j + 1]
                pr, pi = nr, ni
            st_ref[0, :, c_re] = jnp.broadcast_to(pr, (SUBLANES, S5_COLS))
            st_ref[1, :, c_re] = jnp.broadcast_to(pi, (SUBLANES, S5_COLS))
            lax.fori_loop(0, nsteps, functools.partial(step, store=True), (x0r, x0i))
        else:
            fr, fi = lax.fori_loop(0, nsteps, functools.partial(step, store=True),
                                   (st_ref[0, :, c_re], st_ref[1, :, c_re]))
            st_ref[0, :, c_re] = fr
            st_ref[1, :, c_re] = fi

    y = _dot(bu_ref[...].astype(BF16), c_ref[0]) + d_ref[...] * u
    z = _gelu_tanh(y).astype(BF16)
    z_ref[...] = _dot(pt_ref[...], z).astype(z_ref.dtype)

    @pl.when(rb == pl.num_programs(1) - 1)
    def _():
        ore_ref[...] = st_ref[0]
        oim_ref[...] = st_ref[1]


def _s5_call(row0, rows, t_rows, h_all, bmat, cmat, lam, d_skip, s_re, s_im, nsteps, chain, prev):
    d = h_all.shape[1]
    kt = bmat.shape[0]
    nc = bmat.shape[2] // 2
    rblk = nsteps * SUBLANES
    blk0 = row0 // rblk
    perm, perm_t = _s5_perm(nsteps)
    fixed = lambda k, r: (0, 0)
    in_specs = [pl.BlockSpec((rblk, S5_KTILE), lambda k, r: (blk0 + r, k)),
                pl.BlockSpec((rblk, rblk), fixed),
                pl.BlockSpec((rblk, rblk), fixed),
                pl.BlockSpec((1, S5_KTILE, 2 * nc), lambda k, r: (k, 0, 0)),
                pl.BlockSpec((1, 2 * nc, S5_KTILE), lambda k, r: (k, 0, 0)),
                pl.BlockSpec((1, SUBLANES, nc), lambda k, r: (k, 0, 0)),
                pl.BlockSpec((1, S5_KTILE), lambda k, r: (0, k)),
                pl.BlockSpec((SUBLANES, nc), lambda k, r: (0, k)),
                pl.BlockSpec((SUBLANES, nc), lambda k, r: (0, k))]
    args = [h_all, perm, perm_t, bmat, cmat, lam, d_skip, s_re, s_im]
    aliases = {}
    if prev is not None:
        aliases[len(args)] = 0
        in_specs.append(pl.BlockSpec(memory_space=pl.ANY))
        args.append(prev)
    return pl.pallas_call(
        functools.partial(_s5_kernel, nsteps=nsteps, chain=chain),
        grid=(kt, rows // rblk),
        in_specs=in_specs,
        out_specs=[pl.BlockSpec((rblk, S5_KTILE), lambda k, r: (blk0 + r, k)),
                   pl.BlockSpec((SUBLANES, nc), lambda k, r: (0, k)),
                   pl.BlockSpec((SUBLANES, nc), lambda k, r: (0, k))],
        out_shape=[jax.ShapeDtypeStruct((t_rows, d), BF16),
                   jax.ShapeDtypeStruct(s_re.shape, F32),
                   jax.ShapeDtypeStruct(s_im.shape, F32)],
        scratch_shapes=[pltpu.VMEM((rblk, 2 * nc), F32), pltpu.VMEM((2, SUBLANES, nc), F32)],
        input_output_aliases=aliases,
        compiler_params=_cparams(2),
        name="s5_scan",
    )(*args)


def _mm_kernel(*refs, mode, lrep, n_w, n_alias):
    lhs_ref = refs[0]
    w_refs = refs[1:1 + n_w]
    pos = 1 + n_w
    i = pl.program_id(1)
    wbf = refs[len(refs) - n_w:]
    o_ref = refs[len(refs) - n_w - 1]

    @pl.when(i == 0)
    def _():
        for w_ref, s_ref in zip(w_refs, wbf):
            s_ref[...] = w_ref[...].astype(BF16)

    lhs = lhs_ref[...]
    if mode == "glu":
        b_ref, zt_ref, x_ref, g_ref = refs[pos:pos + 4]
        t = _dot(lhs, wbf[0][...]) + b_ref[...]
        o = zt_ref[...].astype(F32) * _sigmoid(t)
        o_ref[...] = x_ref[...] + _expand_rows(g_ref[...], lrep) * o
    elif mode == "res":
        x_ref, g_ref = refs[pos:pos + 2]
        o_ref[...] = x_ref[...] + _expand_rows(g_ref[...], lrep) * _dot(lhs, wbf[0][...])
    elif mode == "plain":
        o_ref[...] = _dot(lhs, wbf[0][...]).astype(o_ref.dtype)
    elif mode == "swiglu":
        o_ref[...] = (_silu(_dot(lhs, wbf[0][...])) * _dot(lhs, wbf[1][...])).astype(o_ref.dtype)


def _mm_call(mode, reg, t_rows, lhs, w_list, w_colblk0, n_out, tn, out_dtype, extras, prev, name):
    k_dim = lhs.shape[1]
    nj = n_out // tn
    in_specs = [pl.BlockSpec((reg.tile, k_dim), lambda j, i: (reg.blk0 + i, 0))]
    args = [lhs]
    for w, c0 in zip(w_list, w_colblk0):
        in_specs.append(pl.BlockSpec((k_dim, tn), lambda j, i, c0=c0: (0, c0 + j)))
        args.append(w)
    for a, kind in extras:
        if kind == "col":
            in_specs.append(pl.BlockSpec((1, tn), lambda j, i: (0, j)))
        elif kind == "tile":
            in_specs.append(pl.BlockSpec((reg.tile, tn), lambda j, i: (reg.blk0 + i, j)))
        else:
            in_specs.append(pl.BlockSpec((reg.nseq, tn), lambda j, i: (0, j)))
        args.append(a)
    aliases = {}
    if prev is not None:
        aliases[len(args)] = 0
        in_specs.append(pl.BlockSpec(memory_space=pl.ANY))
        args.append(prev)
    return pl.pallas_call(
        functools.partial(_mm_kernel, mode=mode, lrep=reg.lrep, n_w=len(w_list), n_alias=len(aliases)),
        grid=(nj, reg.ntiles),
        in_specs=in_specs,
        out_specs=pl.BlockSpec((reg.tile, tn), lambda j, i: (reg.blk0 + i, j)),
        out_shape=jax.ShapeDtypeStruct((t_rows, n_out), out_dtype),
        scratch_shapes=[pltpu.VMEM((k_dim, tn), BF16) for _ in w_list],
        input_output_aliases=aliases,
        compiler_params=_cparams(2),
        name=name,
    )(*args)


def _gla_consts(chunk):
    nlev = int(np.log2(chunk))
    assert 1 << nlev == chunk
    tri = np.tril(np.ones((chunk, chunk), np.float32))
    r = np.arange(chunk)
    wall, masks = [tri], []
    for l in range(nlev):
        w = chunk >> (l + 1)
        blk = r // (2 * w)
        second = (r & w) != 0
        wall.append(tri[blk * 2 * w + w - 1])
        masks.append(((blk[:, None] == blk[None, :]) & second[:, None] & (~second)[None, :]).astype(np.float32))
    masks.append(np.eye(chunk, dtype=np.float32))
    return jnp.asarray(np.concatenate(wall, 0), BF16), jnp.asarray(np.stack(masks, 0), F32)


def _gla_kernel(q_ref, k_ref, v_ref, gate_ref, lg_ref, s0_ref, wall_ref, mask_ref, gn_ref,
                *rest, chunk, nheads, dk, dv):
    o_ref, sout_ref, s_ref = rest[-3:]
    c = pl.program_id(1)
    nlev = mask_ref.shape[0] - 1

    @pl.when(c == 0)
    def _():
        s_ref[...] = s0_ref[0]

    lg = lg_ref[...]
    p1, p2, p3 = _split3(lg)
    wall = wall_ref[...]
    bg = _dot(wall, p1) + _dot(wall, p2) + _dot(wall, p3)
    b = bg[0:chunk]
    q = q_ref[...] * (dk ** -0.5)
    k = k_ref[...]
    row = lax.broadcasted_iota(I32, q.shape, 0)
    qs, ks = [], []
    for l in range(nlev):
        w = chunk >> (l + 1)
        g = bg[(l + 1) * chunk:(l + 2) * chunk]
        second = (row & w) != 0
        e = jnp.exp(jnp.where(second, b - g, g - b))
        qk = jnp.where(second, q, k) * e
        qs.append(jnp.where(second, qk, 0.0).astype(BF16))
        ks.append(jnp.where(second, 0.0, qk).astype(BF16))
    qb = q.astype(BF16)
    kb = k.astype(BF16)
    q_in = (q * jnp.exp(b)).astype(BF16)
    k_dec = (k * jnp.exp(b[chunk - 1:chunk] - b)).astype(BF16)
    ones = jnp.ones((chunk, LANES), BF16)
    gn = gn_ref[...]
    for h in range(nheads):
        ck = slice(h * dk, (h + 1) * dk)
        cv = slice(h * dv, (h + 1) * dv)
        att = _dot_t1(qb[:, ck], kb[:, ck]) * mask_ref[nlev]
        for l in range(nlev):
            att = att + _dot_t1(qs[l][:, ck], ks[l][:, ck]) * mask_ref[l]
        vh = v_ref[:, cv].astype(BF16)
        s_h = s_ref[h]
        o = _dot(att.astype(BF16), vh) + _dot(q_in[:, ck], s_h.astype(BF16))
        dcol = jnp.exp(_dot_t0(p1[:, ck], ones) + _dot_t0(p2[:, ck], ones) + _dot_t0(p3[:, ck], ones))
        s_ref[h] = jnp.concatenate([dcol] * (dv // LANES), axis=1) * s_h + _dot_t0(k_dec[:, ck], vh)
        ms = jnp.mean(o * o, axis=-1, keepdims=True)
        on = o * lax.rsqrt(ms + EPS) * gn
        o_ref[:, cv] = (on * _silu(gate_ref[:, cv])).astype(o_ref.dtype)

    @pl.when(c == pl.num_programs(1) - 1)
    def _():
        sout_ref[0] = s_ref[...]


def _gla_call(reg_row0, nseq, seqlen, chunk, t_rows, proj, lg, s0, g_norm, prev):
    nheads, dk, dv = s0.shape[1:]
    hk = nheads * dk
    d = nheads * dv
    nch = seqlen // chunk
    rb0 = reg_row0 // chunk
    wall, masks = _gla_consts(chunk)
    rowblk = lambda b, c: rb0 + b * nch + c
    in_specs = [pl.BlockSpec((chunk, hk), lambda b, c: (rowblk(b, c), 0)),
                pl.BlockSpec((chunk, hk), lambda b, c: (rowblk(b, c), 1)),
                pl.BlockSpec((chunk, d), lambda b, c: (rowblk(b, c), 1)),
                pl.BlockSpec((chunk, d), lambda b, c: (rowblk(b, c), 2)),
                pl.BlockSpec((chunk, hk), lambda b, c: (rowblk(b, c), 0)),
                pl.BlockSpec((1, nheads, dk, dv), lambda b, c: (b, 0, 0, 0)),
                pl.BlockSpec(wall.shape, lambda b, c: (0, 0)),
                pl.BlockSpec(masks.shape, lambda b, c: (0, 0, 0)),
                pl.BlockSpec((1, dv), lambda b, c: (0, 0))]
    args = [proj, proj, proj, proj, lg, s0, wall, masks, g_norm]
    aliases = {}
    if prev is not None:
        aliases[len(args)] = 0
        in_specs.append(pl.BlockSpec(memory_space=pl.ANY))
        args.append(prev)
    return pl.pallas_call(
        functools.partial(_gla_kernel, chunk=chunk, nheads=nheads, dk=dk, dv=dv),
        grid=(nseq, nch),
        in_specs=in_specs,
        out_specs=[pl.BlockSpec((chunk, d), lambda b, c: (rowblk(b, c), 0)),
                   pl.BlockSpec((1, nheads, dk, dv), lambda b, c: (b, 0, 0, 0))],
        out_shape=[jax.ShapeDtypeStruct((t_rows, d), BF16),
                   jax.ShapeDtypeStruct(s0.shape, F32)],
        scratch_shapes=[pltpu.VMEM((nheads, dk, dv), F32)],
        input_output_aliases=aliases,
        compiler_params=_cparams(2),
        name="gla_chunk",
    )(*args)


def _dispatch_plan(idx, rank, counts, n_tokens):
    n_experts = counts.shape[0]
    rb = EXP_ROWBLK
    a = n_tokens * TOP_K
    blocks_e = (counts + rb - 1) // rb
    padded = blocks_e * rb
    pad_start = jnp.cumsum(padded) - padded
    hit = idx[:, :, None] == jnp.arange(n_experts, dtype=I32)[None, None, :]
    dest = jnp.sum(jnp.where(hit, pad_start[None, None, :], 0), axis=-1) + rank
    np_rows = (a + rb - 1) // rb * rb + rb * n_experts
    items_e = (blocks_e + EXP_NBLK - 1) // EXP_NBLK
    item_end = jnp.cumsum(items_e)
    item_start = item_end - items_e
    n_items = n_experts + (np_rows // rb) // EXP_NBLK
    ii = jnp.arange(n_items, dtype=I32)
    total = item_end[-1]
    e_of = jnp.minimum(jnp.sum(item_end[None, :] <= ii[:, None], axis=1), n_experts - 1).astype(I32)
    valid = ii < total
    local = ii - item_start[e_of]
    e_last = e_of[jnp.maximum(total - 1, 0)]
    ie = jnp.where(valid, e_of, e_last).astype(I32)
    rsb = jnp.where(valid, pad_start[e_of] // rb + local * EXP_NBLK, 0).astype(I32)
    nrb = jnp.where(valid, jnp.minimum(EXP_NBLK, blocks_e[e_of] - local * EXP_NBLK), 0).astype(I32)
    return ie, rsb, nrb, pad_start.astype(I32), dest.astype(I32), np_rows, n_items


def _dispatch_kernel(cnt_ref, pst_ref, dest_hbm, h_ref, xs_hbm, idx_ref, hbuf, zrow, sem_i, sem_d, sem_z,
                     *, ntiles, n_experts):
    i = pl.program_id(0)
    slot = i % 2
    tile = h_ref.shape[0]
    nidx = tile * TOP_K
    rb = EXP_ROWBLK

    def idx_copy(t, s):
        return pltpu.make_async_copy(dest_hbm.at[pl.ds(pl.multiple_of(t * nidx, nidx), nidx)],
                                     idx_ref.at[pl.ds(pl.multiple_of(s * nidx, nidx), nidx)], sem_i.at[s])

    def wait_rows(s):
        for _ in range(TOP_K):
            pltpu.make_async_copy(hbuf.at[s], xs_hbm.at[pl.ds(0, tile)], sem_d.at[s]).wait()

    @pl.when(i == 0)
    def _():
        idx_copy(0, 0).start()

    idx_copy(i, slot).wait()

    @pl.when(i + 1 < ntiles)
    def _():
        idx_copy(jnp.minimum(i + 1, ntiles - 1), 1 - slot).start()

    hbuf[slot] = h_ref[...]

    def tok(t, carry):
        base = slot * nidx + t * TOP_K
        for j in range(TOP_K):
            p = idx_ref[base + j]
            pltpu.make_async_copy(hbuf.at[slot, pl.ds(t, 1)], xs_hbm.at[pl.ds(p, 1)], sem_d.at[slot]).start()
        return carry
    lax.fori_loop(0, tile, tok, 0)

    @pl.when(i > 0)
    def _():
        wait_rows(1 - slot)

    @pl.when(i == ntiles - 1)
    def _():
        wait_rows(slot)
        zrow[...] = jnp.zeros(zrow.shape, zrow.dtype)

        def expert(e, carry):
            cnt = cnt_ref[e]
            base = pst_ref[e]
            end = (cnt + rb - 1) // rb * rb

            def zstart(r, c2):
                pltpu.make_async_copy(zrow.at[pl.ds(0, 1)], xs_hbm.at[pl.ds(base + r, 1)], sem_z).start()
                return c2

            def zwait(r, c2):
                pltpu.make_async_copy(zrow.at[pl.ds(0, 1)], xs_hbm.at[pl.ds(0, 1)], sem_z).wait()
                return c2
            lax.fori_loop(cnt, end, zstart, 0)
            lax.fori_loop(cnt, end, zwait, 0)
            return carry
        lax.fori_loop(0, n_experts, expert, 0)


def _dispatch_call(counts, pad_start, dest_flat, h_packed, np_rows, tile):
    t_rows, dh = h_packed.shape
    ntiles = t_rows // tile
    n_experts = counts.shape[0]
    grid_spec = pltpu.PrefetchScalarGridSpec(
        num_scalar_prefetch=2,
        grid=(ntiles,),
        in_specs=[pl.BlockSpec(memory_space=pl.ANY),
                  pl.BlockSpec((tile, dh), lambda i, c, p: (i, 0))],
        out_specs=pl.BlockSpec(memory_space=pl.ANY),
        scratch_shapes=[pltpu.SMEM((2 * tile * TOP_K,), I32),
                        pltpu.VMEM((2, tile, dh), jnp.uint32),
                        pltpu.VMEM((SUBLANES, dh), jnp.uint32),
                        pltpu.SemaphoreType.DMA((2,)),
                        pltpu.SemaphoreType.DMA((2,)),
                        pltpu.SemaphoreType.DMA])
    return pl.pallas_call(
        functools.partial(_dispatch_kernel, ntiles=ntiles, n_experts=n_experts),
        grid_spec=grid_spec,
        out_shape=jax.ShapeDtypeStruct((np_rows, dh), jnp.uint32),
        compiler_params=_cparams(1),
        name="moe_dispatch",
    )(counts, pad_start, dest_flat, h_packed)


def _experts_kernel(ie_ref, rsb_ref, nrb_ref, xs_hbm, wi_ref, wo_ref, y_hbm,
                    xt, acc, act, ybuf, wibf, wobf, sem_x, sem_y, ycnt_ref, *, n_items, n_k, n_b):
    i = pl.program_id(0)
    st = pl.program_id(1)
    nrb = nrb_ref[i]
    rb = EXP_ROWBLK
    tkw = xt.shape[2]
    tdw = ybuf.shape[2]
    f = act.shape[1]

    def x_copy(item, kt, bi):
        src = pl.multiple_of((rsb_ref[item] + bi) * rb, rb)
        return pltpu.make_async_copy(
            xs_hbm.at[pl.ds(src, rb), pl.ds(pl.multiple_of(kt * tkw, tkw), tkw)],
            xt.at[kt % 2, pl.ds(pl.multiple_of(bi * rb, rb), rb)], sem_x.at[kt % 2])

    def start_x(item, kt):
        def blk(bi, carry):
            x_copy(item, kt, bi).start()
            return carry
        lax.fori_loop(0, nrb_ref[item], blk, 0)

    def wait_x(item, kt):
        def blk(bi, carry):
            x_copy(item, kt, bi).wait()
            return carry
        lax.fori_loop(0, nrb_ref[item], blk, 0)

    def wait_out(s):
        def blk(bi, carry):
            pltpu.make_async_copy(ybuf.at[s, pl.ds(0, rb)], y_hbm.at[pl.ds(0, rb), pl.ds(0, tdw)],
                                  sem_y.at[s]).wait()
            return carry
        lax.fori_loop(0, ycnt_ref[s], blk, 0)
        ycnt_ref[s] = 0

    @pl.when((i == 0) & (st == 0))
    def _():
        ycnt_ref[0] = 0
        ycnt_ref[1] = 0
        xt[...] = jnp.zeros(xt.shape, xt.dtype)
        start_x(0, 0)

    @pl.when(st < n_k)
    def _():
        wait_x(i, st)

    @pl.when(st + 1 < n_k)
    def _():
        start_x(i, st + 1)

    @pl.when((st == n_k) & (i + 1 < n_items))
    def _():
        start_x(jnp.minimum(i + 1, n_items - 1), 0)

    n_big = nrb // EXP_CHUNK_BLKS
    n_small = nrb - n_big * EXP_CHUNK_BLKS

    def for_chunks(fn):
        def big(ci, carry):
            fn(ci * EXP_CHUNK_BLKS, EXP_CHUNK_BLKS)
            return carry

        def small(si, carry):
            fn(n_big * EXP_CHUNK_BLKS + si, 1)
            return carry
        lax.fori_loop(0, n_big, big, 0)
        lax.fori_loop(0, n_small, small, 0)

    def rows_of(b0, nblk):
        return pl.ds(pl.multiple_of(b0 * rb, rb), nblk * rb)

    @pl.when((st < n_k) & (nrb > 0))
    def _():
        wibf[...] = wi_ref[0, 0].astype(BF16)
        slot = st % 2

        def partial_sum(rows):
            return _dot(_unpack_pairs(xt[slot, rows, :], BF16), wibf[...])

        if n_k > 1:
            @pl.when(st == 0)
            def _():
                def first(b0, nblk):
                    rows = rows_of(b0, nblk)
                    acc[rows, :] = partial_sum(rows)
                for_chunks(first)

            @pl.when((st > 0) & (st < n_k - 1))
            def _():
                def middle(b0, nblk):
                    rows = rows_of(b0, nblk)
                    acc[rows, :] = acc[rows, :] + partial_sum(rows)
                for_chunks(middle)

        @pl.when(st == n_k - 1)
        def _():
            def last(b0, nblk):
                rows = rows_of(b0, nblk)
                a = partial_sum(rows)
                if n_k > 1:
                    a = a + acc[rows, :]
                act[rows, :] = (_silu(a[:, :f]) * a[:, f:]).astype(BF16)
            for_chunks(last)

    @pl.when((st >= n_k) & (nrb > 0))
    def _():
        dj = st - n_k
        ys = (i * n_b + dj) % 2
        wobf[...] = wo_ref[0, 0].astype(BF16)
        wait_out(ys)
        row0 = rsb_ref[i] * rb
        col0 = pl.multiple_of(dj * tdw, tdw)

        def chunk(b0, nblk):
            rows = rows_of(b0, nblk)
            ybuf[ys, rows, :] = _pack_pairs(_dot(act[rows, :], wobf[...]))
            for b in range(nblk):
                r0 = pl.multiple_of((b0 + b) * rb, rb)
                pltpu.make_async_copy(
                    ybuf.at[ys, pl.ds(r0, rb)],
                    y_hbm.at[pl.ds(pl.multiple_of(row0 + r0, rb), rb), pl.ds(col0, tdw)],
                    sem_y.at[ys]).start()
        for_chunks(chunk)
        ycnt_ref[ys] = nrb

    @pl.when((i == n_items - 1) & (st == n_k + n_b - 1))
    def _():
        wait_out(0)
        wait_out(1)


def _experts_call(layer, ie, rsb, nrb, n_items, xs, w_in, w_out):
    _, _, d, f2 = w_in.shape
    f = f2 // 2
    np_rows, dh = xs.shape
    tk = min(EXP_TK, d)
    td = min(EXP_TD, d)
    n_k, n_b = d // tk, d // td
    rmax = EXP_NBLK * EXP_ROWBLK

    def kt(st, nr):
        return jnp.where((nr > 0) & (st < n_k), st, n_k - 1)

    def dj(st, nr):
        return jnp.where(nr > 0, jnp.maximum(st - n_k, 0), n_b - 1)

    grid_spec = pltpu.PrefetchScalarGridSpec(
        num_scalar_prefetch=3,
        grid=(n_items, n_k + n_b),
        in_specs=[pl.BlockSpec(memory_space=pl.ANY),
                  pl.BlockSpec((1, 1, tk, f2), lambda i, st, ie, rsb, nrb: (layer, ie[i], kt(st, nrb[i]), 0)),
                  pl.BlockSpec((1, 1, f, td), lambda i, st, ie, rsb, nrb: (layer, ie[i], 0, dj(st, nrb[i])))],
        out_specs=pl.BlockSpec(memory_space=pl.ANY),
        scratch_shapes=[pltpu.VMEM((2, rmax, tk // 2), jnp.uint32),
                        pltpu.VMEM((rmax, f2), F32),
                        pltpu.VMEM((rmax, f), BF16),
                        pltpu.VMEM((2, rmax, td // 2), jnp.uint32),
                        pltpu.VMEM((tk, f2), BF16),
                        pltpu.VMEM((f, td), BF16),
                        pltpu.SemaphoreType.DMA((2,)),
                        pltpu.SemaphoreType.DMA((2,)),
                        pltpu.SMEM((2,), I32)])
    return pl.pallas_call(
        functools.partial(_experts_kernel, n_items=n_items, n_k=n_k, n_b=n_b),
        grid_spec=grid_spec,
        out_shape=jax.ShapeDtypeStruct((np_rows, dh), jnp.uint32),
        compiler_params=_cparams(2),
        name="moe_experts",
    )(ie, rsb, nrb, xs, w_in, w_out)


def _combine_kernel(pos_hbm, w_ref, sh_ref, x_ref, g_ref, fg_ref, y_hbm, *rest,
                    lrep, blk0, ntiles, final):
    o_ref, idx_ref, gbuf, sem_i, sem_g = rest[-5:]
    i = pl.program_id(0)
    slot = i % 2
    nslot = 1 - slot
    tile = x_ref.shape[0]
    nidx = tile * TOP_K

    def idx_copy(t, s):
        return pltpu.make_async_copy(pos_hbm.at[pl.ds(pl.multiple_of((blk0 + t) * nidx, nidx), nidx)],
                                     idx_ref.at[pl.ds(pl.multiple_of(s * nidx, nidx), nidx)], sem_i.at[s])

    def issue_gather(s):
        def tok(t, carry):
            base = s * nidx + t * TOP_K
            for j in range(TOP_K):
                p = idx_ref[base + j]
                pltpu.make_async_copy(y_hbm.at[pl.ds(p, 1)], gbuf.at[s, j, pl.ds(t, 1)], sem_g.at[s]).start()
            return carry
        lax.fori_loop(0, tile, tok, 0)

    @pl.when(i == 0)
    def _():
        first = idx_copy(0, 0)
        first.start()
        first.wait()
        issue_gather(0)
        if ntiles > 1:
            idx_copy(1, 1).start()

    @pl.when(i + 1 < ntiles)
    def _():
        idx_copy(jnp.minimum(i + 1, ntiles - 1), nslot).wait()

    for j in range(TOP_K):
        pltpu.make_async_copy(y_hbm.at[pl.ds(0, tile)], gbuf.at[slot, j], sem_g.at[slot]).wait()

    @pl.when(i + 1 < ntiles)
    def _():
        issue_gather(nslot)

    @pl.when(i + 2 < ntiles)
    def _():
        idx_copy(jnp.minimum(i + 2, ntiles - 1), slot).start()

    acc = sh_ref[...]
    for j in range(TOP_K):
        acc = acc + w_ref[:, j:j + 1] * _unpack_pairs(gbuf[slot, j], F32)
    o = x_ref[...] + _expand_rows(g_ref[...], lrep, tile, i * tile) * acc
    if final:
        ms = jnp.mean(o * o, axis=-1, keepdims=True)
        o = o * lax.rsqrt(ms + EPS) * fg_ref[...]
    o_ref[...] = o


def _combine_call(reg, t_rows, pos_flat, wts, shared, x_all, g, final_g, y_sorted, final, prev):
    d = x_all.shape[1]
    row = lambda i: (reg.blk0 + i, 0)
    fixed = lambda i: (0, 0)
    in_specs = [pl.BlockSpec(memory_space=pl.ANY),
                pl.BlockSpec((reg.tile, TOP_K), row),
                pl.BlockSpec((reg.tile, d), row),
                pl.BlockSpec((reg.tile, d), row),
                pl.BlockSpec((reg.nseq, d), fixed),
                pl.BlockSpec((1, d), fixed),
                pl.BlockSpec(memory_space=pl.ANY)]
    args = [pos_flat, wts, shared, x_all, g, final_g, y_sorted]
    aliases = {}
    if prev is not None:
        aliases[len(args)] = 0
        in_specs.append(pl.BlockSpec(memory_space=pl.ANY))
        args.append(prev)
    if final:
        out_spec = pl.BlockSpec((reg.tile, d), lambda i: (i, 0))
        out_rows = reg.rows
    else:
        out_spec = pl.BlockSpec((reg.tile, d), row)
        out_rows = t_rows
    return pl.pallas_call(
        functools.partial(_combine_kernel, lrep=reg.lrep, blk0=reg.blk0, ntiles=reg.ntiles, final=final),
        grid=(reg.ntiles,),
        in_specs=in_specs,
        out_specs=out_spec,
        out_shape=jax.ShapeDtypeStruct((out_rows, d), F32),
        scratch_shapes=[pltpu.SMEM((2 * reg.tile * TOP_K,), I32),
                        pltpu.VMEM((2, TOP_K, reg.tile, d // 2), jnp.uint32),
                        pltpu.SemaphoreType.DMA((2,)),
                        pltpu.SemaphoreType.DMA((2,))],
        input_output_aliases=aliases,
        compiler_params=_cparams(1),
        name="moe_combine",
    )(*args)


def _both(fn, regs):
    out = fn(regs[0], None)
    return fn(regs[1], out)


def kernel(x_prompt, x_sample, state_s5_re, state_s5_im, state_gla, c_prompt, c_sample, w_ada, b_ada, norm_g, s5_lam_re, s5_lam_im, s5_log_dt, s5_b_re, s5_b_im, s5_c_re, s5_c_im, s5_d, s5_w_glu, s5_b_glu, gla_w_in, gla_w_a1, gla_w_a2, gla_b_a, gla_g_norm, gla_w_o, moe_w_router, moe_b_router, moe_w_in, moe_w_out, moe_ws_in, moe_ws_out, final_g):
    bp, lp, d = x_prompt.shape
    bs, ls, _ = x_sample.shape
    assert bp == 1 and bs == SUBLANES and ls % SUBLANES == 0
    depth = w_ada.shape[0]
    n_exp = moe_w_in.shape[1]
    f_sh = moe_ws_out.shape[1]
    srows = bs * ls
    t_rows = lp + srows
    s5_steps = 64
    s5_blk = s5_steps * SUBLANES
    gla_chunk = 64
    assert lp % PROMPT_TILE == 0 and lp % s5_blk == 0 and lp % srows == 0

    mm_tile = MM_TILE if lp % MM_TILE == 0 else PROMPT_TILE
    regs_mm = (_Region(0, lp, mm_tile, 1, mm_tile), _Region(lp, srows, srows, bs, ls))
    regs_nm = (_Region(0, lp, PROMPT_TILE // 2, 1, PROMPT_TILE // 2), _Region(lp, srows, srows, bs, ls))
    regs_cb = (_Region(0, lp, COMB_TILE, 1, COMB_TILE), _Region(lp, srows, COMB_TILE, bs, ls))

    x_all = jnp.concatenate([x_prompt.reshape(lp, d), x_sample.reshape(srows, d)], axis=0)
    c_all = jnp.concatenate([c_prompt, c_sample, jnp.zeros((2 * SUBLANES - 1 - bs, d), F32)], axis=0)
    mod = _ada_call(c_all, w_ada, b_ada)

    def mods(layer, k):
        m = mod[layer, :, k * d:(k + 1) * d]
        return (m[0:1], m[1:1 + bs])

    new_re_p, new_im_p, new_gla_p, new_re_s, new_im_s, new_gla_s = [], [], [], [], [], []
    for i in range(depth):
        sh1, sc1, g1, sh2, sc2, g2 = [mods(i, k) for k in range(6)]
        j = i // 2
        ng1 = norm_g[i, 0].reshape(1, d)
        ng2 = norm_g[i, 1].reshape(1, d)
        if i % 2 == 0:
            (h_all,) = _both(lambda r, prev: _norm_call(
                "plain", r, t_rows, x_all, ng1, sc1[r.nseq > 1], sh1[r.nseq > 1], [], [(d, F32)], prev), regs_nm)
            grp, pst = s5_lam_re.shape[1:]
            ncol = grp * pst
            lam, bbr, bbi = _s5_disc_call(s5_lam_re[j], s5_lam_im[j], s5_log_dt[j], s5_b_re[j], s5_b_im[j], s5_steps)
            bmat, cmat = _s5_block_weights(bbr, bbi, s5_c_re[j], s5_c_im[j])
            kt = bmat.shape[0]
            lamt = jnp.transpose(lam.reshape(4, kt, ncol // kt), (1, 0, 2))
            lamt = jnp.concatenate([lamt, jnp.zeros_like(lamt)], axis=1)
            dsk = s5_d[j].reshape(1, d)
            zeros_st = jnp.zeros((SUBLANES, ncol), F32)
            z_all, pre, pim = _s5_call(0, lp, t_rows, h_all, bmat, cmat, lamt, dsk, zeros_st, zeros_st,
                                       s5_steps, True, None)
            z_all, sre, sim = _s5_call(lp, srows, t_rows, h_all, bmat, cmat, lamt, dsk,
                                       state_s5_re[j].reshape(bs, ncol), state_s5_im[j].reshape(bs, ncol),
                                       ls, False, z_all)
            new_re_p.append(pre[0].reshape(1, grp, pst))
            new_im_p.append(pim[0].reshape(1, grp, pst))
            new_re_s.append(sre.reshape(bs, grp, pst))
            new_im_s.append(sim.reshape(bs, grp, pst))
            bglu = s5_b_glu[j].reshape(1, d)
            x_all = _both(lambda r, prev: _mm_call(
                "glu", r, t_rows, z_all, [s5_w_glu[j]], [0], d, 512, F32,
                [(bglu, "col"), (z_all, "tile"), (x_all, "tile"), (g1[r.nseq > 1], "seq")], prev, "s5_glu"), regs_mm)
        else:
            hk = gla_w_a2.shape[2]
            assert 2 * hk == d
            h_bf, lg = _both(lambda r, prev: _norm_call(
                "gate", r, t_rows, x_all, ng1, sc1[r.nseq > 1], sh1[r.nseq > 1],
                [gla_w_a1[j], gla_w_a2[j], gla_b_a[j].reshape(1, hk)], [(d, BF16), (hk, F32)], prev), regs_nm)
            proj = _both(lambda r, prev: _mm_call(
                "plain", r, t_rows, h_bf, [gla_w_in[j]], [0], 3 * d, 512, F32, [], prev, "gla_proj"), regs_mm)
            gn = gla_g_norm[j].reshape(1, -1)
            s0p = jnp.zeros((1,) + state_gla.shape[2:], F32)
            o_all, gla_p = _gla_call(0, 1, lp, gla_chunk, t_rows, proj, lg, s0p, gn, None)
            o_all, gla_s = _gla_call(lp, bs, ls, ls, t_rows, proj, lg, state_gla[j], gn, o_all)
            new_gla_p.append(gla_p)
            new_gla_s.append(gla_s)
            x_all = _both(lambda r, prev: _mm_call(
                "res", r, t_rows, o_all, [gla_w_o[j]], [0], d, 512, F32,
                [(x_all, "tile"), (g1[r.nseq > 1], "seq")], prev, "gla_out"), regs_mm)

        brt = moe_b_router[i].reshape(1, n_exp)
        r_p = _router_call(regs_nm[0], t_rows, x_all, ng2, sc2[0], sh2[0], moe_w_router[i], brt,
                           jnp.zeros((SUBLANES, n_exp), F32), None)
        h_pk, h_bf, idx, wts, rank, cnt = _router_call(
            regs_nm[1], t_rows, x_all, ng2, sc2[1], sh2[1], moe_w_router[i], brt, r_p[5], r_p[:5])
        counts = cnt[0].astype(I32)
        ie, rsb, nrb, pad_start, dest, np_rows, n_items = _dispatch_plan(idx, rank, counts, t_rows)
        pos_flat = dest.reshape(-1)
        xs = _dispatch_call(counts, pad_start, pos_flat, h_pk, np_rows, DISP_TILE)
        y_sorted = _experts_call(i, ie, rsb, nrb, n_items, xs, moe_w_in, moe_w_out)
        act = _both(lambda r, prev: _mm_call(
            "swiglu", r, t_rows, h_bf, [moe_ws_in[i], moe_ws_in[i]], [0, f_sh // 256], f_sh, 256, BF16,
            [], prev, "shared_in"), regs_mm)
        shared = _both(lambda r, prev: _mm_call(
            "plain", r, t_rows, act, [moe_ws_out[i]], [0], d, 512, F32, [], prev, "shared_out"), regs_mm)
        last = i == depth - 1
        fg = final_g.reshape(1, d)
        if last:
            y_prompt, y_sample = [_combine_call(r, t_rows, pos_flat, wts, shared, x_all, g2[r.nseq > 1], fg,
                                                y_sorted, True, None) for r in regs_cb]
        else:
            x_all = _both(lambda r, prev: _combine_call(
                r, t_rows, pos_flat, wts, shared, x_all, g2[r.nseq > 1], fg, y_sorted, False, prev), regs_cb)

    y_prompt = y_prompt.reshape(bp, lp, d)
    y_sample = y_sample.reshape(bs, ls, d)
    return (y_prompt, y_sample, jnp.stack(new_re_p), jnp.stack(new_im_p), jnp.stack(new_gla_p),
            jnp.stack(new_re_s), jnp.stack(new_im_s), jnp.stack(new_gla_s))
```

```python
import functools

import numpy as np
import jax
import jax.numpy as jnp
from jax import lax
from jax.experimental import pallas as pl
from jax.experimental.pallas import tpu as pltpu

F32 = jnp.float32
BF16 = jnp.bfloat16
I32 = jnp.int32

EPS = 1e-6
GLA_TAU = 16.0
TOP_K = 8
N_GROUPS = 8
TOPK_GROUPS = 4
ROUTED_SCALE = 2.5

LANES = 128
SUBLANES = 8
MXU_DIM = 256
VMEM_LIMIT = 56 << 20

PROMPT_TILE = 512
MM_TILE = 1024
S5_KTILE = 256
S5_COLS = 512
EXP_ROWBLK = 128
EXP_NBLK = 18
EXP_CHUNK_BLKS = 4
EXP_TK = 1024
EXP_TD = 1024
PACK_GROUP = 512
DISP_TILE = 128
COMB_TILE = 128


def _cparams(n_axes, vmem=VMEM_LIMIT):
    return pltpu.CompilerParams(dimension_semantics=("arbitrary",) * n_axes, vmem_limit_bytes=vmem)


class _Region:
    def __init__(self, row0, rows, tile, nseq, lrep):
        self.row0, self.rows, self.tile, self.nseq, self.lrep = row0, rows, tile, nseq, lrep
        self.blk0 = row0 // tile
        self.ntiles = rows // tile
        assert row0 % tile == 0 and rows % tile == 0


def _expand_rows(m, lrep, rows=None, row0=0):
    nseq, n = m.shape
    if nseq == 1:
        return m
    rows = nseq * lrep if rows is None else rows
    r = lax.broadcasted_iota(I32, (rows, n), 0) + row0
    out = jnp.broadcast_to(m[0:1, :], (rows, n))
    for b in range(1, nseq):
        out = jnp.where(r >= b * lrep, jnp.broadcast_to(m[b:b + 1, :], (rows, n)), out)
    return out


def _sigmoid(x):
    return 1.0 / (1.0 + jnp.exp(-x))


def _silu(x):
    return x * _sigmoid(x)


def _gelu_tanh(x):
    return 0.5 * x * (1.0 + jnp.tanh(0.7978845608028654 * (x + 0.044715 * x * x * x)))


def _log_sigmoid(x):
    return jnp.minimum(x, 0.0) - jnp.log1p(jnp.exp(-jnp.abs(x)))


def _split3(x):
    p1 = x.astype(BF16)
    r1 = x - p1.astype(F32)
    p2 = r1.astype(BF16)
    r2 = r1 - p2.astype(F32)
    return p1, p2, r2.astype(BF16)


def _pack_pairs(x):
    half = PACK_GROUP // 2
    words = []
    for g in range(x.shape[1] // PACK_GROUP):
        lo = pltpu.bitcast(x[:, g * PACK_GROUP:g * PACK_GROUP + half].astype(BF16).astype(F32), jnp.uint32)
        hi = pltpu.bitcast(x[:, g * PACK_GROUP + half:(g + 1) * PACK_GROUP].astype(BF16).astype(F32), jnp.uint32)
        words.append(hi | (lo >> 16))
    return words[0] if len(words) == 1 else jnp.concatenate(words, axis=1)


def _unpack_pairs(w, dtype):
    half = PACK_GROUP // 2
    parts = []
    for g in range(w.shape[1] // half):
        ww = w[:, g * half:(g + 1) * half]
        parts.append(pltpu.bitcast(ww << 16, F32).astype(dtype))
        parts.append(pltpu.bitcast(ww & jnp.uint32(0xFFFF0000), F32).astype(dtype))
    return jnp.concatenate(parts, axis=1)


def _dot(a, b):
    return jnp.dot(a, b, preferred_element_type=F32)


def _dot_t0(a, b):
    return lax.dot_general(a, b, (((0,), (0,)), ((), ())), preferred_element_type=F32)


def _dot_t1(a, b):
    return lax.dot_general(a, b, (((1,), (1,)), ((), ())), preferred_element_type=F32)


def _ada_kernel(c_ref, w_ref, b_ref, o_ref):
    s = _silu(c_ref[...])
    o_ref[0] = _dot(s.astype(BF16), w_ref[0].astype(BF16)) + b_ref[0]


def _ada_call(c_all, w_ada, b_ada):
    depth, d, n6 = w_ada.shape
    nc = c_all.shape[0]
    tn = 512
    return pl.pallas_call(
        _ada_kernel,
        grid=(depth, n6 // tn),
        in_specs=[pl.BlockSpec((nc, d), lambda l, j: (0, 0)),
                  pl.BlockSpec((1, d, tn), lambda l, j: (l, 0, j)),
                  pl.BlockSpec((1, 1, tn), lambda l, j: (l, 0, j))],
        out_specs=pl.BlockSpec((1, nc, tn), lambda l, j: (l, 0, j)),
        out_shape=jax.ShapeDtypeStruct((depth, nc, n6), F32),
        compiler_params=_cparams(2),
        name="adaln",
    )(c_all, w_ada, b_ada.reshape(depth, 1, n6))


def _norm_mod(x, g, sc, sh, lrep):
    ms = jnp.mean(x * x, axis=-1, keepdims=True)
    y = x * lax.rsqrt(ms + EPS) * g
    return y * (1.0 + _expand_rows(sc, lrep)) + _expand_rows(sh, lrep)


def _norm_kernel(x_ref, g_ref, sc_ref, sh_ref, *rest, lrep, aliased):
    o_ref = rest[-1]
    h = _norm_mod(x_ref[...], g_ref[...], sc_ref[...], sh_ref[...], lrep)
    o_ref[...] = h.astype(o_ref.dtype)


def _cols_to_lanes(cols, dtype):
    r = cols[0].shape[0]
    lane = lax.broadcasted_iota(I32, (r, len(cols)), 1)
    out = jnp.zeros((r, len(cols)), dtype)
    for j, c in enumerate(cols):
        out = jnp.where(lane == j, c.astype(dtype), out)
    return out


def _router_kernel(x_ref, g_ref, sc_ref, sh_ref, w_ref, b_ref, cin_ref, *rest, lrep):
    hp_ref, hb_ref, idx_ref, wt_ref, rk_ref, cout_ref, run_ref = rest[-7:]
    i = pl.program_id(0)

    @pl.when(i == 0)
    def _():
        run_ref[...] = cin_ref[0:1, :]

    h = _norm_mod(x_ref[...], g_ref[...], sc_ref[...], sh_ref[...], lrep)
    hb_ref[...] = h.astype(BF16)
    hp_ref[...] = _pack_pairs(h)
    w = w_ref[...]
    hh = h.astype(BF16)
    hl = (h - hh.astype(F32)).astype(BF16)
    wh = w.astype(BF16)
    wl = (w - wh.astype(F32)).astype(BF16)
    scores = _sigmoid(_dot(hh, wh) + _dot(hl, wh) + _dot(hh, wl))

    rows, n_exp = scores.shape
    per = n_exp // N_GROUPS
    neg = -jnp.inf
    big = n_exp + 1
    lane_i = lax.broadcasted_iota(I32, (rows, n_exp), 1)
    lane = lane_i.astype(F32)
    grp_i = lane_i // per
    grp = grp_i.astype(F32)
    choice = scores + b_ref[...]
    gs = jnp.zeros((rows, n_exp), F32)
    for g in range(N_GROUPS):
        ing = grp_i == g
        m = jnp.where(ing, choice, neg)
        m1 = jnp.max(m, axis=-1, keepdims=True)
        top = m == m1
        ntop = jnp.sum(jnp.where(top, 1.0, 0.0), axis=-1, keepdims=True)
        below = jnp.max(jnp.where(top, neg, m), axis=-1, keepdims=True)
        m2 = jnp.where(ntop > 1.5, m1, below)
        gs = jnp.where(ing, m1 + m2, gs)
    masked = jnp.full((rows, n_exp), neg, F32)
    for _ in range(TOPK_GROUPS):
        mx = jnp.max(gs, axis=-1, keepdims=True)
        gi = jnp.min(jnp.where(gs == mx, grp, big), axis=-1, keepdims=True)
        sel = grp == gi
        masked = jnp.where(sel, choice, masked)
        gs = jnp.where(sel, neg, gs)
    idx_cols, w_cols, hits = [], [], []
    for _ in range(TOP_K):
        mx = jnp.max(masked, axis=-1, keepdims=True)
        ei = jnp.min(jnp.where(masked == mx, lane, big), axis=-1, keepdims=True)
        hit = lane == ei
        hits.append(hit)
        idx_cols.append(ei)
        w_cols.append(jnp.sum(jnp.where(hit, scores, 0.0), axis=-1, keepdims=True))
        masked = jnp.where(hit, neg, masked)
    wsum = w_cols[0]
    for c in w_cols[1:]:
        wsum = wsum + c
    scale = ROUTED_SCALE / wsum
    idx_ref[...] = _cols_to_lanes(idx_cols, I32)
    wt_ref[...] = _cols_to_lanes([c * scale for c in w_cols], F32)
    onehot = jnp.zeros((rows, n_exp), F32)
    for hit in hits:
        onehot = jnp.where(hit, 1.0, onehot)
    rr = lax.broadcasted_iota(I32, (rows, rows), 0)
    cc = lax.broadcasted_iota(I32, (rows, rows), 1)
    before = jnp.where(rr > cc, 1.0, 0.0).astype(BF16)
    cum = _dot(before, onehot.astype(BF16)) + run_ref[...]
    rk_ref[...] = _cols_to_lanes(
        [jnp.sum(jnp.where(hit, cum, 0.0), axis=-1, keepdims=True) for hit in hits], I32)
    run_ref[...] = run_ref[...] + jnp.sum(onehot, axis=0, keepdims=True)
    cout_ref[...] = jnp.broadcast_to(run_ref[...], cout_ref.shape)


def _router_call(reg, t_rows, x_all, g, sc, sh, w_router, b_router, cnt_in, prev):
    d = x_all.shape[1]
    n_exp = w_router.shape[1]
    row = lambda i: (reg.blk0 + i, 0)
    fixed = lambda i: (0, 0)
    in_specs = [pl.BlockSpec((reg.tile, d), row), pl.BlockSpec((1, d), fixed),
                pl.BlockSpec((reg.nseq, d), fixed), pl.BlockSpec((reg.nseq, d), fixed),
                pl.BlockSpec((d, n_exp), fixed), pl.BlockSpec((1, n_exp), fixed),
                pl.BlockSpec((SUBLANES, n_exp), fixed)]
    args = [x_all, g, sc, sh, w_router, b_router, cnt_in]
    outs = [(d // 2, jnp.uint32), (d, BF16), (TOP_K, I32), (TOP_K, F32), (TOP_K, I32)]
    aliases = {}
    if prev is not None:
        for k in range(len(outs)):
            aliases[len(args)] = k
            in_specs.append(pl.BlockSpec(memory_space=pl.ANY))
            args.append(prev[k])
    res = pl.pallas_call(
        functools.partial(_router_kernel, lrep=reg.lrep),
        grid=(reg.ntiles,),
        in_specs=in_specs,
        out_specs=[pl.BlockSpec((reg.tile, n), row) for n, _ in outs] + [pl.BlockSpec((SUBLANES, n_exp), fixed)],
        out_shape=[jax.ShapeDtypeStruct((t_rows, n), dt) for n, dt in outs]
        + [jax.ShapeDtypeStruct((SUBLANES, n_exp), F32)],
        scratch_shapes=[pltpu.VMEM((1, n_exp), F32)],
        input_output_aliases=aliases,
        compiler_params=_cparams(1),
        name="moe_router",
    )(*args)
    return list(res)


def _norm_gate_kernel(x_ref, g_ref, sc_ref, sh_ref, wa1_ref, wa2_ref, ba_ref, *rest, lrep, aliased):
    h_ref, lg_ref = rest[-2:]
    h = _norm_mod(x_ref[...], g_ref[...], sc_ref[...], sh_ref[...], lrep)
    hb = h.astype(BF16)
    h_ref[...] = hb
    a = _dot(hb, wa1_ref[...].astype(BF16))
    z = _dot(a.astype(BF16), wa2_ref[...].astype(BF16)) + ba_ref[...]
    lg_ref[...] = _log_sigmoid(z) * (1.0 / GLA_TAU)


def _norm_call(kind, reg, t_rows, x_all, g, sc, sh, extra, outs, prev, x_own=False):
    d = x_all.shape[1]
    kern = {"plain": _norm_kernel, "gate": _norm_gate_kernel}[kind]
    row = lambda i: (reg.blk0 + i, 0)
    xrow = (lambda i: (i, 0)) if x_own else row
    fixed = lambda i: (0, 0)
    in_specs = [pl.BlockSpec((reg.tile, d), xrow), pl.BlockSpec((1, d), fixed),
                pl.BlockSpec((reg.nseq, d), fixed), pl.BlockSpec((reg.nseq, d), fixed)]
    args = [x_all, g, sc, sh]
    for e in extra:
        in_specs.append(pl.BlockSpec(e.shape, fixed))
        args.append(e)
    aliases = {}
    if prev is not None:
        for k, p in enumerate(prev):
            aliases[len(args)] = k
            in_specs.append(pl.BlockSpec(memory_space=pl.ANY))
            args.append(p)
    res = pl.pallas_call(
        functools.partial(kern, lrep=reg.lrep, aliased=prev is not None),
        grid=(reg.ntiles,),
        in_specs=in_specs,
        out_specs=[pl.BlockSpec((reg.tile, n), row) for n, _ in outs],
        out_shape=[jax.ShapeDtypeStruct((t_rows, n), dt) for n, dt in outs],
        input_output_aliases=aliases,
        compiler_params=_cparams(1),
        name="norm_" + kind,
    )(*args)
    return list(res)


def _s5_disc_kernel(lr_ref, li_ref, ldt_ref, br_ref, bi_ref, lam_ref, bbr_ref, bbi_ref, *, nsteps):
    lr = lr_ref[...]
    li = li_ref[...]
    dt = jnp.exp(ldt_ref[...])
    mag = jnp.exp(lr * dt)
    ar = mag * jnp.cos(li * dt)
    ai = mag * jnp.sin(li * dt)
    den = lr * lr + li * li
    fr = ((ar - 1.0) * lr + ai * li) / den
    fi = (ai * lr - (ar - 1.0) * li) / den
    lam_ref[0] = ar
    lam_ref[1] = ai
    mags = jnp.exp(nsteps * (lr * dt))
    lam_ref[2] = mags * jnp.cos(nsteps * (li * dt))
    lam_ref[3] = mags * jnp.sin(nsteps * (li * dt))
    for h in range(br_ref.shape[0]):
        br = br_ref[h]
        bi = bi_ref[h]
        bbr_ref[h] = fr * br - fi * bi
        bbi_ref[h] = fr * bi + fi * br


def _s5_disc_call(lam_re, lam_im, log_dt, b_re, b_im, nsteps):
    g, p = lam_re.shape
    hg = b_re.shape[2]
    ldt = jnp.broadcast_to(log_dt[:, None], (g, p))
    brt = jnp.transpose(b_re, (2, 0, 1))
    bit = jnp.transpose(b_im, (2, 0, 1))
    return pl.pallas_call(
        functools.partial(_s5_disc_kernel, nsteps=float(nsteps)),
        out_shape=[jax.ShapeDtypeStruct((4, g, p), F32),
                   jax.ShapeDtypeStruct((hg, g, p), F32),
                   jax.ShapeDtypeStruct((hg, g, p), F32)],
        name="s5_disc",
    )(lam_re, lam_im, ldt, brt, bit)


def _s5_block_weights(bbr, bbi, c_re, c_im):
    hg, g, p = bbr.shape
    gt = S5_KTILE // hg
    kt = g // gt
    eye = jnp.eye(gt, dtype=F32)

    def bd_in(b):
        b = b.reshape(hg, kt, gt, p)
        return jnp.einsum("hkgp,gq->kghqp", b, eye).reshape(kt, gt * hg, gt * p)

    def bd_out(c):
        c = c.reshape(kt, gt, hg, p)
        return jnp.einsum("kghp,gq->kqpgh", c, eye).reshape(kt, gt * p, gt * hg)

    bmat = jnp.concatenate([bd_in(bbr), bd_in(bbi)], axis=2).astype(BF16)
    cmat = jnp.concatenate([bd_out(c_re), -bd_out(c_im)], axis=1).astype(BF16)
    return bmat, cmat


def _s5_perm(nsteps):
    r = np.arange(nsteps * SUBLANES)
    p = np.zeros((r.size, r.size), np.float32)
    p[(r % nsteps) * SUBLANES + r // nsteps, r] = 1.0
    return jnp.asarray(p, BF16), jnp.asarray(p.T, BF16)


def _s5_kernel(h_ref, p_ref, pt_ref, b_ref, c_ref, lam_ref, d_ref, sre_ref, sim_ref, *rest, nsteps, chain):
    z_ref, ore_ref, oim_ref, bu_ref, st_ref = rest[-5:]
    rb = pl.program_id(1)
    nc = sre_ref.shape[1]

    @pl.when(rb == 0)
    def _():
        st_ref[0] = sre_ref[...]
        st_ref[1] = sim_ref[...]

    perm = p_ref[...]
    h1, h2, h3 = _split3(h_ref[...])
    u = _dot(perm, h1) + _dot(perm, h2) + _dot(perm, h3)
    bu_ref[...] = _dot(u.astype(BF16), b_ref[0])
    rowid = lax.broadcasted_iota(I32, (SUBLANES, S5_COLS), 0)

    for cb in range(nc // S5_COLS):
        c_re = slice(cb * S5_COLS, (cb + 1) * S5_COLS)
        c_im = slice(nc + cb * S5_COLS, nc + (cb + 1) * S5_COLS)
        ar = jnp.broadcast_to(lam_ref[0, 0:1, c_re], (SUBLANES, S5_COLS))
        ai = jnp.broadcast_to(lam_ref[0, 1:2, c_re], (SUBLANES, S5_COLS))

        def step(s, carry, store):
            xr, xi = carry
            r0 = pl.multiple_of(s * SUBLANES, SUBLANES)
            br = bu_ref[pl.ds(r0, SUBLANES), c_re]
            bi = bu_ref[pl.ds(r0, SUBLANES), c_im]
            nxr = ar * xr - ai * xi + br
            nxi = ar * xi + ai * xr + bi
            if store:
                bu_ref[pl.ds(r0, SUBLANES), c_re] = nxr
                bu_ref[pl.ds(r0, SUBLANES), c_im] = nxi
            return nxr, nxi

        if chain:
            zero = jnp.zeros((SUBLANES, S5_COLS), F32)
            er, ei = lax.fori_loop(0, nsteps, functools.partial(step, store=False), (zero, zero))
            asr = lam_ref[0, 2:3, c_re]
            asi = lam_ref[0, 3:4, c_re]
            pr = st_ref[0, 0:1, c_re]
            pi = st_ref[1, 0:1, c_re]
            x0r, x0i = zero, zero
            for j in range(SUBLANES):
                x0r = jnp.where(rowid == j, jnp.broadcast_to(pr, (SUBLANES, S5_COLS)), x0r)
                x0i = jnp.where(rowid == j, jnp.broadcast_to(pi, (SUBLANES, S5_COLS)), x0i)
                nr = asr * pr - asi * pi + er[j:j + 1]
                ni = asr * pi + asi * pr + ei[j:j + 1]
                pr, pi = nr, ni
            st_ref[0, :, c_re] = jnp.broadcast_to(pr, (SUBLANES, S5_COLS))
            st_ref[1, :, c_re] = jnp.broadcast_to(pi, (SUBLANES, S5_COLS))
            lax.fori_loop(0, nsteps, functools.partial(step, store=True), (x0r, x0i))
        else:
            fr, fi = lax.fori_loop(0, nsteps, functools.partial(step, store=True),
                                   (st_ref[0, :, c_re], st_ref[1, :, c_re]))
            st_ref[0, :, c_re] = fr
            st_ref[1, :, c_re] = fi

    y = _dot(bu_ref[...].astype(BF16), c_ref[0]) + d_ref[...] * u
    z = _gelu_tanh(y).astype(BF16)
    z_ref[...] = _dot(pt_ref[...], z).astype(z_ref.dtype)

    @pl.when(rb == pl.num_programs(1) - 1)
    def _():
        ore_ref[...] = st_ref[0]
        oim_ref[...] = st_ref[1]


def _s5_call(row0, rows, t_rows, h_all, bmat, cmat, lam, d_skip, s_re, s_im, nsteps, chain, prev):
    d = h_all.shape[1]
    kt = bmat.shape[0]
    nc = bmat.shape[2] // 2
    rblk = nsteps * SUBLANES
    blk0 = row0 // rblk
    perm, perm_t = _s5_perm(nsteps)
    fixed = lambda k, r: (0, 0)
    in_specs = [pl.BlockSpec((rblk, S5_KTILE), lambda k, r: (blk0 + r, k)),
                pl.BlockSpec((rblk, rblk), fixed),
                pl.BlockSpec((rblk, rblk), fixed),
                pl.BlockSpec((1, S5_KTILE, 2 * nc), lambda k, r: (k, 0, 0)),
                pl.BlockSpec((1, 2 * nc, S5_KTILE), lambda k, r: (k, 0, 0)),
                pl.BlockSpec((1, SUBLANES, nc), lambda k, r: (k, 0, 0)),
                pl.BlockSpec((1, S5_KTILE), lambda k, r: (0, k)),
                pl.BlockSpec((SUBLANES, nc), lambda k, r: (0, k)),
                pl.BlockSpec((SUBLANES, nc), lambda k, r: (0, k))]
    args = [h_all, perm, perm_t, bmat, cmat, lam, d_skip, s_re, s_im]
    aliases = {}
    if prev is not None:
        aliases[len(args)] = 0
        in_specs.append(pl.BlockSpec(memory_space=pl.ANY))
        args.append(prev)
    return pl.pallas_call(
        functools.partial(_s5_kernel, nsteps=nsteps, chain=chain),
        grid=(kt, rows // rblk),
        in_specs=in_specs,
        out_specs=[pl.BlockSpec((rblk, S5_KTILE), lambda k, r: (blk0 + r, k)),
                   pl.BlockSpec((SUBLANES, nc), lambda k, r: (0, k)),
                   pl.BlockSpec((SUBLANES, nc), lambda k, r: (0, k))],
        out_shape=[jax.ShapeDtypeStruct((t_rows, d), BF16),
                   jax.ShapeDtypeStruct(s_re.shape, F32),
                   jax.ShapeDtypeStruct(s_im.shape, F32)],
        scratch_shapes=[pltpu.VMEM((rblk, 2 * nc), F32), pltpu.VMEM((2, SUBLANES, nc), F32)],
        input_output_aliases=aliases,
        compiler_params=_cparams(2),
        name="s5_scan",
    )(*args)


def _mm_kernel(*refs, mode, lrep, n_w, n_alias):
    lhs_ref = refs[0]
    w_refs = refs[1:1 + n_w]
    pos = 1 + n_w
    i = pl.program_id(1)
    wbf = refs[len(refs) - n_w:]
    o_ref = refs[len(refs) - n_w - 1]

    @pl.when(i == 0)
    def _():
        for w_ref, s_ref in zip(w_refs, wbf):
            s_ref[...] = w_ref[...].astype(BF16)

    lhs = lhs_ref[...]
    if mode == "glu":
        b_ref, zt_ref, x_ref, g_ref = refs[pos:pos + 4]
        t = _dot(lhs, wbf[0][...]) + b_ref[...]
        o = zt_ref[...].astype(F32) * _sigmoid(t)
        o_ref[...] = x_ref[...] + _expand_rows(g_ref[...], lrep) * o
    elif mode == "res":
        x_ref, g_ref = refs[pos:pos + 2]
        o_ref[...] = x_ref[...] + _expand_rows(g_ref[...], lrep) * _dot(lhs, wbf[0][...])
    elif mode == "plain":
        o_ref[...] = _dot(lhs, wbf[0][...]).astype(o_ref.dtype)
    elif mode == "swiglu":
        o_ref[...] = (_silu(_dot(lhs, wbf[0][...])) * _dot(lhs, wbf[1][...])).astype(o_ref.dtype)


def _mm_call(mode, reg, t_rows, lhs, w_list, w_colblk0, n_out, tn, out_dtype, extras, prev, name):
    k_dim = lhs.shape[1]
    nj = n_out // tn
    in_specs = [pl.BlockSpec((reg.tile, k_dim), lambda j, i: (reg.blk0 + i, 0))]
    args = [lhs]
    for w, c0 in zip(w_list, w_colblk0):
        in_specs.append(pl.BlockSpec((k_dim, tn), lambda j, i, c0=c0: (0, c0 + j)))
        args.append(w)
    for a, kind in extras:
        if kind == "col":
            in_specs.append(pl.BlockSpec((1, tn), lambda j, i: (0, j)))
        elif kind == "tile":
            in_specs.append(pl.BlockSpec((reg.tile, tn), lambda j, i: (reg.blk0 + i, j)))
        elif kind == "own":
            in_specs.append(pl.BlockSpec((reg.tile, tn), lambda j, i: (i, j)))
        else:
            in_specs.append(pl.BlockSpec((reg.nseq, tn), lambda j, i: (0, j)))
        args.append(a)
    aliases = {}
    if prev is not None:
        aliases[len(args)] = 0
        in_specs.append(pl.BlockSpec(memory_space=pl.ANY))
        args.append(prev)
    return pl.pallas_call(
        functools.partial(_mm_kernel, mode=mode, lrep=reg.lrep, n_w=len(w_list), n_alias=len(aliases)),
        grid=(nj, reg.ntiles),
        in_specs=in_specs,
        out_specs=pl.BlockSpec((reg.tile, tn), lambda j, i: (reg.blk0 + i, j)),
        out_shape=jax.ShapeDtypeStruct((t_rows, n_out), out_dtype),
        scratch_shapes=[pltpu.VMEM((k_dim, tn), BF16) for _ in w_list],
        input_output_aliases=aliases,
        compiler_params=_cparams(2),
        name=name,
    )(*args)


def _gla_consts(chunk):
    nlev = int(np.log2(chunk))
    assert 1 << nlev == chunk
    tri = np.tril(np.ones((chunk, chunk), np.float32))
    r = np.arange(chunk)
    wall, masks = [tri], []
    for l in range(nlev):
        w = chunk >> (l + 1)
        blk = r // (2 * w)
        second = (r & w) != 0
        wall.append(tri[blk * 2 * w + w - 1])
        masks.append(((blk[:, None] == blk[None, :]) & second[:, None] & (~second)[None, :]).astype(np.float32))
    masks.append(np.eye(chunk, dtype=np.float32))
    return jnp.asarray(np.concatenate(wall, 0), BF16), jnp.asarray(np.stack(masks, 0), F32)


def _gla_kernel(q_ref, k_ref, v_ref, gate_ref, lg_ref, s0_ref, wall_ref, mask_ref, gn_ref,
                *rest, chunk, nheads, dk, dv):
    o_ref, sout_ref, s_ref = rest[-3:]
    c = pl.program_id(1)
    nlev = mask_ref.shape[0] - 1

    @pl.when(c == 0)
    def _():
        s_ref[...] = s0_ref[0]

    lg = lg_ref[...]
    p1, p2, p3 = _split3(lg)
    wall = wall_ref[...]
    bg = _dot(wall, p1) + _dot(wall, p2) + _dot(wall, p3)
    b = bg[0:chunk]
    q = q_ref[...] * (dk ** -0.5)
    k = k_ref[...]
    row = lax.broadcasted_iota(I32, q.shape, 0)
    qs, ks = [], []
    for l in range(nlev):
        w = chunk >> (l + 1)
        g = bg[(l + 1) * chunk:(l + 2) * chunk]
        second = (row & w) != 0
        e = jnp.exp(jnp.where(second, b - g, g - b))
        qk = jnp.where(second, q, k) * e
        qs.append(jnp.where(second, qk, 0.0).astype(BF16))
        ks.append(jnp.where(second, 0.0, qk).astype(BF16))
    qb = q.astype(BF16)
    kb = k.astype(BF16)
    q_in = (q * jnp.exp(b)).astype(BF16)
    k_dec = (k * jnp.exp(b[chunk - 1:chunk] - b)).astype(BF16)
    ones = jnp.ones((chunk, LANES), BF16)
    dcol_all = jnp.exp(_dot_t0(p1, ones) + _dot_t0(p2, ones) + _dot_t0(p3, ones))
    gn = gn_ref[...]
    for h in range(nheads):
        ck = slice(h * dk, (h + 1) * dk)
        cv = slice(h * dv, (h + 1) * dv)
        att = _dot_t1(qb[:, ck], kb[:, ck]) * mask_ref[nlev]
        for l in range(nlev):
            att = att + _dot_t1(qs[l][:, ck], ks[l][:, ck]) * mask_ref[l]
        vh = v_ref[:, cv].astype(BF16)
        s_h = s_ref[h]
        o = _dot(att.astype(BF16), vh) + _dot(q_in[:, ck], s_h.astype(BF16))
        dcol = dcol_all[h * dk:(h + 1) * dk, :]
        s_ref[h] = jnp.concatenate([dcol] * (dv // LANES), axis=1) * s_h + _dot_t0(k_dec[:, ck], vh)
        ms = jnp.mean(o * o, axis=-1, keepdims=True)
        on = o * lax.rsqrt(ms + EPS) * gn
        o_ref[:, cv] = (on * _silu(gate_ref[:, cv])).astype(o_ref.dtype)

    @pl.when(c == pl.num_programs(1) - 1)
    def _():
        sout_ref[0] = s_ref[...]


def _gla_call(reg_row0, nseq, seqlen, chunk, t_rows, proj, lg, s0, g_norm, prev):
    nheads, dk, dv = s0.shape[1:]
    hk = nheads * dk
    d = nheads * dv
    nch = seqlen // chunk
    rb0 = reg_row0 // chunk
    wall, masks = _gla_consts(chunk)
    rowblk = lambda b, c: rb0 + b * nch + c
    in_specs = [pl.BlockSpec((chunk, hk), lambda b, c: (rowblk(b, c), 0)),
                pl.BlockSpec((chunk, hk), lambda b, c: (rowblk(b, c), 1)),
                pl.BlockSpec((chunk, d), lambda b, c: (rowblk(b, c), 1)),
                pl.BlockSpec((chunk, d), lambda b, c: (rowblk(b, c), 2)),
                pl.BlockSpec((chunk, hk), lambda b, c: (rowblk(b, c), 0)),
                pl.BlockSpec((1, nheads, dk, dv), lambda b, c: (b, 0, 0, 0)),
                pl.BlockSpec(wall.shape, lambda b, c: (0, 0)),
                pl.BlockSpec(masks.shape, lambda b, c: (0, 0, 0)),
                pl.BlockSpec((1, dv), lambda b, c: (0, 0))]
    args = [proj, proj, proj, proj, lg, s0, wall, masks, g_norm]
    aliases = {}
    if prev is not None:
        aliases[len(args)] = 0
        in_specs.append(pl.BlockSpec(memory_space=pl.ANY))
        args.append(prev)
    return pl.pallas_call(
        functools.partial(_gla_kernel, chunk=chunk, nheads=nheads, dk=dk, dv=dv),
        grid=(nseq, nch),
        in_specs=in_specs,
        out_specs=[pl.BlockSpec((chunk, d), lambda b, c: (rowblk(b, c), 0)),
                   pl.BlockSpec((1, nheads, dk, dv), lambda b, c: (b, 0, 0, 0))],
        out_shape=[jax.ShapeDtypeStruct((t_rows, d), BF16),
                   jax.ShapeDtypeStruct(s0.shape, F32)],
        scratch_shapes=[pltpu.VMEM((nheads, dk, dv), F32)],
        input_output_aliases=aliases,
        compiler_params=_cparams(2),
        name="gla_chunk",
    )(*args)


def _dispatch_plan(idx, rank, counts, n_tokens):
    n_experts = counts.shape[0]
    rb = EXP_ROWBLK
    a = n_tokens * TOP_K
    blocks_e = (counts + rb - 1) // rb
    padded = blocks_e * rb
    pad_start = jnp.cumsum(padded) - padded
    hit = idx[:, :, None] == jnp.arange(n_experts, dtype=I32)[None, None, :]
    dest = jnp.sum(jnp.where(hit, pad_start[None, None, :], 0), axis=-1) + rank
    np_rows = (a + rb - 1) // rb * rb + rb * n_experts
    items_e = (blocks_e + EXP_NBLK - 1) // EXP_NBLK
    item_end = jnp.cumsum(items_e)
    item_start = item_end - items_e
    n_items = n_experts + (np_rows // rb) // EXP_NBLK
    ii = jnp.arange(n_items, dtype=I32)
    total = item_end[-1]
    e_of = jnp.minimum(jnp.sum(item_end[None, :] <= ii[:, None], axis=1), n_experts - 1).astype(I32)
    valid = ii < total
    local = ii - item_start[e_of]
    e_last = e_of[jnp.maximum(total - 1, 0)]
    ie = jnp.where(valid, e_of, e_last).astype(I32)
    rsb = jnp.where(valid, pad_start[e_of] // rb + local * EXP_NBLK, 0).astype(I32)
    nrb = jnp.where(valid, jnp.minimum(EXP_NBLK, blocks_e[e_of] - local * EXP_NBLK), 0).astype(I32)
    return ie, rsb, nrb, pad_start.astype(I32), dest.astype(I32), np_rows, n_items


def _dispatch_kernel(cnt_ref, pst_ref, dest_hbm, h_ref, xs_hbm, idx_ref, hbuf, zrow, sem_i, sem_d, sem_z,
                     *, ntiles, n_experts):
    i = pl.program_id(0)
    slot = i % 2
    tile = h_ref.shape[0]
    nidx = tile * TOP_K
    rb = EXP_ROWBLK

    def idx_copy(t, s):
        return pltpu.make_async_copy(dest_hbm.at[pl.ds(pl.multiple_of(t * nidx, nidx), nidx)],
                                     idx_ref.at[pl.ds(pl.multiple_of(s * nidx, nidx), nidx)], sem_i.at[s])

    def wait_rows(s):
        for _ in range(TOP_K):
            pltpu.make_async_copy(hbuf.at[s], xs_hbm.at[pl.ds(0, tile)], sem_d.at[s]).wait()

    @pl.when(i == 0)
    def _():
        idx_copy(0, 0).start()

    idx_copy(i, slot).wait()

    @pl.when(i + 1 < ntiles)
    def _():
        idx_copy(jnp.minimum(i + 1, ntiles - 1), 1 - slot).start()

    hbuf[slot] = h_ref[...]

    def tok(t, carry):
        base = slot * nidx + t * TOP_K
        for j in range(TOP_K):
            p = idx_ref[base + j]
            pltpu.make_async_copy(hbuf.at[slot, pl.ds(t, 1)], xs_hbm.at[pl.ds(p, 1)], sem_d.at[slot]).start()
        return carry
    lax.fori_loop(0, tile, tok, 0)

    @pl.when(i > 0)
    def _():
        wait_rows(1 - slot)

    @pl.when(i == ntiles - 1)
    def _():
        wait_rows(slot)
        zrow[...] = jnp.zeros(zrow.shape, zrow.dtype)

        def expert(e, carry):
            cnt = cnt_ref[e]
            base = pst_ref[e]
            end = (cnt + rb - 1) // rb * rb

            def zstart(r, c2):
                pltpu.make_async_copy(zrow.at[pl.ds(0, 1)], xs_hbm.at[pl.ds(base + r, 1)], sem_z).start()
                return c2

            def zwait(r, c2):
                pltpu.make_async_copy(zrow.at[pl.ds(0, 1)], xs_hbm.at[pl.ds(0, 1)], sem_z).wait()
                return c2
            lax.fori_loop(cnt, end, zstart, 0)
            lax.fori_loop(cnt, end, zwait, 0)
            return carry
        lax.fori_loop(0, n_experts, expert, 0)


def _dispatch_call(counts, pad_start, dest_flat, h_packed, np_rows, tile):
    t_rows, dh = h_packed.shape
    ntiles = t_rows // tile
    n_experts = counts.shape[0]
    grid_spec = pltpu.PrefetchScalarGridSpec(
        num_scalar_prefetch=2,
        grid=(ntiles,),
        in_specs=[pl.BlockSpec(memory_space=pl.ANY),
                  pl.BlockSpec((tile, dh), lambda i, c, p: (i, 0))],
        out_specs=pl.BlockSpec(memory_space=pl.ANY),
        scratch_shapes=[pltpu.SMEM((2 * tile * TOP_K,), I32),
                        pltpu.VMEM((2, tile, dh), jnp.uint32),
                        pltpu.VMEM((SUBLANES, dh), jnp.uint32),
                        pltpu.SemaphoreType.DMA((2,)),
                        pltpu.SemaphoreType.DMA((2,)),
                        pltpu.SemaphoreType.DMA])
    return pl.pallas_call(
        functools.partial(_dispatch_kernel, ntiles=ntiles, n_experts=n_experts),
        grid_spec=grid_spec,
        out_shape=jax.ShapeDtypeStruct((np_rows, dh), jnp.uint32),
        compiler_params=_cparams(1),
        name="moe_dispatch",
    )(counts, pad_start, dest_flat, h_packed)


def _experts_kernel(ie_ref, rsb_ref, nrb_ref, xs_hbm, wi_ref, wo_ref, y_hbm,
                    xt, acc, act, ybuf, wibf, wobf, sem_x, sem_y, ycnt_ref, *, n_items, n_k, n_b):
    i = pl.program_id(0)
    st = pl.program_id(1)
    nrb = nrb_ref[i]
    rb = EXP_ROWBLK
    tkw = xt.shape[2]
    tdw = ybuf.shape[2]
    f = act.shape[1]

    def x_copy(item, kt, bi):
        src = pl.multiple_of((rsb_ref[item] + bi) * rb, rb)
        return pltpu.make_async_copy(
            xs_hbm.at[pl.ds(src, rb), pl.ds(pl.multiple_of(kt * tkw, tkw), tkw)],
            xt.at[kt % 2, pl.ds(pl.multiple_of(bi * rb, rb), rb)], sem_x.at[kt % 2])

    def start_x(item, kt):
        def blk(bi, carry):
            x_copy(item, kt, bi).start()
            return carry
        lax.fori_loop(0, nrb_ref[item], blk, 0)

    def wait_x(item, kt):
        def blk(bi, carry):
            x_copy(item, kt, bi).wait()
            return carry
        lax.fori_loop(0, nrb_ref[item], blk, 0)

    def wait_out(s):
        def blk(bi, carry):
            pltpu.make_async_copy(ybuf.at[s, pl.ds(0, rb)], y_hbm.at[pl.ds(0, rb), pl.ds(0, tdw)],
                                  sem_y.at[s]).wait()
            return carry
        lax.fori_loop(0, ycnt_ref[s], blk, 0)
        ycnt_ref[s] = 0

    @pl.when((i == 0) & (st == 0))
    def _():
        ycnt_ref[0] = 0
        ycnt_ref[1] = 0
        xt[...] = jnp.zeros(xt.shape, xt.dtype)
        start_x(0, 0)

    @pl.when(st < n_k)
    def _():
        wait_x(i, st)

    @pl.when(st + 1 < n_k)
    def _():
        start_x(i, st + 1)

    @pl.when((st == n_k) & (i + 1 < n_items))
    def _():
        start_x(jnp.minimum(i + 1, n_items - 1), 0)

    n_big = nrb // EXP_CHUNK_BLKS
    n_small = nrb - n_big * EXP_CHUNK_BLKS

    def for_chunks(fn):
        def big(ci, carry):
            fn(ci * EXP_CHUNK_BLKS, EXP_CHUNK_BLKS)
            return carry

        def small(si, carry):
            fn(n_big * EXP_CHUNK_BLKS + si, 1)
            return carry
        lax.fori_loop(0, n_big, big, 0)
        lax.fori_loop(0, n_small, small, 0)

    def rows_of(b0, nblk):
        return pl.ds(pl.multiple_of(b0 * rb, rb), nblk * rb)

    @pl.when((st < n_k) & (nrb > 0))
    def _():
        wibf[...] = wi_ref[0, 0].astype(BF16)
        slot = st % 2

        def partial_sum(rows):
            return _dot(_unpack_pairs(xt[slot, rows, :], BF16), wibf[...])

        if n_k > 1:
            @pl.when(st == 0)
            def _():
                def first(b0, nblk):
                    rows = rows_of(b0, nblk)
                    acc[rows, :] = partial_sum(rows)
                for_chunks(first)

            @pl.when((st > 0) & (st < n_k - 1))
            def _():
                def middle(b0, nblk):
                    rows = rows_of(b0, nblk)
                    acc[rows, :] = acc[rows, :] + partial_sum(rows)
                for_chunks(middle)

        @pl.when(st == n_k - 1)
        def _():
            def last(b0, nblk):
                rows = rows_of(b0, nblk)
                a = partial_sum(rows)
                if n_k > 1:
                    a = a + acc[rows, :]
                act[rows, :] = (_silu(a[:, :f]) * a[:, f:]).astype(BF16)
            for_chunks(last)

    @pl.when((st >= n_k) & (nrb > 0))
    def _():
        dj = st - n_k
        ys = (i * n_b + dj) % 2
        wobf[...] = wo_ref[0, 0].astype(BF16)
        wait_out(ys)
        row0 = rsb_ref[i] * rb
        col0 = pl.multiple_of(dj * tdw, tdw)

        def chunk(b0, nblk):
            rows = rows_of(b0, nblk)
            ybuf[ys, rows, :] = _pack_pairs(_dot(act[rows, :], wobf[...]))
            for b in range(nblk):
                r0 = pl.multiple_of((b0 + b) * rb, rb)
                pltpu.make_async_copy(
                    ybuf.at[ys, pl.ds(r0, rb)],
                    y_hbm.at[pl.ds(pl.multiple_of(row0 + r0, rb), rb), pl.ds(col0, tdw)],
                    sem_y.at[ys]).start()
        for_chunks(chunk)
        ycnt_ref[ys] = nrb

    @pl.when((i == n_items - 1) & (st == n_k + n_b - 1))
    def _():
        wait_out(0)
        wait_out(1)


def _experts_call(layer, ie, rsb, nrb, n_items, xs, w_in, w_out):
    _, _, d, f2 = w_in.shape
    f = f2 // 2
    np_rows, dh = xs.shape
    tk = min(EXP_TK, d)
    td = min(EXP_TD, d)
    n_k, n_b = d // tk, d // td
    rmax = EXP_NBLK * EXP_ROWBLK

    def kt(st, nr):
        return jnp.where((nr > 0) & (st < n_k), st, n_k - 1)

    def dj(st, nr):
        return jnp.where(nr > 0, jnp.maximum(st - n_k, 0), n_b - 1)

    grid_spec = pltpu.PrefetchScalarGridSpec(
        num_scalar_prefetch=3,
        grid=(n_items, n_k + n_b),
        in_specs=[pl.BlockSpec(memory_space=pl.ANY),
                  pl.BlockSpec((1, 1, tk, f2), lambda i, st, ie, rsb, nrb: (layer, ie[i], kt(st, nrb[i]), 0)),
                  pl.BlockSpec((1, 1, f, td), lambda i, st, ie, rsb, nrb: (layer, ie[i], 0, dj(st, nrb[i])))],
        out_specs=pl.BlockSpec(memory_space=pl.ANY),
        scratch_shapes=[pltpu.VMEM((2, rmax, tk // 2), jnp.uint32),
                        pltpu.VMEM((rmax, f2), F32),
                        pltpu.VMEM((rmax, f), BF16),
                        pltpu.VMEM((2, rmax, td // 2), jnp.uint32),
                        pltpu.VMEM((tk, f2), BF16),
                        pltpu.VMEM((f, td), BF16),
                        pltpu.SemaphoreType.DMA((2,)),
                        pltpu.SemaphoreType.DMA((2,)),
                        pltpu.SMEM((2,), I32)])
    return pl.pallas_call(
        functools.partial(_experts_kernel, n_items=n_items, n_k=n_k, n_b=n_b),
        grid_spec=grid_spec,
        out_shape=jax.ShapeDtypeStruct((np_rows, dh), jnp.uint32),
        compiler_params=_cparams(2),
        name="moe_experts",
    )(ie, rsb, nrb, xs, w_in, w_out)


def _combine_kernel(pos_hbm, w_ref, sh_ref, x_ref, g_ref, fg_ref, y_hbm, *rest,
                    lrep, blk0, ntiles, final):
    o_ref, idx_ref, gbuf, sem_i, sem_g = rest[-5:]
    i = pl.program_id(0)
    slot = i % 2
    nslot = 1 - slot
    tile = x_ref.shape[0]
    nidx = tile * TOP_K

    def idx_copy(t, s):
        return pltpu.make_async_copy(pos_hbm.at[pl.ds(pl.multiple_of((blk0 + t) * nidx, nidx), nidx)],
                                     idx_ref.at[pl.ds(pl.multiple_of(s * nidx, nidx), nidx)], sem_i.at[s])

    def issue_gather(s):
        def tok(t, carry):
            base = s * nidx + t * TOP_K
            for j in range(TOP_K):
                p = idx_ref[base + j]
                pltpu.make_async_copy(y_hbm.at[pl.ds(p, 1)], gbuf.at[s, j, pl.ds(t, 1)], sem_g.at[s]).start()
            return carry
        lax.fori_loop(0, tile, tok, 0)

    @pl.when(i == 0)
    def _():
        first = idx_copy(0, 0)
        first.start()
        first.wait()
        issue_gather(0)
        if ntiles > 1:
            idx_copy(1, 1).start()

    @pl.when(i + 1 < ntiles)
    def _():
        idx_copy(jnp.minimum(i + 1, ntiles - 1), nslot).wait()

    for j in range(TOP_K):
        pltpu.make_async_copy(y_hbm.at[pl.ds(0, tile)], gbuf.at[slot, j], sem_g.at[slot]).wait()

    @pl.when(i + 1 < ntiles)
    def _():
        issue_gather(nslot)

    @pl.when(i + 2 < ntiles)
    def _():
        idx_copy(jnp.minimum(i + 2, ntiles - 1), slot).start()

    acc = sh_ref[...]
    for j in range(TOP_K):
        acc = acc + w_ref[:, j:j + 1] * _unpack_pairs(gbuf[slot, j], F32)
    o = x_ref[...] + _expand_rows(g_ref[...], lrep, tile, i * tile) * acc
    if final:
        ms = jnp.mean(o * o, axis=-1, keepdims=True)
        o = o * lax.rsqrt(ms + EPS) * fg_ref[...]
    o_ref[...] = o


def _combine_call(reg, t_rows, pos_flat, wts, shared, x_all, g, final_g, y_sorted, final, prev):
    d = x_all.shape[1]
    row = lambda i: (reg.blk0 + i, 0)
    fixed = lambda i: (0, 0)
    in_specs = [pl.BlockSpec(memory_space=pl.ANY),
                pl.BlockSpec((reg.tile, TOP_K), row),
                pl.BlockSpec((reg.tile, d), row),
                pl.BlockSpec((reg.tile, d), row),
                pl.BlockSpec((reg.nseq, d), fixed),
                pl.BlockSpec((1, d), fixed),
                pl.BlockSpec(memory_space=pl.ANY)]
    args = [pos_flat, wts, shared, x_all, g, final_g, y_sorted]
    aliases = {}
    if prev is not None:
        aliases[len(args)] = 0
        in_specs.append(pl.BlockSpec(memory_space=pl.ANY))
        args.append(prev)
    if final:
        out_spec = pl.BlockSpec((reg.tile, d), lambda i: (i, 0))
        out_rows = reg.rows
    else:
        out_spec = pl.BlockSpec((reg.tile, d), row)
        out_rows = t_rows
    return pl.pallas_call(
        functools.partial(_combine_kernel, lrep=reg.lrep, blk0=reg.blk0, ntiles=reg.ntiles, final=final),
        grid=(reg.ntiles,),
        in_specs=in_specs,
        out_specs=out_spec,
        out_shape=jax.ShapeDtypeStruct((out_rows, d), F32),
        scratch_shapes=[pltpu.SMEM((2 * reg.tile * TOP_K,), I32),
                        pltpu.VMEM((2, TOP_K, reg.tile, d // 2), jnp.uint32),
                        pltpu.SemaphoreType.DMA((2,)),
                        pltpu.SemaphoreType.DMA((2,))],
        input_output_aliases=aliases,
        compiler_params=_cparams(1),
        name="moe_combine",
    )(*args)


def _both(fn, regs):
    out = fn(regs[0], None)
    return fn(regs[1], out)


def kernel(x_prompt, x_sample, state_s5_re, state_s5_im, state_gla, c_prompt, c_sample, w_ada, b_ada, norm_g, s5_lam_re, s5_lam_im, s5_log_dt, s5_b_re, s5_b_im, s5_c_re, s5_c_im, s5_d, s5_w_glu, s5_b_glu, gla_w_in, gla_w_a1, gla_w_a2, gla_b_a, gla_g_norm, gla_w_o, moe_w_router, moe_b_router, moe_w_in, moe_w_out, moe_ws_in, moe_ws_out, final_g):
    bp, lp, d = x_prompt.shape
    bs, ls, _ = x_sample.shape
    assert bp == 1 and bs == SUBLANES and ls % SUBLANES == 0
    depth = w_ada.shape[0]
    n_exp = moe_w_in.shape[1]
    f_sh = moe_ws_out.shape[1]
    srows = bs * ls
    t_rows = lp + srows
    s5_steps = 64
    s5_blk = s5_steps * SUBLANES
    gla_chunk = 64
    assert lp % PROMPT_TILE == 0 and lp % s5_blk == 0 and lp % srows == 0

    mm_tile = MM_TILE if lp % MM_TILE == 0 else PROMPT_TILE
    regs_mm = (_Region(0, lp, mm_tile, 1, mm_tile), _Region(lp, srows, srows, bs, ls))
    regs_nm = (_Region(0, lp, PROMPT_TILE // 2, 1, PROMPT_TILE // 2), _Region(lp, srows, srows, bs, ls))
    regs_cb = (_Region(0, lp, COMB_TILE, 1, COMB_TILE), _Region(lp, srows, COMB_TILE, bs, ls))

    x_parts = (x_prompt.reshape(lp, d), x_sample.reshape(srows, d))
    x_all = None

    def x_of(r):
        return x_parts[r.nseq > 1] if x_parts is not None else x_all

    c_all = jnp.concatenate([c_prompt, c_sample, jnp.zeros((2 * SUBLANES - 1 - bs, d), F32)], axis=0)
    mod = _ada_call(c_all, w_ada, b_ada)

    def mods(layer, k):
        m = mod[layer, :, k * d:(k + 1) * d]
        return (m[0:1], m[1:1 + bs])

    new_re_p, new_im_p, new_gla_p, new_re_s, new_im_s, new_gla_s = [], [], [], [], [], []
    for i in range(depth):
        sh1, sc1, g1, sh2, sc2, g2 = [mods(i, k) for k in range(6)]
        j = i // 2
        ng1 = norm_g[i, 0].reshape(1, d)
        ng2 = norm_g[i, 1].reshape(1, d)
        if i % 2 == 0:
            own = x_parts is not None
            (h_all,) = _both(lambda r, prev: _norm_call(
                "plain", r, t_rows, x_of(r), ng1, sc1[r.nseq > 1], sh1[r.nseq > 1], [], [(d, F32)], prev,
                x_own=own), regs_nm)
            grp, pst = s5_lam_re.shape[1:]
            ncol = grp * pst
            lam, bbr, bbi = _s5_disc_call(s5_lam_re[j], s5_lam_im[j], s5_log_dt[j], s5_b_re[j], s5_b_im[j], s5_steps)
            bmat, cmat = _s5_block_weights(bbr, bbi, s5_c_re[j], s5_c_im[j])
            kt = bmat.shape[0]
            lamt = jnp.transpose(lam.reshape(4, kt, ncol // kt), (1, 0, 2))
            lamt = jnp.concatenate([lamt, jnp.zeros_like(lamt)], axis=1)
            dsk = s5_d[j].reshape(1, d)
            zeros_st = jnp.zeros((SUBLANES, ncol), F32)
            z_all, pre, pim = _s5_call(0, lp, t_rows, h_all, bmat, cmat, lamt, dsk, zeros_st, zeros_st,
                                       s5_steps, True, None)
            z_all, sre, sim = _s5_call(lp, srows, t_rows, h_all, bmat, cmat, lamt, dsk,
                                       state_s5_re[j].reshape(bs, ncol), state_s5_im[j].reshape(bs, ncol),
                                       ls, False, z_all)
            new_re_p.append(pre[0].reshape(1, grp, pst))
            new_im_p.append(pim[0].reshape(1, grp, pst))
            new_re_s.append(sre.reshape(bs, grp, pst))
            new_im_s.append(sim.reshape(bs, grp, pst))
            bglu = s5_b_glu[j].reshape(1, d)
            x_all = _both(lambda r, prev: _mm_call(
                "glu", r, t_rows, z_all, [s5_w_glu[j]], [0], d, 512, F32,
                [(bglu, "col"), (z_all, "tile"), (x_of(r), "own" if own else "tile"), (g1[r.nseq > 1], "seq")],
                prev, "s5_glu"), regs_mm)
            x_parts = None
        else:
            hk = gla_w_a2.shape[2]
            assert 2 * hk == d
            own = x_parts is not None
            h_bf, lg = _both(lambda r, prev: _norm_call(
                "gate", r, t_rows, x_of(r), ng1, sc1[r.nseq > 1], sh1[r.nseq > 1],
                [gla_w_a1[j], gla_w_a2[j], gla_b_a[j].reshape(1, hk)], [(d, BF16), (hk, F32)], prev,
                x_own=own), regs_nm)
            proj = _both(lambda r, prev: _mm_call(
                "plain", r, t_rows, h_bf, [gla_w_in[j]], [0], 3 * d, 512, F32, [], prev, "gla_proj"), regs_mm)
            gn = gla_g_norm[j].reshape(1, -1)
            s0p = jnp.zeros((1,) + state_gla.shape[2:], F32)
            o_all, gla_p = _gla_call(0, 1, lp, gla_chunk, t_rows, proj, lg, s0p, gn, None)
            o_all, gla_s = _gla_call(lp, bs, ls, ls, t_rows, proj, lg, state_gla[j], gn, o_all)
            new_gla_p.append(gla_p)
            new_gla_s.append(gla_s)
            x_all = _both(lambda r, prev: _mm_call(
                "res", r, t_rows, o_all, [gla_w_o[j]], [0], d, 512, F32,
                [(x_of(r), "own" if own else "tile"), (g1[r.nseq > 1], "seq")], prev, "gla_out"), regs_mm)
            x_parts = None

        brt = moe_b_router[i].reshape(1, n_exp)
        r_p = _router_call(regs_nm[0], t_rows, x_all, ng2, sc2[0], sh2[0], moe_w_router[i], brt,
                           jnp.zeros((SUBLANES, n_exp), F32), None)
        h_pk, h_bf, idx, wts, rank, cnt = _router_call(
            regs_nm[1], t_rows, x_all, ng2, sc2[1], sh2[1], moe_w_router[i], brt, r_p[5], r_p[:5])
        counts = cnt[0].astype(I32)
        ie, rsb, nrb, pad_start, dest, np_rows, n_items = _dispatch_plan(idx, rank, counts, t_rows)
        pos_flat = dest.reshape(-1)
        xs = _dispatch_call(counts, pad_start, pos_flat, h_pk, np_rows, DISP_TILE)
        y_sorted = _experts_call(i, ie, rsb, nrb, n_items, xs, moe_w_in, moe_w_out)
        act = _both(lambda r, prev: _mm_call(
            "swiglu", r, t_rows, h_bf, [moe_ws_in[i], moe_ws_in[i]], [0, f_sh // 256], f_sh, 256, BF16,
            [], prev, "shared_in"), regs_mm)
        shared = _both(lambda r, prev: _mm_call(
            "plain", r, t_rows, act, [moe_ws_out[i]], [0], d, 512, F32, [], prev, "shared_out"), regs_mm)
        last = i == depth - 1
        fg = final_g.reshape(1, d)
        if last:
            y_prompt, y_sample = [_combine_call(r, t_rows, pos_flat, wts, shared, x_all, g2[r.nseq > 1], fg,
                                                y_sorted, True, None) for r in regs_cb]
        else:
            x_all = _both(lambda r, prev: _combine_call(
                r, t_rows, pos_flat, wts, shared, x_all, g2[r.nseq > 1], fg, y_sorted, False, prev), regs_cb)

    y_prompt = y_prompt.reshape(bp, lp, d)
    y_sample = y_sample.reshape(bs, ls, d)
    return (y_prompt, y_sample, jnp.stack(new_re_p), jnp.stack(new_im_p), jnp.stack(new_gla_p),
            jnp.stack(new_re_s), jnp.stack(new_im_s), jnp.stack(new_gla_s))
```

```python
import functools

import numpy as np
import jax
import jax.numpy as jnp
from jax import lax
from jax.experimental import pallas as pl
from jax.experimental.pallas import tpu as pltpu

F32 = jnp.float32
BF16 = jnp.bfloat16
I32 = jnp.int32

EPS = 1e-6
GLA_TAU = 16.0
TOP_K = 8
N_GROUPS = 8
TOPK_GROUPS = 4
ROUTED_SCALE = 2.5

LANES = 128
SUBLANES = 8
MXU_DIM = 256
VMEM_LIMIT = 56 << 20

PROMPT_TILE = 512
MM_TILE = 1024
S5_KTILE = 256
S5_COLS = 512
EXP_ROWBLK = 128
EXP_NBLK = 18
EXP_CHUNKS = (4, 2, 1)
EXP_TK = 1024
EXP_TD = 1024
PACK_GROUP = 512
DISP_TILE = 128
COMB_TILE = 128


def _cparams(n_axes, vmem=VMEM_LIMIT):
    return pltpu.CompilerParams(dimension_semantics=("arbitrary",) * n_axes, vmem_limit_bytes=vmem)


class _Region:
    def __init__(self, row0, rows, tile, nseq, lrep):
        self.row0, self.rows, self.tile, self.nseq, self.lrep = row0, rows, tile, nseq, lrep
        self.blk0 = row0 // tile
        self.ntiles = rows // tile
        assert row0 % tile == 0 and rows % tile == 0


def _expand_rows(m, lrep, rows=None, row0=0):
    nseq, n = m.shape
    if nseq == 1:
        return m
    rows = nseq * lrep if rows is None else rows
    r = lax.broadcasted_iota(I32, (rows, n), 0) + row0
    out = jnp.broadcast_to(m[0:1, :], (rows, n))
    for b in range(1, nseq):
        out = jnp.where(r >= b * lrep, jnp.broadcast_to(m[b:b + 1, :], (rows, n)), out)
    return out


def _sigmoid(x):
    return 1.0 / (1.0 + jnp.exp(-x))


def _silu(x):
    return x * _sigmoid(x)


def _gelu_tanh(x):
    return 0.5 * x * (1.0 + jnp.tanh(0.7978845608028654 * (x + 0.044715 * x * x * x)))


def _log_sigmoid(x):
    return jnp.minimum(x, 0.0) - jnp.log1p(jnp.exp(-jnp.abs(x)))


def _split3(x):
    p1 = x.astype(BF16)
    r1 = x - p1.astype(F32)
    p2 = r1.astype(BF16)
    r2 = r1 - p2.astype(F32)
    return p1, p2, r2.astype(BF16)


def _pack_pairs(x):
    half = PACK_GROUP // 2
    words = []
    for g in range(x.shape[1] // PACK_GROUP):
        lo = pltpu.bitcast(x[:, g * PACK_GROUP:g * PACK_GROUP + half].astype(BF16).astype(F32), jnp.uint32)
        hi = pltpu.bitcast(x[:, g * PACK_GROUP + half:(g + 1) * PACK_GROUP].astype(BF16).astype(F32), jnp.uint32)
        words.append(hi | (lo >> 16))
    return words[0] if len(words) == 1 else jnp.concatenate(words, axis=1)


def _unpack_pairs(w, dtype):
    half = PACK_GROUP // 2
    parts = []
    for g in range(w.shape[1] // half):
        ww = w[:, g * half:(g + 1) * half]
        parts.append(pltpu.bitcast(ww << 16, F32).astype(dtype))
        parts.append(pltpu.bitcast(ww & jnp.uint32(0xFFFF0000), F32).astype(dtype))
    return jnp.concatenate(parts, axis=1)


def _dot(a, b):
    return jnp.dot(a, b, preferred_element_type=F32)


def _dot_t0(a, b):
    return lax.dot_general(a, b, (((0,), (0,)), ((), ())), preferred_element_type=F32)


def _dot_t1(a, b):
    return lax.dot_general(a, b, (((1,), (1,)), ((), ())), preferred_element_type=F32)


def _ada_kernel(c_ref, w_ref, b_ref, o_ref):
    s = _silu(c_ref[...])
    o_ref[0] = _dot(s.astype(BF16), w_ref[0].astype(BF16)) + b_ref[0]


def _ada_call(c_all, w_ada, b_ada):
    depth, d, n6 = w_ada.shape
    nc = c_all.shape[0]
    tn = 512
    return pl.pallas_call(
        _ada_kernel,
        grid=(depth, n6 // tn),
        in_specs=[pl.BlockSpec((nc, d), lambda l, j: (0, 0)),
                  pl.BlockSpec((1, d, tn), lambda l, j: (l, 0, j)),
                  pl.BlockSpec((1, 1, tn), lambda l, j: (l, 0, j))],
        out_specs=pl.BlockSpec((1, nc, tn), lambda l, j: (l, 0, j)),
        out_shape=jax.ShapeDtypeStruct((depth, nc, n6), F32),
        compiler_params=_cparams(2),
        name="adaln",
    )(c_all, w_ada, b_ada.reshape(depth, 1, n6))


def _norm_mod(x, g, sc, sh, lrep):
    ms = jnp.mean(x * x, axis=-1, keepdims=True)
    y = x * lax.rsqrt(ms + EPS) * g
    return y * (1.0 + _expand_rows(sc, lrep)) + _expand_rows(sh, lrep)


def _norm_kernel(x_ref, g_ref, sc_ref, sh_ref, *rest, lrep, aliased):
    o_ref = rest[-1]
    h = _norm_mod(x_ref[...], g_ref[...], sc_ref[...], sh_ref[...], lrep)
    o_ref[...] = h.astype(o_ref.dtype)


def _cols_to_lanes(cols, dtype):
    r = cols[0].shape[0]
    lane = lax.broadcasted_iota(I32, (r, len(cols)), 1)
    out = jnp.zeros((r, len(cols)), dtype)
    for j, c in enumerate(cols):
        out = jnp.where(lane == j, c.astype(dtype), out)
    return out


def _router_kernel(x_ref, g_ref, sc_ref, sh_ref, w_ref, b_ref, cin_ref, *rest, lrep):
    hp_ref, hb_ref, idx_ref, wt_ref, rk_ref, cout_ref, run_ref = rest[-7:]
    i = pl.program_id(0)

    @pl.when(i == 0)
    def _():
        run_ref[...] = cin_ref[0:1, :]

    h = _norm_mod(x_ref[...], g_ref[...], sc_ref[...], sh_ref[...], lrep)
    hb_ref[...] = h.astype(BF16)
    hp_ref[...] = _pack_pairs(h)
    w = w_ref[...]
    hh = h.astype(BF16)
    hl = (h - hh.astype(F32)).astype(BF16)
    wh = w.astype(BF16)
    wl = (w - wh.astype(F32)).astype(BF16)
    scores = _sigmoid(_dot(hh, wh) + _dot(hl, wh) + _dot(hh, wl))

    rows, n_exp = scores.shape
    per = n_exp // N_GROUPS
    neg = -jnp.inf
    big = n_exp + 1
    lane_i = lax.broadcasted_iota(I32, (rows, n_exp), 1)
    lane = lane_i.astype(F32)
    grp_i = lane_i // per
    grp = grp_i.astype(F32)
    choice = scores + b_ref[...]
    gs = jnp.zeros((rows, n_exp), F32)
    for g in range(N_GROUPS):
        ing = grp_i == g
        m = jnp.where(ing, choice, neg)
        m1 = jnp.max(m, axis=-1, keepdims=True)
        top = m == m1
        ntop = jnp.sum(jnp.where(top, 1.0, 0.0), axis=-1, keepdims=True)
        below = jnp.max(jnp.where(top, neg, m), axis=-1, keepdims=True)
        m2 = jnp.where(ntop > 1.5, m1, below)
        gs = jnp.where(ing, m1 + m2, gs)
    masked = jnp.full((rows, n_exp), neg, F32)
    for _ in range(TOPK_GROUPS):
        mx = jnp.max(gs, axis=-1, keepdims=True)
        gi = jnp.min(jnp.where(gs == mx, grp, big), axis=-1, keepdims=True)
        sel = grp == gi
        masked = jnp.where(sel, choice, masked)
        gs = jnp.where(sel, neg, gs)
    idx_cols, w_cols, hits = [], [], []
    for _ in range(TOP_K):
        mx = jnp.max(masked, axis=-1, keepdims=True)
        ei = jnp.min(jnp.where(masked == mx, lane, big), axis=-1, keepdims=True)
        hit = lane == ei
        hits.append(hit)
        idx_cols.append(ei)
        w_cols.append(jnp.sum(jnp.where(hit, scores, 0.0), axis=-1, keepdims=True))
        masked = jnp.where(hit, neg, masked)
    wsum = w_cols[0]
    for c in w_cols[1:]:
        wsum = wsum + c
    scale = ROUTED_SCALE / wsum
    idx_ref[...] = _cols_to_lanes(idx_cols, I32)
    wt_ref[...] = _cols_to_lanes([c * scale for c in w_cols], F32)
    onehot = jnp.zeros((rows, n_exp), F32)
    for hit in hits:
        onehot = jnp.where(hit, 1.0, onehot)
    rr = lax.broadcasted_iota(I32, (rows, rows), 0)
    cc = lax.broadcasted_iota(I32, (rows, rows), 1)
    before = jnp.where(rr > cc, 1.0, 0.0).astype(BF16)
    cum = _dot(before, onehot.astype(BF16)) + run_ref[...]
    rk_ref[...] = _cols_to_lanes(
        [jnp.sum(jnp.where(hit, cum, 0.0), axis=-1, keepdims=True) for hit in hits], I32)
    run_ref[...] = run_ref[...] + jnp.sum(onehot, axis=0, keepdims=True)
    cout_ref[...] = jnp.broadcast_to(run_ref[...], cout_ref.shape)


def _router_call(reg, t_rows, x_all, g, sc, sh, w_router, b_router, cnt_in, prev):
    d = x_all.shape[1]
    n_exp = w_router.shape[1]
    row = lambda i: (reg.blk0 + i, 0)
    fixed = lambda i: (0, 0)
    in_specs = [pl.BlockSpec((reg.tile, d), row), pl.BlockSpec((1, d), fixed),
                pl.BlockSpec((reg.nseq, d), fixed), pl.BlockSpec((reg.nseq, d), fixed),
                pl.BlockSpec((d, n_exp), fixed), pl.BlockSpec((1, n_exp), fixed),
                pl.BlockSpec((SUBLANES, n_exp), fixed)]
    args = [x_all, g, sc, sh, w_router, b_router, cnt_in]
    outs = [(d // 2, jnp.uint32), (d, BF16), (TOP_K, I32), (TOP_K, F32), (TOP_K, I32)]
    aliases = {}
    if prev is not None:
        for k in range(len(outs)):
            aliases[len(args)] = k
            in_specs.append(pl.BlockSpec(memory_space=pl.ANY))
            args.append(prev[k])
    res = pl.pallas_call(
        functools.partial(_router_kernel, lrep=reg.lrep),
        grid=(reg.ntiles,),
        in_specs=in_specs,
        out_specs=[pl.BlockSpec((reg.tile, n), row) for n, _ in outs] + [pl.BlockSpec((SUBLANES, n_exp), fixed)],
        out_shape=[jax.ShapeDtypeStruct((t_rows, n), dt) for n, dt in outs]
        + [jax.ShapeDtypeStruct((SUBLANES, n_exp), F32)],
        scratch_shapes=[pltpu.VMEM((1, n_exp), F32)],
        input_output_aliases=aliases,
        compiler_params=_cparams(1),
        name="moe_router",
    )(*args)
    return list(res)


def _norm_gate_kernel(x_ref, g_ref, sc_ref, sh_ref, wa1_ref, wa2_ref, ba_ref, *rest, lrep, aliased):
    h_ref, lg_ref = rest[-2:]
    h = _norm_mod(x_ref[...], g_ref[...], sc_ref[...], sh_ref[...], lrep)
    hb = h.astype(BF16)
    h_ref[...] = hb
    a = _dot(hb, wa1_ref[...].astype(BF16))
    z = _dot(a.astype(BF16), wa2_ref[...].astype(BF16)) + ba_ref[...]
    lg_ref[...] = _log_sigmoid(z) * (1.0 / GLA_TAU)


def _norm_call(kind, reg, t_rows, x_all, g, sc, sh, extra, outs, prev, x_own=False):
    d = x_all.shape[1]
    kern = {"plain": _norm_kernel, "gate": _norm_gate_kernel}[kind]
    row = lambda i: (reg.blk0 + i, 0)
    xrow = (lambda i: (i, 0)) if x_own else row
    fixed = lambda i: (0, 0)
    in_specs = [pl.BlockSpec((reg.tile, d), xrow), pl.BlockSpec((1, d), fixed),
                pl.BlockSpec((reg.nseq, d), fixed), pl.BlockSpec((reg.nseq, d), fixed)]
    args = [x_all, g, sc, sh]
    for e in extra:
        in_specs.append(pl.BlockSpec(e.shape, fixed))
        args.append(e)
    aliases = {}
    if prev is not None:
        for k, p in enumerate(prev):
            aliases[len(args)] = k
            in_specs.append(pl.BlockSpec(memory_space=pl.ANY))
            args.append(p)
    res = pl.pallas_call(
        functools.partial(kern, lrep=reg.lrep, aliased=prev is not None),
        grid=(reg.ntiles,),
        in_specs=in_specs,
        out_specs=[pl.BlockSpec((reg.tile, n), row) for n, _ in outs],
        out_shape=[jax.ShapeDtypeStruct((t_rows, n), dt) for n, dt in outs],
        input_output_aliases=aliases,
        compiler_params=_cparams(1),
        name="norm_" + kind,
    )(*args)
    return list(res)


def _s5_disc_kernel(lr_ref, li_ref, ldt_ref, br_ref, bi_ref, lam_ref, bbr_ref, bbi_ref, *, nsteps):
    lr = lr_ref[...]
    li = li_ref[...]
    dt = jnp.exp(ldt_ref[...])
    mag = jnp.exp(lr * dt)
    ar = mag * jnp.cos(li * dt)
    ai = mag * jnp.sin(li * dt)
    den = lr * lr + li * li
    fr = ((ar - 1.0) * lr + ai * li) / den
    fi = (ai * lr - (ar - 1.0) * li) / den
    lam_ref[0] = ar
    lam_ref[1] = ai
    mags = jnp.exp(nsteps * (lr * dt))
    lam_ref[2] = mags * jnp.cos(nsteps * (li * dt))
    lam_ref[3] = mags * jnp.sin(nsteps * (li * dt))
    for h in range(br_ref.shape[0]):
        br = br_ref[h]
        bi = bi_ref[h]
        bbr_ref[h] = fr * br - fi * bi
        bbi_ref[h] = fr * bi + fi * br


def _s5_disc_call(lam_re, lam_im, log_dt, b_re, b_im, nsteps):
    g, p = lam_re.shape
    hg = b_re.shape[2]
    ldt = jnp.broadcast_to(log_dt[:, None], (g, p))
    brt = jnp.transpose(b_re, (2, 0, 1))
    bit = jnp.transpose(b_im, (2, 0, 1))
    return pl.pallas_call(
        functools.partial(_s5_disc_kernel, nsteps=float(nsteps)),
        out_shape=[jax.ShapeDtypeStruct((4, g, p), F32),
                   jax.ShapeDtypeStruct((hg, g, p), F32),
                   jax.ShapeDtypeStruct((hg, g, p), F32)],
        name="s5_disc",
    )(lam_re, lam_im, ldt, brt, bit)


def _s5_block_weights(bbr, bbi, c_re, c_im):
    hg, g, p = bbr.shape
    gt = S5_KTILE // hg
    kt = g // gt
    eye = jnp.eye(gt, dtype=F32)

    def bd_in(b):
        b = b.reshape(hg, kt, gt, p)
        return jnp.einsum("hkgp,gq->kghqp", b, eye).reshape(kt, gt * hg, gt * p)

    def bd_out(c):
        c = c.reshape(kt, gt, hg, p)
        return jnp.einsum("kghp,gq->kqpgh", c, eye).reshape(kt, gt * p, gt * hg)

    bmat = jnp.concatenate([bd_in(bbr), bd_in(bbi)], axis=2).astype(BF16)
    cmat = jnp.concatenate([bd_out(c_re), -bd_out(c_im)], axis=1).astype(BF16)
    return bmat, cmat


def _s5_perm(nsteps):
    r = np.arange(nsteps * SUBLANES)
    p = np.zeros((r.size, r.size), np.float32)
    p[(r % nsteps) * SUBLANES + r // nsteps, r] = 1.0
    return jnp.asarray(p, BF16), jnp.asarray(p.T, BF16)


def _s5_kernel(h_ref, p_ref, pt_ref, b_ref, c_ref, lam_ref, d_ref, sre_ref, sim_ref, *rest, nsteps, chain):
    z_ref, ore_ref, oim_ref, bu_ref, st_ref = rest[-5:]
    rb = pl.program_id(1)
    nc = sre_ref.shape[1]

    @pl.when(rb == 0)
    def _():
        st_ref[0] = sre_ref[...]
        st_ref[1] = sim_ref[...]

    perm = p_ref[...]
    h1, h2, h3 = _split3(h_ref[...])
    u = _dot(perm, h1) + _dot(perm, h2) + _dot(perm, h3)
    bu_ref[...] = _dot(u.astype(BF16), b_ref[0])
    rowid = lax.broadcasted_iota(I32, (SUBLANES, S5_COLS), 0)

    for cb in range(nc // S5_COLS):
        c_re = slice(cb * S5_COLS, (cb + 1) * S5_COLS)
        c_im = slice(nc + cb * S5_COLS, nc + (cb + 1) * S5_COLS)
        ar = jnp.broadcast_to(lam_ref[0, 0:1, c_re], (SUBLANES, S5_COLS))
        ai = jnp.broadcast_to(lam_ref[0, 1:2, c_re], (SUBLANES, S5_COLS))

        def step(s, carry, store):
            xr, xi = carry
            r0 = pl.multiple_of(s * SUBLANES, SUBLANES)
            br = bu_ref[pl.ds(r0, SUBLANES), c_re]
            bi = bu_ref[pl.ds(r0, SUBLANES), c_im]
            nxr = ar * xr - ai * xi + br
            nxi = ar * xi + ai * xr + bi
            if store:
                bu_ref[pl.ds(r0, SUBLANES), c_re] = nxr
                bu_ref[pl.ds(r0, SUBLANES), c_im] = nxi
            return nxr, nxi

        if chain:
            zero = jnp.zeros((SUBLANES, S5_COLS), F32)
            er, ei = lax.fori_loop(0, nsteps, functools.partial(step, store=False), (zero, zero))
            asr = lam_ref[0, 2:3, c_re]
            asi = lam_ref[0, 3:4, c_re]
            pr = st_ref[0, 0:1, c_re]
            pi = st_ref[1, 0:1, c_re]
            x0r, x0i = zero, zero
            for j in range(SUBLANES):
                x0r = jnp.where(rowid == j, jnp.broadcast_to(pr, (SUBLANES, S5_COLS)), x0r)
                x0i = jnp.where(rowid == j, jnp.broadcast_to(pi, (SUBLANES, S5_COLS)), x0i)
                nr = asr * pr - asi * pi + er[j:j + 1]
                ni = asr * pi + asi * pr + ei[j:j + 1]
                pr, pi = nr, ni
            st_ref[0, :, c_re] = jnp.broadcast_to(pr, (SUBLANES, S5_COLS))
            st_ref[1, :, c_re] = jnp.broadcast_to(pi, (SUBLANES, S5_COLS))
            lax.fori_loop(0, nsteps, functools.partial(step, store=True), (x0r, x0i))
        else:
            fr, fi = lax.fori_loop(0, nsteps, functools.partial(step, store=True),
                                   (st_ref[0, :, c_re], st_ref[1, :, c_re]))
            st_ref[0, :, c_re] = fr
            st_ref[1, :, c_re] = fi

    y = _dot(bu_ref[...].astype(BF16), c_ref[0]) + d_ref[...] * u
    z = _gelu_tanh(y).astype(BF16)
    z_ref[...] = _dot(pt_ref[...], z).astype(z_ref.dtype)

    @pl.when(rb == pl.num_programs(1) - 1)
    def _():
        ore_ref[...] = st_ref[0]
        oim_ref[...] = st_ref[1]


def _s5_call(row0, rows, t_rows, h_all, bmat, cmat, lam, d_skip, s_re, s_im, nsteps, chain, prev):
    d = h_all.shape[1]
    kt = bmat.shape[0]
    nc = bmat.shape[2] // 2
    rblk = nsteps * SUBLANES
    blk0 = row0 // rblk
    perm, perm_t = _s5_perm(nsteps)
    fixed = lambda k, r: (0, 0)
    in_specs = [pl.BlockSpec((rblk, S5_KTILE), lambda k, r: (blk0 + r, k)),
                pl.BlockSpec((rblk, rblk), fixed),
                pl.BlockSpec((rblk, rblk), fixed),
                pl.BlockSpec((1, S5_KTILE, 2 * nc), lambda k, r: (k, 0, 0)),
                pl.BlockSpec((1, 2 * nc, S5_KTILE), lambda k, r: (k, 0, 0)),
                pl.BlockSpec((1, SUBLANES, nc), lambda k, r: (k, 0, 0)),
                pl.BlockSpec((1, S5_KTILE), lambda k, r: (0, k)),
                pl.BlockSpec((SUBLANES, nc), lambda k, r: (0, k)),
                pl.BlockSpec((SUBLANES, nc), lambda k, r: (0, k))]
    args = [h_all, perm, perm_t, bmat, cmat, lam, d_skip, s_re, s_im]
    aliases = {}
    if prev is not None:
        aliases[len(args)] = 0
        in_specs.append(pl.BlockSpec(memory_space=pl.ANY))
        args.append(prev)
    return pl.pallas_call(
        functools.partial(_s5_kernel, nsteps=nsteps, chain=chain),
        grid=(kt, rows // rblk),
        in_specs=in_specs,
        out_specs=[pl.BlockSpec((rblk, S5_KTILE), lambda k, r: (blk0 + r, k)),
                   pl.BlockSpec((SUBLANES, nc), lambda k, r: (0, k)),
                   pl.BlockSpec((SUBLANES, nc), lambda k, r: (0, k))],
        out_shape=[jax.ShapeDtypeStruct((t_rows, d), BF16),
                   jax.ShapeDtypeStruct(s_re.shape, F32),
                   jax.ShapeDtypeStruct(s_im.shape, F32)],
        scratch_shapes=[pltpu.VMEM((rblk, 2 * nc), F32), pltpu.VMEM((2, SUBLANES, nc), F32)],
        input_output_aliases=aliases,
        compiler_params=_cparams(2),
        name="s5_scan",
    )(*args)


def _mm_kernel(*refs, mode, lrep, n_w, n_alias):
    lhs_ref = refs[0]
    w_refs = refs[1:1 + n_w]
    pos = 1 + n_w
    i = pl.program_id(1)
    wbf = refs[len(refs) - n_w:]
    o_ref = refs[len(refs) - n_w - 1]

    @pl.when(i == 0)
    def _():
        for w_ref, s_ref in zip(w_refs, wbf):
            s_ref[...] = w_ref[...].astype(BF16)

    lhs = lhs_ref[...]
    if mode == "glu":
        b_ref, zt_ref, x_ref, g_ref = refs[pos:pos + 4]
        t = _dot(lhs, wbf[0][...]) + b_ref[...]
        o = zt_ref[...].astype(F32) * _sigmoid(t)
        o_ref[...] = x_ref[...] + _expand_rows(g_ref[...], lrep) * o
    elif mode == "res":
        x_ref, g_ref = refs[pos:pos + 2]
        o_ref[...] = x_ref[...] + _expand_rows(g_ref[...], lrep) * _dot(lhs, wbf[0][...])
    elif mode == "plain":
        o_ref[...] = _dot(lhs, wbf[0][...]).astype(o_ref.dtype)
    elif mode == "swiglu":
        o_ref[...] = (_silu(_dot(lhs, wbf[0][...])) * _dot(lhs, wbf[1][...])).astype(o_ref.dtype)


def _mm_call(mode, reg, t_rows, lhs, w_list, w_colblk0, n_out, tn, out_dtype, extras, prev, name):
    k_dim = lhs.shape[1]
    nj = n_out // tn
    in_specs = [pl.BlockSpec((reg.tile, k_dim), lambda j, i: (reg.blk0 + i, 0))]
    args = [lhs]
    for w, c0 in zip(w_list, w_colblk0):
        in_specs.append(pl.BlockSpec((k_dim, tn), lambda j, i, c0=c0: (0, c0 + j)))
        args.append(w)
    for a, kind in extras:
        if kind == "col":
            in_specs.append(pl.BlockSpec((1, tn), lambda j, i: (0, j)))
        elif kind == "tile":
            in_specs.append(pl.BlockSpec((reg.tile, tn), lambda j, i: (reg.blk0 + i, j)))
        elif kind == "own":
            in_specs.append(pl.BlockSpec((reg.tile, tn), lambda j, i: (i, j)))
        else:
            in_specs.append(pl.BlockSpec((reg.nseq, tn), lambda j, i: (0, j)))
        args.append(a)
    aliases = {}
    if prev is not None:
        aliases[len(args)] = 0
        in_specs.append(pl.BlockSpec(memory_space=pl.ANY))
        args.append(prev)
    return pl.pallas_call(
        functools.partial(_mm_kernel, mode=mode, lrep=reg.lrep, n_w=len(w_list), n_alias=len(aliases)),
        grid=(nj, reg.ntiles),
        in_specs=in_specs,
        out_specs=pl.BlockSpec((reg.tile, tn), lambda j, i: (reg.blk0 + i, j)),
        out_shape=jax.ShapeDtypeStruct((t_rows, n_out), out_dtype),
        scratch_shapes=[pltpu.VMEM((k_dim, tn), BF16) for _ in w_list],
        input_output_aliases=aliases,
        compiler_params=_cparams(2),
        name=name,
    )(*args)


def _gla_consts(chunk):
    nlev = int(np.log2(chunk))
    assert 1 << nlev == chunk
    tri = np.tril(np.ones((chunk, chunk), np.float32))
    r = np.arange(chunk)
    wall, masks = [tri], []
    for l in range(nlev):
        w = chunk >> (l + 1)
        blk = r // (2 * w)
        second = (r & w) != 0
        wall.append(tri[blk * 2 * w + w - 1])
        masks.append(((blk[:, None] == blk[None, :]) & second[:, None] & (~second)[None, :]).astype(np.float32))
    masks.append(np.eye(chunk, dtype=np.float32))
    return jnp.asarray(np.concatenate(wall, 0), BF16), jnp.asarray(np.stack(masks, 0), F32)


def _gla_kernel(q_ref, k_ref, v_ref, gate_ref, lg_ref, s0_ref, wall_ref, mask_ref, gn_ref,
                *rest, chunk, nheads, dk, dv):
    o_ref, sout_ref, s_ref = rest[-3:]
    c = pl.program_id(1)
    nlev = mask_ref.shape[0] - 1

    @pl.when(c == 0)
    def _():
        s_ref[...] = s0_ref[0]

    lg = lg_ref[...]
    p1, p2, p3 = _split3(lg)
    wall = wall_ref[...]
    bg = _dot(wall, p1) + _dot(wall, p2) + _dot(wall, p3)
    b = bg[0:chunk]
    q = q_ref[...] * (dk ** -0.5)
    k = k_ref[...]
    row = lax.broadcasted_iota(I32, q.shape, 0)
    qs, ks = [], []
    for l in range(nlev):
        w = chunk >> (l + 1)
        g = bg[(l + 1) * chunk:(l + 2) * chunk]
        second = (row & w) != 0
        e = jnp.exp(jnp.where(second, b - g, g - b))
        qk = jnp.where(second, q, k) * e
        qs.append(jnp.where(second, qk, 0.0).astype(BF16))
        ks.append(jnp.where(second, 0.0, qk).astype(BF16))
    qb = q.astype(BF16)
    kb = k.astype(BF16)
    q_in = (q * jnp.exp(b)).astype(BF16)
    k_dec = (k * jnp.exp(b[chunk - 1:chunk] - b)).astype(BF16)
    ones = jnp.ones((chunk, LANES), BF16)
    dcol_all = jnp.exp(_dot_t0(p1, ones) + _dot_t0(p2, ones) + _dot_t0(p3, ones))
    gn = gn_ref[...]
    for h in range(nheads):
        ck = slice(h * dk, (h + 1) * dk)
        cv = slice(h * dv, (h + 1) * dv)
        att = _dot_t1(qb[:, ck], kb[:, ck]) * mask_ref[nlev]
        for l in range(nlev):
            att = att + _dot_t1(qs[l][:, ck], ks[l][:, ck]) * mask_ref[l]
        vh = v_ref[:, cv].astype(BF16)
        s_h = s_ref[h]
        o = _dot(att.astype(BF16), vh) + _dot(q_in[:, ck], s_h.astype(BF16))
        dcol = dcol_all[h * dk:(h + 1) * dk, :]
        s_ref[h] = jnp.concatenate([dcol] * (dv // LANES), axis=1) * s_h + _dot_t0(k_dec[:, ck], vh)
        ms = jnp.mean(o * o, axis=-1, keepdims=True)
        on = o * lax.rsqrt(ms + EPS) * gn
        o_ref[:, cv] = (on * _silu(gate_ref[:, cv])).astype(o_ref.dtype)

    @pl.when(c == pl.num_programs(1) - 1)
    def _():
        sout_ref[0] = s_ref[...]


def _gla_call(reg_row0, nseq, seqlen, chunk, t_rows, proj, lg, s0, g_norm, prev):
    nheads, dk, dv = s0.shape[1:]
    hk = nheads * dk
    d = nheads * dv
    nch = seqlen // chunk
    rb0 = reg_row0 // chunk
    wall, masks = _gla_consts(chunk)
    rowblk = lambda b, c: rb0 + b * nch + c
    in_specs = [pl.BlockSpec((chunk, hk), lambda b, c: (rowblk(b, c), 0)),
                pl.BlockSpec((chunk, hk), lambda b, c: (rowblk(b, c), 1)),
                pl.BlockSpec((chunk, d), lambda b, c: (rowblk(b, c), 1)),
                pl.BlockSpec((chunk, d), lambda b, c: (rowblk(b, c), 2)),
                pl.BlockSpec((chunk, hk), lambda b, c: (rowblk(b, c), 0)),
                pl.BlockSpec((1, nheads, dk, dv), lambda b, c: (b, 0, 0, 0)),
                pl.BlockSpec(wall.shape, lambda b, c: (0, 0)),
                pl.BlockSpec(masks.shape, lambda b, c: (0, 0, 0)),
                pl.BlockSpec((1, dv), lambda b, c: (0, 0))]
    args = [proj, proj, proj, proj, lg, s0, wall, masks, g_norm]
    aliases = {}
    if prev is not None:
        aliases[len(args)] = 0
        in_specs.append(pl.BlockSpec(memory_space=pl.ANY))
        args.append(prev)
    return pl.pallas_call(
        functools.partial(_gla_kernel, chunk=chunk, nheads=nheads, dk=dk, dv=dv),
        grid=(nseq, nch),
        in_specs=in_specs,
        out_specs=[pl.BlockSpec((chunk, d), lambda b, c: (rowblk(b, c), 0)),
                   pl.BlockSpec((1, nheads, dk, dv), lambda b, c: (b, 0, 0, 0))],
        out_shape=[jax.ShapeDtypeStruct((t_rows, d), BF16),
                   jax.ShapeDtypeStruct(s0.shape, F32)],
        scratch_shapes=[pltpu.VMEM((nheads, dk, dv), F32)],
        input_output_aliases=aliases,
        compiler_params=_cparams(2),
        name="gla_chunk",
    )(*args)


def _dispatch_plan(idx, rank, counts, n_tokens):
    n_experts = counts.shape[0]
    rb = EXP_ROWBLK
    a = n_tokens * TOP_K
    blocks_e = (counts + rb - 1) // rb
    padded = blocks_e * rb
    pad_start = jnp.cumsum(padded) - padded
    hit = idx[:, :, None] == jnp.arange(n_experts, dtype=I32)[None, None, :]
    dest = jnp.sum(jnp.where(hit, pad_start[None, None, :], 0), axis=-1) + rank
    np_rows = (a + rb - 1) // rb * rb + rb * n_experts
    items_e = (blocks_e + EXP_NBLK - 1) // EXP_NBLK
    item_end = jnp.cumsum(items_e)
    item_start = item_end - items_e
    n_items = n_experts + (np_rows // rb) // EXP_NBLK
    ii = jnp.arange(n_items, dtype=I32)
    total = item_end[-1]
    e_of = jnp.minimum(jnp.sum(item_end[None, :] <= ii[:, None], axis=1), n_experts - 1).astype(I32)
    valid = ii < total
    local = ii - item_start[e_of]
    e_last = e_of[jnp.maximum(total - 1, 0)]
    ie = jnp.where(valid, e_of, e_last).astype(I32)
    rsb = jnp.where(valid, pad_start[e_of] // rb + local * EXP_NBLK, 0).astype(I32)
    nrb = jnp.where(valid, jnp.minimum(EXP_NBLK, blocks_e[e_of] - local * EXP_NBLK), 0).astype(I32)
    return ie, rsb, nrb, pad_start.astype(I32), dest.astype(I32), np_rows, n_items


def _dispatch_kernel(cnt_ref, pst_ref, dest_hbm, h_ref, xs_hbm, idx_ref, hbuf, zrow, sem_i, sem_d, sem_z,
                     *, ntiles, n_experts):
    i = pl.program_id(0)
    slot = i % 2
    tile = h_ref.shape[0]
    nidx = tile * TOP_K
    rb = EXP_ROWBLK

    def idx_copy(t, s):
        return pltpu.make_async_copy(dest_hbm.at[pl.ds(pl.multiple_of(t * nidx, nidx), nidx)],
                                     idx_ref.at[pl.ds(pl.multiple_of(s * nidx, nidx), nidx)], sem_i.at[s])

    def wait_rows(s):
        for _ in range(TOP_K):
            pltpu.make_async_copy(hbuf.at[s], xs_hbm.at[pl.ds(0, tile)], sem_d.at[s]).wait()

    @pl.when(i == 0)
    def _():
        idx_copy(0, 0).start()

    idx_copy(i, slot).wait()

    @pl.when(i + 1 < ntiles)
    def _():
        idx_copy(jnp.minimum(i + 1, ntiles - 1), 1 - slot).start()

    hbuf[slot] = h_ref[...]

    def tok(t, carry):
        base = slot * nidx + t * TOP_K
        for j in range(TOP_K):
            p = idx_ref[base + j]
            pltpu.make_async_copy(hbuf.at[slot, pl.ds(t, 1)], xs_hbm.at[pl.ds(p, 1)], sem_d.at[slot]).start()
        return carry
    lax.fori_loop(0, tile, tok, 0)

    @pl.when(i > 0)
    def _():
        wait_rows(1 - slot)

    @pl.when(i == ntiles - 1)
    def _():
        wait_rows(slot)
        zrow[...] = jnp.zeros(zrow.shape, zrow.dtype)

        def expert(e, carry):
            cnt = cnt_ref[e]
            base = pst_ref[e]
            end = (cnt + rb - 1) // rb * rb

            def zstart(r, c2):
                pltpu.make_async_copy(zrow.at[pl.ds(0, 1)], xs_hbm.at[pl.ds(base + r, 1)], sem_z).start()
                return c2

            def zwait(r, c2):
                pltpu.make_async_copy(zrow.at[pl.ds(0, 1)], xs_hbm.at[pl.ds(0, 1)], sem_z).wait()
                return c2
            lax.fori_loop(cnt, end, zstart, 0)
            lax.fori_loop(cnt, end, zwait, 0)
            return carry
        lax.fori_loop(0, n_experts, expert, 0)


def _dispatch_call(counts, pad_start, dest_flat, h_packed, np_rows, tile):
    t_rows, dh = h_packed.shape
    ntiles = t_rows // tile
    n_experts = counts.shape[0]
    grid_spec = pltpu.PrefetchScalarGridSpec(
        num_scalar_prefetch=2,
        grid=(ntiles,),
        in_specs=[pl.BlockSpec(memory_space=pl.ANY),
                  pl.BlockSpec((tile, dh), lambda i, c, p: (i, 0))],
        out_specs=pl.BlockSpec(memory_space=pl.ANY),
        scratch_shapes=[pltpu.SMEM((2 * tile * TOP_K,), I32),
                        pltpu.VMEM((2, tile, dh), jnp.uint32),
                        pltpu.VMEM((SUBLANES, dh), jnp.uint32),
                        pltpu.SemaphoreType.DMA((2,)),
                        pltpu.SemaphoreType.DMA((2,)),
                        pltpu.SemaphoreType.DMA])
    return pl.pallas_call(
        functools.partial(_dispatch_kernel, ntiles=ntiles, n_experts=n_experts),
        grid_spec=grid_spec,
        out_shape=jax.ShapeDtypeStruct((np_rows, dh), jnp.uint32),
        compiler_params=_cparams(1),
        name="moe_dispatch",
    )(counts, pad_start, dest_flat, h_packed)


def _experts_kernel(ie_ref, rsb_ref, nrb_ref, xs_hbm, wi_ref, wo_ref, y_hbm,
                    xt, acc, act, ybuf, wibf, wobf, sem_x, sem_y, ycnt_ref, *, n_items, n_k, n_b):
    i = pl.program_id(0)
    st = pl.program_id(1)
    nrb = nrb_ref[i]
    rb = EXP_ROWBLK
    tkw = xt.shape[2]
    tdw = ybuf.shape[2]
    f = act.shape[1]

    def for_chunks(nblocks, fn):
        b0 = 0
        for size in EXP_CHUNKS:
            cnt = (nblocks - b0) // size

            def body(ci, carry, size=size, b0=b0):
                fn(b0 + ci * size, size)
                return carry
            lax.fori_loop(0, cnt, body, 0)
            b0 = b0 + cnt * size

    def rows_of(b0, nblk):
        return pl.ds(pl.multiple_of(b0 * rb, rb), nblk * rb)

    def x_copy(item, kt, b0, nblk):
        src = pl.multiple_of((rsb_ref[item] + b0) * rb, rb)
        return pltpu.make_async_copy(
            xs_hbm.at[pl.ds(src, nblk * rb), pl.ds(pl.multiple_of(kt * tkw, tkw), tkw)],
            xt.at[kt % 2, rows_of(b0, nblk)], sem_x.at[kt % 2])

    def start_x(item, kt):
        for_chunks(nrb_ref[item], lambda b0, nblk: x_copy(item, kt, b0, nblk).start())

    def wait_x(item, kt):
        for_chunks(nrb_ref[item], lambda b0, nblk: x_copy(item, kt, b0, nblk).wait())

    def wait_out(s):
        for_chunks(ycnt_ref[s], lambda b0, nblk: pltpu.make_async_copy(
            ybuf.at[s, pl.ds(0, nblk * rb)], y_hbm.at[pl.ds(0, nblk * rb), pl.ds(0, tdw)], sem_y.at[s]).wait())
        ycnt_ref[s] = 0

    @pl.when((i == 0) & (st == 0))
    def _():
        ycnt_ref[0] = 0
        ycnt_ref[1] = 0
        xt[...] = jnp.zeros(xt.shape, xt.dtype)
        start_x(0, 0)

    @pl.when(st < n_k)
    def _():
        wait_x(i, st)

    @pl.when(st + 1 < n_k)
    def _():
        start_x(i, st + 1)

    @pl.when((st == n_k) & (i + 1 < n_items))
    def _():
        start_x(jnp.minimum(i + 1, n_items - 1), 0)

    @pl.when((st < n_k) & (nrb > 0))
    def _():
        wibf[...] = wi_ref[0, 0].astype(BF16)
        slot = st % 2

        def partial_sum(rows):
            return _dot(_unpack_pairs(xt[slot, rows, :], BF16), wibf[...])

        if n_k > 1:
            @pl.when(st == 0)
            def _():
                def first(b0, nblk):
                    rows = rows_of(b0, nblk)
                    acc[rows, :] = partial_sum(rows)
                for_chunks(nrb, first)

            @pl.when((st > 0) & (st < n_k - 1))
            def _():
                def middle(b0, nblk):
                    rows = rows_of(b0, nblk)
                    acc[rows, :] = acc[rows, :] + partial_sum(rows)
                for_chunks(nrb, middle)

        @pl.when(st == n_k - 1)
        def _():
            def last(b0, nblk):
                rows = rows_of(b0, nblk)
                a = partial_sum(rows)
                if n_k > 1:
                    a = a + acc[rows, :]
                act[rows, :] = (_silu(a[:, :f]) * a[:, f:]).astype(BF16)
            for_chunks(nrb, last)

    @pl.when((st >= n_k) & (nrb > 0))
    def _():
        dj = st - n_k
        ys = (i * n_b + dj) % 2
        wobf[...] = wo_ref[0, 0].astype(BF16)
        wait_out(ys)
        row0 = rsb_ref[i] * rb
        col0 = pl.multiple_of(dj * tdw, tdw)

        def chunk(b0, nblk):
            rows = rows_of(b0, nblk)
            ybuf[ys, rows, :] = _pack_pairs(_dot(act[rows, :], wobf[...]))
            pltpu.make_async_copy(
                ybuf.at[ys, rows],
                y_hbm.at[pl.ds(pl.multiple_of(row0 + b0 * rb, rb), nblk * rb), pl.ds(col0, tdw)],
                sem_y.at[ys]).start()
        for_chunks(nrb, chunk)
        ycnt_ref[ys] = nrb

    @pl.when((i == n_items - 1) & (st == n_k + n_b - 1))
    def _():
        wait_out(0)
        wait_out(1)


def _experts_call(layer, ie, rsb, nrb, n_items, xs, w_in, w_out):
    _, _, d, f2 = w_in.shape
    f = f2 // 2
    np_rows, dh = xs.shape
    tk = min(EXP_TK, d)
    td = min(EXP_TD, d)
    n_k, n_b = d // tk, d // td
    rmax = EXP_NBLK * EXP_ROWBLK

    def kt(st, nr):
        return jnp.where((nr > 0) & (st < n_k), st, n_k - 1)

    def dj(st, nr):
        return jnp.where(nr > 0, jnp.maximum(st - n_k, 0), n_b - 1)

    grid_spec = pltpu.PrefetchScalarGridSpec(
        num_scalar_prefetch=3,
        grid=(n_items, n_k + n_b),
        in_specs=[pl.BlockSpec(memory_space=pl.ANY),
                  pl.BlockSpec((1, 1, tk, f2), lambda i, st, ie, rsb, nrb: (layer, ie[i], kt(st, nrb[i]), 0)),
                  pl.BlockSpec((1, 1, f, td), lambda i, st, ie, rsb, nrb: (layer, ie[i], 0, dj(st, nrb[i])))],
        out_specs=pl.BlockSpec(memory_space=pl.ANY),
        scratch_shapes=[pltpu.VMEM((2, rmax, tk // 2), jnp.uint32),
                        pltpu.VMEM((rmax, f2), F32),
                        pltpu.VMEM((rmax, f), BF16),
                        pltpu.VMEM((2, rmax, td // 2), jnp.uint32),
                        pltpu.VMEM((tk, f2), BF16),
                        pltpu.VMEM((f, td), BF16),
                        pltpu.SemaphoreType.DMA((2,)),
                        pltpu.SemaphoreType.DMA((2,)),
                        pltpu.SMEM((2,), I32)])
    return pl.pallas_call(
        functools.partial(_experts_kernel, n_items=n_items, n_k=n_k, n_b=n_b),
        grid_spec=grid_spec,
        out_shape=jax.ShapeDtypeStruct((np_rows, dh), jnp.uint32),
        compiler_params=_cparams(2),
        name="moe_experts",
    )(ie, rsb, nrb, xs, w_in, w_out)


def _combine_kernel(pos_hbm, w_ref, sh_ref, x_ref, g_ref, fg_ref, y_hbm, *rest,
                    lrep, blk0, ntiles, final):
    o_ref, idx_ref, gbuf, sem_i, sem_g = rest[-5:]
    i = pl.program_id(0)
    slot = i % 2
    nslot = 1 - slot
    tile = x_ref.shape[0]
    nidx = tile * TOP_K

    def idx_copy(t, s):
        return pltpu.make_async_copy(pos_hbm.at[pl.ds(pl.multiple_of((blk0 + t) * nidx, nidx), nidx)],
                                     idx_ref.at[pl.ds(pl.multiple_of(s * nidx, nidx), nidx)], sem_i.at[s])

    def issue_gather(s):
        def tok(t, carry):
            base = s * nidx + t * TOP_K
            for j in range(TOP_K):
                p = idx_ref[base + j]
                pltpu.make_async_copy(y_hbm.at[pl.ds(p, 1)], gbuf.at[s, j, pl.ds(t, 1)], sem_g.at[s]).start()
            return carry
        lax.fori_loop(0, tile, tok, 0)

    @pl.when(i == 0)
    def _():
        first = idx_copy(0, 0)
        first.start()
        first.wait()
        issue_gather(0)
        if ntiles > 1:
            idx_copy(1, 1).start()

    @pl.when(i + 1 < ntiles)
    def _():
        idx_copy(jnp.minimum(i + 1, ntiles - 1), nslot).wait()

    for j in range(TOP_K):
        pltpu.make_async_copy(y_hbm.at[pl.ds(0, tile)], gbuf.at[slot, j], sem_g.at[slot]).wait()

    @pl.when(i + 1 < ntiles)
    def _():
        issue_gather(nslot)

    @pl.when(i + 2 < ntiles)
    def _():
        idx_copy(jnp.minimum(i + 2, ntiles - 1), slot).start()

    acc = sh_ref[...]
    for j in range(TOP_K):
        acc = acc + w_ref[:, j:j + 1] * _unpack_pairs(gbuf[slot, j], F32)
    o = x_ref[...] + _expand_rows(g_ref[...], lrep, tile, i * tile) * acc
    if final:
        ms = jnp.mean(o * o, axis=-1, keepdims=True)
        o = o * lax.rsqrt(ms + EPS) * fg_ref[...]
    o_ref[...] = o


def _combine_call(reg, t_rows, pos_flat, wts, shared, x_all, g, final_g, y_sorted, final, prev):
    d = x_all.shape[1]
    row = lambda i: (reg.blk0 + i, 0)
    fixed = lambda i: (0, 0)
    in_specs = [pl.BlockSpec(memory_space=pl.ANY),
                pl.BlockSpec((reg.tile, TOP_K), row),
                pl.BlockSpec((reg.tile, d), row),
                pl.BlockSpec((reg.tile, d), row),
                pl.BlockSpec((reg.nseq, d), fixed),
                pl.BlockSpec((1, d), fixed),
                pl.BlockSpec(memory_space=pl.ANY)]
    args = [pos_flat, wts, shared, x_all, g, final_g, y_sorted]
    aliases = {}
    if prev is not None:
        aliases[len(args)] = 0
        in_specs.append(pl.BlockSpec(memory_space=pl.ANY))
        args.append(prev)
    if final:
        out_spec = pl.BlockSpec((reg.tile, d), lambda i: (i, 0))
        out_rows = reg.rows
    else:
        out_spec = pl.BlockSpec((reg.tile, d), row)
        out_rows = t_rows
    return pl.pallas_call(
        functools.partial(_combine_kernel, lrep=reg.lrep, blk0=reg.blk0, ntiles=reg.ntiles, final=final),
        grid=(reg.ntiles,),
        in_specs=in_specs,
        out_specs=out_spec,
        out_shape=jax.ShapeDtypeStruct((out_rows, d), F32),
        scratch_shapes=[pltpu.SMEM((2 * reg.tile * TOP_K,), I32),
                        pltpu.VMEM((2, TOP_K, reg.tile, d // 2), jnp.uint32),
                        pltpu.SemaphoreType.DMA((2,)),
                        pltpu.SemaphoreType.DMA((2,))],
        input_output_aliases=aliases,
        compiler_params=_cparams(1),
        name="moe_combine",
    )(*args)


def _both(fn, regs):
    out = fn(regs[0], None)
    return fn(regs[1], out)


def kernel(x_prompt, x_sample, state_s5_re, state_s5_im, state_gla, c_prompt, c_sample, w_ada, b_ada, norm_g, s5_lam_re, s5_lam_im, s5_log_dt, s5_b_re, s5_b_im, s5_c_re, s5_c_im, s5_d, s5_w_glu, s5_b_glu, gla_w_in, gla_w_a1, gla_w_a2, gla_b_a, gla_g_norm, gla_w_o, moe_w_router, moe_b_router, moe_w_in, moe_w_out, moe_ws_in, moe_ws_out, final_g):
    bp, lp, d = x_prompt.shape
    bs, ls, _ = x_sample.shape
    assert bp == 1 and bs == SUBLANES and ls % SUBLANES == 0
    depth = w_ada.shape[0]
    n_exp = moe_w_in.shape[1]
    f_sh = moe_ws_out.shape[1]
    srows = bs * ls
    t_rows = lp + srows
    s5_steps = 64
    s5_blk = s5_steps * SUBLANES
    gla_chunk = 64
    assert lp % PROMPT_TILE == 0 and lp % s5_blk == 0 and lp % srows == 0

    mm_tile = MM_TILE if lp % MM_TILE == 0 else PROMPT_TILE
    regs_mm = (_Region(0, lp, mm_tile, 1, mm_tile), _Region(lp, srows, srows, bs, ls))
    regs_nm = (_Region(0, lp, PROMPT_TILE // 2, 1, PROMPT_TILE // 2), _Region(lp, srows, srows, bs, ls))
    regs_cb = (_Region(0, lp, COMB_TILE, 1, COMB_TILE), _Region(lp, srows, COMB_TILE, bs, ls))

    x_parts = (x_prompt.reshape(lp, d), x_sample.reshape(srows, d))
    x_all = None

    def x_of(r):
        return x_parts[r.nseq > 1] if x_parts is not None else x_all

    c_all = jnp.concatenate([c_prompt, c_sample, jnp.zeros((2 * SUBLANES - 1 - bs, d), F32)], axis=0)
    mod = _ada_call(c_all, w_ada, b_ada)

    def mods(layer, k):
        m = mod[layer, :, k * d:(k + 1) * d]
        return (m[0:1], m[1:1 + bs])

    new_re_p, new_im_p, new_gla_p, new_re_s, new_im_s, new_gla_s = [], [], [], [], [], []
    for i in range(depth):
        sh1, sc1, g1, sh2, sc2, g2 = [mods(i, k) for k in range(6)]
        j = i // 2
        ng1 = norm_g[i, 0].reshape(1, d)
        ng2 = norm_g[i, 1].reshape(1, d)
        if i % 2 == 0:
            own = x_parts is not None
            (h_all,) = _both(lambda r, prev: _norm_call(
                "plain", r, t_rows, x_of(r), ng1, sc1[r.nseq > 1], sh1[r.nseq > 1], [], [(d, F32)], prev,
                x_own=own), regs_nm)
            grp, pst = s5_lam_re.shape[1:]
            ncol = grp * pst
            lam, bbr, bbi = _s5_disc_call(s5_lam_re[j], s5_lam_im[j], s5_log_dt[j], s5_b_re[j], s5_b_im[j], s5_steps)
            bmat, cmat = _s5_block_weights(bbr, bbi, s5_c_re[j], s5_c_im[j])
            kt = bmat.shape[0]
            lamt = jnp.transpose(lam.reshape(4, kt, ncol // kt), (1, 0, 2))
            lamt = jnp.concatenate([lamt, jnp.zeros_like(lamt)], axis=1)
            dsk = s5_d[j].reshape(1, d)
            zeros_st = jnp.zeros((SUBLANES, ncol), F32)
            z_all, pre, pim = _s5_call(0, lp, t_rows, h_all, bmat, cmat, lamt, dsk, zeros_st, zeros_st,
                                       s5_steps, True, None)
            z_all, sre, sim = _s5_call(lp, srows, t_rows, h_all, bmat, cmat, lamt, dsk,
                                       state_s5_re[j].reshape(bs, ncol), state_s5_im[j].reshape(bs, ncol),
                                       ls, False, z_all)
            new_re_p.append(pre[0].reshape(1, grp, pst))
            new_im_p.append(pim[0].reshape(1, grp, pst))
            new_re_s.append(sre.reshape(bs, grp, pst))
            new_im_s.append(sim.reshape(bs, grp, pst))
            bglu = s5_b_glu[j].reshape(1, d)
            x_all = _both(lambda r, prev: _mm_call(
                "glu", r, t_rows, z_all, [s5_w_glu[j]], [0], d, 512, F32,
                [(bglu, "col"), (z_all, "tile"), (x_of(r), "own" if own else "tile"), (g1[r.nseq > 1], "seq")],
                prev, "s5_glu"), regs_mm)
            x_parts = None
        else:
            hk = gla_w_a2.shape[2]
            assert 2 * hk == d
            own = x_parts is not None
            h_bf, lg = _both(lambda r, prev: _norm_call(
                "gate", r, t_rows, x_of(r), ng1, sc1[r.nseq > 1], sh1[r.nseq > 1],
                [gla_w_a1[j], gla_w_a2[j], gla_b_a[j].reshape(1, hk)], [(d, BF16), (hk, F32)], prev,
                x_own=own), regs_nm)
            proj = _both(lambda r, prev: _mm_call(
                "plain", r, t_rows, h_bf, [gla_w_in[j]], [0], 3 * d, 512, F32, [], prev, "gla_proj"), regs_mm)
            gn = gla_g_norm[j].reshape(1, -1)
            s0p = jnp.zeros((1,) + state_gla.shape[2:], F32)
            o_all, gla_p = _gla_call(0, 1, lp, gla_chunk, t_rows, proj, lg, s0p, gn, None)
            o_all, gla_s = _gla_call(lp, bs, ls, ls, t_rows, proj, lg, state_gla[j], gn, o_all)
            new_gla_p.append(gla_p)
            new_gla_s.append(gla_s)
            x_all = _both(lambda r, prev: _mm_call(
                "res", r, t_rows, o_all, [gla_w_o[j]], [0], d, 512, F32,
                [(x_of(r), "own" if own else "tile"), (g1[r.nseq > 1], "seq")], prev, "gla_out"), regs_mm)
            x_parts = None

        brt = moe_b_router[i].reshape(1, n_exp)
        r_p = _router_call(regs_nm[0], t_rows, x_all, ng2, sc2[0], sh2[0], moe_w_router[i], brt,
                           jnp.zeros((SUBLANES, n_exp), F32), None)
        h_pk, h_bf, idx, wts, rank, cnt = _router_call(
            regs_nm[1], t_rows, x_all, ng2, sc2[1], sh2[1], moe_w_router[i], brt, r_p[5], r_p[:5])
        counts = cnt[0].astype(I32)
        ie, rsb, nrb, pad_start, dest, np_rows, n_items = _dispatch_plan(idx, rank, counts, t_rows)
        pos_flat = dest.reshape(-1)
        xs = _dispatch_call(counts, pad_start, pos_flat, h_pk, np_rows, DISP_TILE)
        y_sorted = _experts_call(i, ie, rsb, nrb, n_items, xs, moe_w_in, moe_w_out)
        act = _both(lambda r, prev: _mm_call(
            "swiglu", r, t_rows, h_bf, [moe_ws_in[i], moe_ws_in[i]], [0, f_sh // 256], f_sh, 256, BF16,
            [], prev, "shared_in"), regs_mm)
        shared = _both(lambda r, prev: _mm_call(
            "plain", r, t_rows, act, [moe_ws_out[i]], [0], d, 512, F32, [], prev, "shared_out"), regs_mm)
        last = i == depth - 1
        fg = final_g.reshape(1, d)
        if last:
            y_prompt, y_sample = [_combine_call(r, t_rows, pos_flat, wts, shared, x_all, g2[r.nseq > 1], fg,
                                                y_sorted, True, None) for r in regs_cb]
        else:
            x_all = _both(lambda r, prev: _combine_call(
                r, t_rows, pos_flat, wts, shared, x_all, g2[r.nseq > 1], fg, y_sorted, False, prev), regs_cb)

    y_prompt = y_prompt.reshape(bp, lp, d)
    y_sample = y_sample.reshape(bs, ls, d)
    return (y_prompt, y_sample, jnp.stack(new_re_p), jnp.stack(new_im_p), jnp.stack(new_gla_p),
            jnp.stack(new_re_s), jnp.stack(new_im_s), jnp.stack(new_gla_s))
```

```python
import functools

import numpy as np
import jax
import jax.numpy as jnp
from jax import lax
from jax.experimental import pallas as pl
from jax.experimental.pallas import tpu as pltpu

F32 = jnp.float32
BF16 = jnp.bfloat16
I32 = jnp.int32

EPS = 1e-6
GLA_TAU = 16.0
TOP_K = 8
N_GROUPS = 8
TOPK_GROUPS = 4
ROUTED_SCALE = 2.5

LANES = 128
SUBLANES = 8
MXU_DIM = 256
VMEM_LIMIT = 56 << 20

PROMPT_TILE = 512
MM_TILE = 1024
S5_KTILE = 256
S5_COLS = 512
GLA_SUBCHUNKS = 2
EXP_ROWBLK = 128
EXP_NBLK = 18
EXP_CHUNKS = (4, 2, 1)
EXP_TK = 1024
EXP_TD = 1024
PACK_GROUP = 512
DISP_TILE = 128
COMB_TILE = 128


def _cparams(n_axes, vmem=VMEM_LIMIT):
    return pltpu.CompilerParams(dimension_semantics=("arbitrary",) * n_axes, vmem_limit_bytes=vmem)


class _Region:
    def __init__(self, row0, rows, tile, nseq, lrep):
        self.row0, self.rows, self.tile, self.nseq, self.lrep = row0, rows, tile, nseq, lrep
        self.blk0 = row0 // tile
        self.ntiles = rows // tile
        assert row0 % tile == 0 and rows % tile == 0


def _expand_rows(m, lrep, rows=None, row0=0):
    nseq, n = m.shape
    if nseq == 1:
        return m
    rows = nseq * lrep if rows is None else rows
    r = lax.broadcasted_iota(I32, (rows, n), 0) + row0
    out = jnp.broadcast_to(m[0:1, :], (rows, n))
    for b in range(1, nseq):
        out = jnp.where(r >= b * lrep, jnp.broadcast_to(m[b:b + 1, :], (rows, n)), out)
    return out


def _sigmoid(x):
    return 1.0 / (1.0 + jnp.exp(-x))


def _silu(x):
    return x * _sigmoid(x)


def _gelu_tanh(x):
    return 0.5 * x * (1.0 + jnp.tanh(0.7978845608028654 * (x + 0.044715 * x * x * x)))


def _log_sigmoid(x):
    return jnp.minimum(x, 0.0) - jnp.log1p(jnp.exp(-jnp.abs(x)))


def _split3(x):
    p1 = x.astype(BF16)
    r1 = x - p1.astype(F32)
    p2 = r1.astype(BF16)
    r2 = r1 - p2.astype(F32)
    return p1, p2, r2.astype(BF16)


def _pack_pairs(x):
    half = PACK_GROUP // 2
    words = []
    for g in range(x.shape[1] // PACK_GROUP):
        lo = pltpu.bitcast(x[:, g * PACK_GROUP:g * PACK_GROUP + half].astype(BF16).astype(F32), jnp.uint32)
        hi = pltpu.bitcast(x[:, g * PACK_GROUP + half:(g + 1) * PACK_GROUP].astype(BF16).astype(F32), jnp.uint32)
        words.append(hi | (lo >> 16))
    return words[0] if len(words) == 1 else jnp.concatenate(words, axis=1)


def _unpack_pairs(w, dtype):
    half = PACK_GROUP // 2
    parts = []
    for g in range(w.shape[1] // half):
        ww = w[:, g * half:(g + 1) * half]
        parts.append(pltpu.bitcast(ww << 16, F32).astype(dtype))
        parts.append(pltpu.bitcast(ww & jnp.uint32(0xFFFF0000), F32).astype(dtype))
    return jnp.concatenate(parts, axis=1)


def _dot(a, b):
    return jnp.dot(a, b, preferred_element_type=F32)


def _dot_t0(a, b):
    return lax.dot_general(a, b, (((0,), (0,)), ((), ())), preferred_element_type=F32)


def _dot_t1(a, b):
    return lax.dot_general(a, b, (((1,), (1,)), ((), ())), preferred_element_type=F32)


def _ada_kernel(c_ref, w_ref, b_ref, o_ref):
    s = _silu(c_ref[...])
    o_ref[0] = _dot(s.astype(BF16), w_ref[0].astype(BF16)) + b_ref[0]


def _ada_call(c_all, w_ada, b_ada):
    depth, d, n6 = w_ada.shape
    nc = c_all.shape[0]
    tn = 512
    return pl.pallas_call(
        _ada_kernel,
        grid=(depth, n6 // tn),
        in_specs=[pl.BlockSpec((nc, d), lambda l, j: (0, 0)),
                  pl.BlockSpec((1, d, tn), lambda l, j: (l, 0, j)),
                  pl.BlockSpec((1, 1, tn), lambda l, j: (l, 0, j))],
        out_specs=pl.BlockSpec((1, nc, tn), lambda l, j: (l, 0, j)),
        out_shape=jax.ShapeDtypeStruct((depth, nc, n6), F32),
        compiler_params=_cparams(2),
        name="adaln",
    )(c_all, w_ada, b_ada.reshape(depth, 1, n6))


def _norm_mod(x, g, sc, sh, lrep):
    ms = jnp.mean(x * x, axis=-1, keepdims=True)
    y = x * lax.rsqrt(ms + EPS) * g
    return y * (1.0 + _expand_rows(sc, lrep)) + _expand_rows(sh, lrep)


def _norm_kernel(x_ref, g_ref, sc_ref, sh_ref, *rest, lrep, aliased):
    o_ref = rest[-1]
    h = _norm_mod(x_ref[...], g_ref[...], sc_ref[...], sh_ref[...], lrep)
    o_ref[...] = h.astype(o_ref.dtype)


def _cols_to_lanes(cols, dtype):
    r = cols[0].shape[0]
    lane = lax.broadcasted_iota(I32, (r, len(cols)), 1)
    out = jnp.zeros((r, len(cols)), dtype)
    for j, c in enumerate(cols):
        out = jnp.where(lane == j, c.astype(dtype), out)
    return out


def _router_kernel(x_ref, g_ref, sc_ref, sh_ref, w_ref, b_ref, cin_ref, *rest, lrep):
    hp_ref, hb_ref, idx_ref, wt_ref, rk_ref, cout_ref, run_ref = rest[-7:]
    i = pl.program_id(0)

    @pl.when(i == 0)
    def _():
        run_ref[...] = cin_ref[0:1, :]

    h = _norm_mod(x_ref[...], g_ref[...], sc_ref[...], sh_ref[...], lrep)
    hb_ref[...] = h.astype(BF16)
    hp_ref[...] = _pack_pairs(h)
    w = w_ref[...]
    hh = h.astype(BF16)
    hl = (h - hh.astype(F32)).astype(BF16)
    wh = w.astype(BF16)
    wl = (w - wh.astype(F32)).astype(BF16)
    scores = _sigmoid(_dot(hh, wh) + _dot(hl, wh) + _dot(hh, wl))

    rows, n_exp = scores.shape
    per = n_exp // N_GROUPS
    neg = -jnp.inf
    big = n_exp + 1
    lane_i = lax.broadcasted_iota(I32, (rows, n_exp), 1)
    lane = lane_i.astype(F32)
    grp_i = lane_i // per
    grp = grp_i.astype(F32)
    choice = scores + b_ref[...]
    gs = jnp.zeros((rows, n_exp), F32)
    for g in range(N_GROUPS):
        ing = grp_i == g
        m = jnp.where(ing, choice, neg)
        m1 = jnp.max(m, axis=-1, keepdims=True)
        top = m == m1
        ntop = jnp.sum(jnp.where(top, 1.0, 0.0), axis=-1, keepdims=True)
        below = jnp.max(jnp.where(top, neg, m), axis=-1, keepdims=True)
        m2 = jnp.where(ntop > 1.5, m1, below)
        gs = jnp.where(ing, m1 + m2, gs)
    masked = jnp.full((rows, n_exp), neg, F32)
    for _ in range(TOPK_GROUPS):
        mx = jnp.max(gs, axis=-1, keepdims=True)
        gi = jnp.min(jnp.where(gs == mx, grp, big), axis=-1, keepdims=True)
        sel = grp == gi
        masked = jnp.where(sel, choice, masked)
        gs = jnp.where(sel, neg, gs)
    idx_cols, w_cols, hits = [], [], []
    for _ in range(TOP_K):
        mx = jnp.max(masked, axis=-1, keepdims=True)
        ei = jnp.min(jnp.where(masked == mx, lane, big), axis=-1, keepdims=True)
        hit = lane == ei
        hits.append(hit)
        idx_cols.append(ei)
        w_cols.append(jnp.sum(jnp.where(hit, scores, 0.0), axis=-1, keepdims=True))
        masked = jnp.where(hit, neg, masked)
    wsum = w_cols[0]
    for c in w_cols[1:]:
        wsum = wsum + c
    scale = ROUTED_SCALE / wsum
    idx_ref[...] = _cols_to_lanes(idx_cols, I32)
    wt_ref[...] = _cols_to_lanes([c * scale for c in w_cols], F32)
    onehot = jnp.zeros((rows, n_exp), F32)
    for hit in hits:
        onehot = jnp.where(hit, 1.0, onehot)
    rr = lax.broadcasted_iota(I32, (rows, rows), 0)
    cc = lax.broadcasted_iota(I32, (rows, rows), 1)
    before = jnp.where(rr > cc, 1.0, 0.0).astype(BF16)
    cum = _dot(before, onehot.astype(BF16)) + run_ref[...]
    rk_ref[...] = _cols_to_lanes(
        [jnp.sum(jnp.where(hit, cum, 0.0), axis=-1, keepdims=True) for hit in hits], I32)
    run_ref[...] = run_ref[...] + jnp.sum(onehot, axis=0, keepdims=True)
    cout_ref[...] = jnp.broadcast_to(run_ref[...], cout_ref.shape)


def _router_call(reg, t_rows, x_all, g, sc, sh, w_router, b_router, cnt_in, prev):
    d = x_all.shape[1]
    n_exp = w_router.shape[1]
    row = lambda i: (reg.blk0 + i, 0)
    fixed = lambda i: (0, 0)
    in_specs = [pl.BlockSpec((reg.tile, d), row), pl.BlockSpec((1, d), fixed),
                pl.BlockSpec((reg.nseq, d), fixed), pl.BlockSpec((reg.nseq, d), fixed),
                pl.BlockSpec((d, n_exp), fixed), pl.BlockSpec((1, n_exp), fixed),
                pl.BlockSpec((SUBLANES, n_exp), fixed)]
    args = [x_all, g, sc, sh, w_router, b_router, cnt_in]
    outs = [(d // 2, jnp.uint32), (d, BF16), (TOP_K, I32), (TOP_K, F32), (TOP_K, I32)]
    aliases = {}
    if prev is not None:
        for k in range(len(outs)):
            aliases[len(args)] = k
            in_specs.append(pl.BlockSpec(memory_space=pl.ANY))
            args.append(prev[k])
    res = pl.pallas_call(
        functools.partial(_router_kernel, lrep=reg.lrep),
        grid=(reg.ntiles,),
        in_specs=in_specs,
        out_specs=[pl.BlockSpec((reg.tile, n), row) for n, _ in outs] + [pl.BlockSpec((SUBLANES, n_exp), fixed)],
        out_shape=[jax.ShapeDtypeStruct((t_rows, n), dt) for n, dt in outs]
        + [jax.ShapeDtypeStruct((SUBLANES, n_exp), F32)],
        scratch_shapes=[pltpu.VMEM((1, n_exp), F32)],
        input_output_aliases=aliases,
        compiler_params=_cparams(1),
        name="moe_router",
    )(*args)
    return list(res)


def _norm_gate_kernel(x_ref, g_ref, sc_ref, sh_ref, wa1_ref, wa2_ref, ba_ref, *rest, lrep, aliased):
    h_ref, lg_ref = rest[-2:]
    h = _norm_mod(x_ref[...], g_ref[...], sc_ref[...], sh_ref[...], lrep)
    hb = h.astype(BF16)
    h_ref[...] = hb
    a = _dot(hb, wa1_ref[...].astype(BF16))
    z = _dot(a.astype(BF16), wa2_ref[...].astype(BF16)) + ba_ref[...]
    lg_ref[...] = _log_sigmoid(z) * (1.0 / GLA_TAU)


def _norm_call(kind, reg, t_rows, x_all, g, sc, sh, extra, outs, prev, x_own=False):
    d = x_all.shape[1]
    kern = {"plain": _norm_kernel, "gate": _norm_gate_kernel}[kind]
    row = lambda i: (reg.blk0 + i, 0)
    xrow = (lambda i: (i, 0)) if x_own else row
    fixed = lambda i: (0, 0)
    in_specs = [pl.BlockSpec((reg.tile, d), xrow), pl.BlockSpec((1, d), fixed),
                pl.BlockSpec((reg.nseq, d), fixed), pl.BlockSpec((reg.nseq, d), fixed)]
    args = [x_all, g, sc, sh]
    for e in extra:
        in_specs.append(pl.BlockSpec(e.shape, fixed))
        args.append(e)
    aliases = {}
    if prev is not None:
        for k, p in enumerate(prev):
            aliases[len(args)] = k
            in_specs.append(pl.BlockSpec(memory_space=pl.ANY))
            args.append(p)
    res = pl.pallas_call(
        functools.partial(kern, lrep=reg.lrep, aliased=prev is not None),
        grid=(reg.ntiles,),
        in_specs=in_specs,
        out_specs=[pl.BlockSpec((reg.tile, n), row) for n, _ in outs],
        out_shape=[jax.ShapeDtypeStruct((t_rows, n), dt) for n, dt in outs],
        input_output_aliases=aliases,
        compiler_params=_cparams(1),
        name="norm_" + kind,
    )(*args)
    return list(res)


def _s5_disc_kernel(lr_ref, li_ref, ldt_ref, br_ref, bi_ref, lam_ref, bbr_ref, bbi_ref, *, nsteps):
    lr = lr_ref[...]
    li = li_ref[...]
    dt = jnp.exp(ldt_ref[...])
    mag = jnp.exp(lr * dt)
    ar = mag * jnp.cos(li * dt)
    ai = mag * jnp.sin(li * dt)
    den = lr * lr + li * li
    fr = ((ar - 1.0) * lr + ai * li) / den
    fi = (ai * lr - (ar - 1.0) * li) / den
    lam_ref[0] = ar
    lam_ref[1] = ai
    mags = jnp.exp(nsteps * (lr * dt))
    lam_ref[2] = mags * jnp.cos(nsteps * (li * dt))
    lam_ref[3] = mags * jnp.sin(nsteps * (li * dt))
    for h in range(br_ref.shape[0]):
        br = br_ref[h]
        bi = bi_ref[h]
        bbr_ref[h] = fr * br - fi * bi
        bbi_ref[h] = fr * bi + fi * br


def _s5_disc_call(lam_re, lam_im, log_dt, b_re, b_im, nsteps):
    g, p = lam_re.shape
    hg = b_re.shape[2]
    ldt = jnp.broadcast_to(log_dt[:, None], (g, p))
    brt = jnp.transpose(b_re, (2, 0, 1))
    bit = jnp.transpose(b_im, (2, 0, 1))
    return pl.pallas_call(
        functools.partial(_s5_disc_kernel, nsteps=float(nsteps)),
        out_shape=[jax.ShapeDtypeStruct((4, g, p), F32),
                   jax.ShapeDtypeStruct((hg, g, p), F32),
                   jax.ShapeDtypeStruct((hg, g, p), F32)],
        name="s5_disc",
    )(lam_re, lam_im, ldt, brt, bit)


def _s5_block_weights(bbr, bbi, c_re, c_im):
    hg, g, p = bbr.shape
    gt = S5_KTILE // hg
    kt = g // gt
    eye = jnp.eye(gt, dtype=F32)

    def bd_in(b):
        b = b.reshape(hg, kt, gt, p)
        return jnp.einsum("hkgp,gq->kghqp", b, eye).reshape(kt, gt * hg, gt * p)

    def bd_out(c):
        c = c.reshape(kt, gt, hg, p)
        return jnp.einsum("kghp,gq->kqpgh", c, eye).reshape(kt, gt * p, gt * hg)

    bmat = jnp.concatenate([bd_in(bbr), bd_in(bbi)], axis=2).astype(BF16)
    cmat = jnp.concatenate([bd_out(c_re), -bd_out(c_im)], axis=1).astype(BF16)
    return bmat, cmat


def _s5_perm(nsteps):
    r = np.arange(nsteps * SUBLANES)
    p = np.zeros((r.size, r.size), np.float32)
    p[(r % nsteps) * SUBLANES + r // nsteps, r] = 1.0
    return jnp.asarray(p, BF16), jnp.asarray(p.T, BF16)


def _s5_kernel(h_ref, p_ref, pt_ref, b_ref, c_ref, lam_ref, d_ref, sre_ref, sim_ref, *rest, nsteps, chain):
    z_ref, ore_ref, oim_ref, bu_ref, st_ref = rest[-5:]
    rb = pl.program_id(1)
    nc = sre_ref.shape[1]

    @pl.when(rb == 0)
    def _():
        st_ref[0] = sre_ref[...]
        st_ref[1] = sim_ref[...]

    perm = p_ref[...]
    h1, h2, h3 = _split3(h_ref[...])
    u = _dot(perm, h1) + _dot(perm, h2) + _dot(perm, h3)
    bu_ref[...] = _dot(u.astype(BF16), b_ref[0])
    rowid = lax.broadcasted_iota(I32, (SUBLANES, S5_COLS), 0)

    for cb in range(nc // S5_COLS):
        c_re = slice(cb * S5_COLS, (cb + 1) * S5_COLS)
        c_im = slice(nc + cb * S5_COLS, nc + (cb + 1) * S5_COLS)
        ar = jnp.broadcast_to(lam_ref[0, 0:1, c_re], (SUBLANES, S5_COLS))
        ai = jnp.broadcast_to(lam_ref[0, 1:2, c_re], (SUBLANES, S5_COLS))

        def step(s, carry, store):
            xr, xi = carry
            r0 = pl.multiple_of(s * SUBLANES, SUBLANES)
            br = bu_ref[pl.ds(r0, SUBLANES), c_re]
            bi = bu_ref[pl.ds(r0, SUBLANES), c_im]
            nxr = ar * xr - ai * xi + br
            nxi = ar * xi + ai * xr + bi
            if store:
                bu_ref[pl.ds(r0, SUBLANES), c_re] = nxr
                bu_ref[pl.ds(r0, SUBLANES), c_im] = nxi
            return nxr, nxi

        if chain:
            zero = jnp.zeros((SUBLANES, S5_COLS), F32)
            er, ei = lax.fori_loop(0, nsteps, functools.partial(step, store=False), (zero, zero))
            asr = lam_ref[0, 2:3, c_re]
            asi = lam_ref[0, 3:4, c_re]
            pr = st_ref[0, 0:1, c_re]
            pi = st_ref[1, 0:1, c_re]
            x0r, x0i = zero, zero
            for j in range(SUBLANES):
                x0r = jnp.where(rowid == j, jnp.broadcast_to(pr, (SUBLANES, S5_COLS)), x0r)
                x0i = jnp.where(rowid == j, jnp.broadcast_to(pi, (SUBLANES, S5_COLS)), x0i)
                nr = asr * pr - asi * pi + er[j:j + 1]
                ni = asr * pi + asi * pr + ei[j:j + 1]
                pr, pi = nr, ni
            st_ref[0, :, c_re] = jnp.broadcast_to(pr, (SUBLANES, S5_COLS))
            st_ref[1, :, c_re] = jnp.broadcast_to(pi, (SUBLANES, S5_COLS))
            lax.fori_loop(0, nsteps, functools.partial(step, store=True), (x0r, x0i))
        else:
            fr, fi = lax.fori_loop(0, nsteps, functools.partial(step, store=True),
                                   (st_ref[0, :, c_re], st_ref[1, :, c_re]))
            st_ref[0, :, c_re] = fr
            st_ref[1, :, c_re] = fi

    y = _dot(bu_ref[...].astype(BF16), c_ref[0]) + d_ref[...] * u
    z = _gelu_tanh(y).astype(BF16)
    z_ref[...] = _dot(pt_ref[...], z).astype(z_ref.dtype)

    @pl.when(rb == pl.num_programs(1) - 1)
    def _():
        ore_ref[...] = st_ref[0]
        oim_ref[...] = st_ref[1]


def _s5_call(row0, rows, t_rows, h_all, bmat, cmat, lam, d_skip, s_re, s_im, nsteps, chain, prev):
    d = h_all.shape[1]
    kt = bmat.shape[0]
    nc = bmat.shape[2] // 2
    rblk = nsteps * SUBLANES
    blk0 = row0 // rblk
    perm, perm_t = _s5_perm(nsteps)
    fixed = lambda k, r: (0, 0)
    in_specs = [pl.BlockSpec((rblk, S5_KTILE), lambda k, r: (blk0 + r, k)),
                pl.BlockSpec((rblk, rblk), fixed),
                pl.BlockSpec((rblk, rblk), fixed),
                pl.BlockSpec((1, S5_KTILE, 2 * nc), lambda k, r: (k, 0, 0)),
                pl.BlockSpec((1, 2 * nc, S5_KTILE), lambda k, r: (k, 0, 0)),
                pl.BlockSpec((1, SUBLANES, nc), lambda k, r: (k, 0, 0)),
                pl.BlockSpec((1, S5_KTILE), lambda k, r: (0, k)),
                pl.BlockSpec((SUBLANES, nc), lambda k, r: (0, k)),
                pl.BlockSpec((SUBLANES, nc), lambda k, r: (0, k))]
    args = [h_all, perm, perm_t, bmat, cmat, lam, d_skip, s_re, s_im]
    aliases = {}
    if prev is not None:
        aliases[len(args)] = 0
        in_specs.append(pl.BlockSpec(memory_space=pl.ANY))
        args.append(prev)
    return pl.pallas_call(
        functools.partial(_s5_kernel, nsteps=nsteps, chain=chain),
        grid=(kt, rows // rblk),
        in_specs=in_specs,
        out_specs=[pl.BlockSpec((rblk, S5_KTILE), lambda k, r: (blk0 + r, k)),
                   pl.BlockSpec((SUBLANES, nc), lambda k, r: (0, k)),
                   pl.BlockSpec((SUBLANES, nc), lambda k, r: (0, k))],
        out_shape=[jax.ShapeDtypeStruct((t_rows, d), BF16),
                   jax.ShapeDtypeStruct(s_re.shape, F32),
                   jax.ShapeDtypeStruct(s_im.shape, F32)],
        scratch_shapes=[pltpu.VMEM((rblk, 2 * nc), F32), pltpu.VMEM((2, SUBLANES, nc), F32)],
        input_output_aliases=aliases,
        compiler_params=_cparams(2),
        name="s5_scan",
    )(*args)


def _mm_kernel(*refs, mode, lrep, n_w, n_alias):
    lhs_ref = refs[0]
    w_refs = refs[1:1 + n_w]
    pos = 1 + n_w
    i = pl.program_id(1)
    wbf = refs[len(refs) - n_w:]
    o_ref = refs[len(refs) - n_w - 1]

    @pl.when(i == 0)
    def _():
        for w_ref, s_ref in zip(w_refs, wbf):
            s_ref[...] = w_ref[...].astype(BF16)

    lhs = lhs_ref[...]
    if mode == "glu":
        b_ref, zt_ref, x_ref, g_ref = refs[pos:pos + 4]
        t = _dot(lhs, wbf[0][...]) + b_ref[...]
        o = zt_ref[...].astype(F32) * _sigmoid(t)
        o_ref[...] = x_ref[...] + _expand_rows(g_ref[...], lrep) * o
    elif mode == "res":
        x_ref, g_ref = refs[pos:pos + 2]
        o_ref[...] = x_ref[...] + _expand_rows(g_ref[...], lrep) * _dot(lhs, wbf[0][...])
    elif mode == "plain":
        o_ref[...] = _dot(lhs, wbf[0][...]).astype(o_ref.dtype)
    elif mode == "swiglu":
        o_ref[...] = (_silu(_dot(lhs, wbf[0][...])) * _dot(lhs, wbf[1][...])).astype(o_ref.dtype)


def _mm_call(mode, reg, t_rows, lhs, w_list, w_colblk0, n_out, tn, out_dtype, extras, prev, name):
    k_dim = lhs.shape[1]
    nj = n_out // tn
    in_specs = [pl.BlockSpec((reg.tile, k_dim), lambda j, i: (reg.blk0 + i, 0))]
    args = [lhs]
    for w, c0 in zip(w_list, w_colblk0):
        in_specs.append(pl.BlockSpec((k_dim, tn), lambda j, i, c0=c0: (0, c0 + j)))
        args.append(w)
    for a, kind in extras:
        if kind == "col":
            in_specs.append(pl.BlockSpec((1, tn), lambda j, i: (0, j)))
        elif kind == "tile":
            in_specs.append(pl.BlockSpec((reg.tile, tn), lambda j, i: (reg.blk0 + i, j)))
        elif kind == "own":
            in_specs.append(pl.BlockSpec((reg.tile, tn), lambda j, i: (i, j)))
        else:
            in_specs.append(pl.BlockSpec((reg.nseq, tn), lambda j, i: (0, j)))
        args.append(a)
    aliases = {}
    if prev is not None:
        aliases[len(args)] = 0
        in_specs.append(pl.BlockSpec(memory_space=pl.ANY))
        args.append(prev)
    return pl.pallas_call(
        functools.partial(_mm_kernel, mode=mode, lrep=reg.lrep, n_w=len(w_list), n_alias=len(aliases)),
        grid=(nj, reg.ntiles),
        in_specs=in_specs,
        out_specs=pl.BlockSpec((reg.tile, tn), lambda j, i: (reg.blk0 + i, j)),
        out_shape=jax.ShapeDtypeStruct((t_rows, n_out), out_dtype),
        scratch_shapes=[pltpu.VMEM((k_dim, tn), BF16) for _ in w_list],
        input_output_aliases=aliases,
        compiler_params=_cparams(2),
        name=name,
    )(*args)


def _gla_consts(chunk):
    nlev = int(np.log2(chunk))
    assert 1 << nlev == chunk
    tri = np.tril(np.ones((chunk, chunk), np.float32))
    r = np.arange(chunk)
    wall, masks = [tri], []
    for l in range(nlev):
        w = chunk >> (l + 1)
        blk = r // (2 * w)
        second = (r & w) != 0
        wall.append(tri[blk * 2 * w + w - 1])
        masks.append(((blk[:, None] == blk[None, :]) & second[:, None] & (~second)[None, :]).astype(np.float32))
    masks.append(np.eye(chunk, dtype=np.float32))
    return jnp.asarray(np.concatenate(wall, 0), BF16), jnp.asarray(np.stack(masks, 0), F32)


def _gla_kernel(q_ref, k_ref, v_ref, gate_ref, lg_ref, s0_ref, wall_ref, mask_ref, gn_ref,
                *rest, chunk, nsub, nheads, dk, dv):
    o_ref, sout_ref, s_ref = rest[-3:]
    c = pl.program_id(1)

    @pl.when(c == 0)
    def _():
        s_ref[...] = s0_ref[0]

    for sc in range(nsub):
        _gla_one_chunk(slice(sc * chunk, (sc + 1) * chunk), q_ref, k_ref, v_ref, gate_ref, lg_ref,
                       wall_ref, mask_ref, gn_ref, o_ref, s_ref, chunk, nheads, dk, dv)

    @pl.when(c == pl.num_programs(1) - 1)
    def _():
        sout_ref[0] = s_ref[...]


def _gla_one_chunk(r, q_ref, k_ref, v_ref, gate_ref, lg_ref, wall_ref, mask_ref, gn_ref, o_ref, s_ref,
                   chunk, nheads, dk, dv):
    nlev = mask_ref.shape[0] - 1
    lg = lg_ref[r, :]
    p1, p2, p3 = _split3(lg)
    wall = wall_ref[...]
    bg = _dot(wall, p1) + _dot(wall, p2) + _dot(wall, p3)
    b = bg[0:chunk]
    q = q_ref[r, :] * (dk ** -0.5)
    k = k_ref[r, :]
    row = lax.broadcasted_iota(I32, q.shape, 0)
    qs, ks = [], []
    for l in range(nlev):
        w = chunk >> (l + 1)
        g = bg[(l + 1) * chunk:(l + 2) * chunk]
        second = (row & w) != 0
        e = jnp.exp(jnp.where(second, b - g, g - b))
        qk = jnp.where(second, q, k) * e
        qs.append(jnp.where(second, qk, 0.0).astype(BF16))
        ks.append(jnp.where(second, 0.0, qk).astype(BF16))
    qb = q.astype(BF16)
    kb = k.astype(BF16)
    q_in = (q * jnp.exp(b)).astype(BF16)
    k_dec = (k * jnp.exp(b[chunk - 1:chunk] - b)).astype(BF16)
    ones = jnp.ones((chunk, LANES), BF16)
    dcol_all = jnp.exp(_dot_t0(p1, ones) + _dot_t0(p2, ones) + _dot_t0(p3, ones))
    gn = gn_ref[...]
    for h in range(nheads):
        ck = slice(h * dk, (h + 1) * dk)
        cv = slice(h * dv, (h + 1) * dv)
        att = _dot_t1(qb[:, ck], kb[:, ck]) * mask_ref[nlev]
        for l in range(nlev):
            att = att + _dot_t1(qs[l][:, ck], ks[l][:, ck]) * mask_ref[l]
        vh = v_ref[r, cv].astype(BF16)
        s_h = s_ref[h]
        o = _dot(att.astype(BF16), vh) + _dot(q_in[:, ck], s_h.astype(BF16))
        dcol = dcol_all[h * dk:(h + 1) * dk, :]
        s_ref[h] = jnp.concatenate([dcol] * (dv // LANES), axis=1) * s_h + _dot_t0(k_dec[:, ck], vh)
        ms = jnp.mean(o * o, axis=-1, keepdims=True)
        on = o * lax.rsqrt(ms + EPS) * gn
        o_ref[r, cv] = (on * _silu(gate_ref[r, cv])).astype(o_ref.dtype)


def _gla_call(reg_row0, nseq, seqlen, chunk, t_rows, proj, lg, s0, g_norm, prev):
    nheads, dk, dv = s0.shape[1:]
    hk = nheads * dk
    d = nheads * dv
    nsub = GLA_SUBCHUNKS if (seqlen // chunk) % GLA_SUBCHUNKS == 0 else 1
    rows = chunk * nsub
    nch = seqlen // rows
    rb0 = reg_row0 // rows
    assert reg_row0 % rows == 0
    wall, masks = _gla_consts(chunk)
    rowblk = lambda b, c: rb0 + b * nch + c
    in_specs = [pl.BlockSpec((rows, hk), lambda b, c: (rowblk(b, c), 0)),
                pl.BlockSpec((rows, hk), lambda b, c: (rowblk(b, c), 1)),
                pl.BlockSpec((rows, d), lambda b, c: (rowblk(b, c), 1)),
                pl.BlockSpec((rows, d), lambda b, c: (rowblk(b, c), 2)),
                pl.BlockSpec((rows, hk), lambda b, c: (rowblk(b, c), 0)),
                pl.BlockSpec((1, nheads, dk, dv), lambda b, c: (b, 0, 0, 0)),
                pl.BlockSpec(wall.shape, lambda b, c: (0, 0)),
                pl.BlockSpec(masks.shape, lambda b, c: (0, 0, 0)),
                pl.BlockSpec((1, dv), lambda b, c: (0, 0))]
    args = [proj, proj, proj, proj, lg, s0, wall, masks, g_norm]
    aliases = {}
    if prev is not None:
        aliases[len(args)] = 0
        in_specs.append(pl.BlockSpec(memory_space=pl.ANY))
        args.append(prev)
    return pl.pallas_call(
        functools.partial(_gla_kernel, chunk=chunk, nsub=nsub, nheads=nheads, dk=dk, dv=dv),
        grid=(nseq, nch),
        in_specs=in_specs,
        out_specs=[pl.BlockSpec((rows, d), lambda b, c: (rowblk(b, c), 0)),
                   pl.BlockSpec((1, nheads, dk, dv), lambda b, c: (b, 0, 0, 0))],
        out_shape=[jax.ShapeDtypeStruct((t_rows, d), BF16),
                   jax.ShapeDtypeStruct(s0.shape, F32)],
        scratch_shapes=[pltpu.VMEM((nheads, dk, dv), F32)],
        input_output_aliases=aliases,
        compiler_params=_cparams(2),
        name="gla_chunk",
    )(*args)


def _dispatch_plan(idx, rank, counts, n_tokens):
    n_experts = counts.shape[0]
    rb = EXP_ROWBLK
    a = n_tokens * TOP_K
    blocks_e = (counts + rb - 1) // rb
    padded = blocks_e * rb
    pad_start = jnp.cumsum(padded) - padded
    hit = idx[:, :, None] == jnp.arange(n_experts, dtype=I32)[None, None, :]
    dest = jnp.sum(jnp.where(hit, pad_start[None, None, :], 0), axis=-1) + rank
    np_rows = (a + rb - 1) // rb * rb + rb * n_experts
    items_e = (blocks_e + EXP_NBLK - 1) // EXP_NBLK
    item_end = jnp.cumsum(items_e)
    item_start = item_end - items_e
    n_items = n_experts + (np_rows // rb) // EXP_NBLK
    ii = jnp.arange(n_items, dtype=I32)
    total = item_end[-1]
    e_of = jnp.minimum(jnp.sum(item_end[None, :] <= ii[:, None], axis=1), n_experts - 1).astype(I32)
    valid = ii < total
    local = ii - item_start[e_of]
    e_last = e_of[jnp.maximum(total - 1, 0)]
    ie = jnp.where(valid, e_of, e_last).astype(I32)
    rsb = jnp.where(valid, pad_start[e_of] // rb + local * EXP_NBLK, 0).astype(I32)
    nrb = jnp.where(valid, jnp.minimum(EXP_NBLK, blocks_e[e_of] - local * EXP_NBLK), 0).astype(I32)
    return ie, rsb, nrb, pad_start.astype(I32), dest.astype(I32), np_rows, n_items


def _dispatch_kernel(cnt_ref, pst_ref, dest_hbm, h_ref, xs_hbm, idx_ref, hbuf, zrow, sem_i, sem_d, sem_z,
                     *, ntiles, n_experts):
    i = pl.program_id(0)
    slot = i % 2
    tile = h_ref.shape[0]
    nidx = tile * TOP_K
    rb = EXP_ROWBLK

    def idx_copy(t, s):
        return pltpu.make_async_copy(dest_hbm.at[pl.ds(pl.multiple_of(t * nidx, nidx), nidx)],
                                     idx_ref.at[pl.ds(pl.multiple_of(s * nidx, nidx), nidx)], sem_i.at[s])

    def wait_rows(s):
        for _ in range(TOP_K):
            pltpu.make_async_copy(hbuf.at[s], xs_hbm.at[pl.ds(0, tile)], sem_d.at[s]).wait()

    @pl.when(i == 0)
    def _():
        idx_copy(0, 0).start()

    idx_copy(i, slot).wait()

    @pl.when(i + 1 < ntiles)
    def _():
        idx_copy(jnp.minimum(i + 1, ntiles - 1), 1 - slot).start()

    hbuf[slot] = h_ref[...]

    def tok(t, carry):
        base = slot * nidx + t * TOP_K
        for j in range(TOP_K):
            p = idx_ref[base + j]
            pltpu.make_async_copy(hbuf.at[slot, pl.ds(t, 1)], xs_hbm.at[pl.ds(p, 1)], sem_d.at[slot]).start()
        return carry
    lax.fori_loop(0, tile, tok, 0)

    @pl.when(i > 0)
    def _():
        wait_rows(1 - slot)

    @pl.when(i == ntiles - 1)
    def _():
        wait_rows(slot)
        zrow[...] = jnp.zeros(zrow.shape, zrow.dtype)

        def expert(e, carry):
            cnt = cnt_ref[e]
            base = pst_ref[e]
            end = (cnt + rb - 1) // rb * rb

            def zstart(r, c2):
                pltpu.make_async_copy(zrow.at[pl.ds(0, 1)], xs_hbm.at[pl.ds(base + r, 1)], sem_z).start()
                return c2

            def zwait(r, c2):
                pltpu.make_async_copy(zrow.at[pl.ds(0, 1)], xs_hbm.at[pl.ds(0, 1)], sem_z).wait()
                return c2
            lax.fori_loop(cnt, end, zstart, 0)
            lax.fori_loop(cnt, end, zwait, 0)
            return carry
        lax.fori_loop(0, n_experts, expert, 0)


def _dispatch_call(counts, pad_start, dest_flat, h_packed, np_rows, tile):
    t_rows, dh = h_packed.shape
    ntiles = t_rows // tile
    n_experts = counts.shape[0]
    grid_spec = pltpu.PrefetchScalarGridSpec(
        num_scalar_prefetch=2,
        grid=(ntiles,),
        in_specs=[pl.BlockSpec(memory_space=pl.ANY),
                  pl.BlockSpec((tile, dh), lambda i, c, p: (i, 0))],
        out_specs=pl.BlockSpec(memory_space=pl.ANY),
        scratch_shapes=[pltpu.SMEM((2 * tile * TOP_K,), I32),
                        pltpu.VMEM((2, tile, dh), jnp.uint32),
                        pltpu.VMEM((SUBLANES, dh), jnp.uint32),
                        pltpu.SemaphoreType.DMA((2,)),
                        pltpu.SemaphoreType.DMA((2,)),
                        pltpu.SemaphoreType.DMA])
    return pl.pallas_call(
        functools.partial(_dispatch_kernel, ntiles=ntiles, n_experts=n_experts),
        grid_spec=grid_spec,
        out_shape=jax.ShapeDtypeStruct((np_rows, dh), jnp.uint32),
        compiler_params=_cparams(1),
        name="moe_dispatch",
    )(counts, pad_start, dest_flat, h_packed)


def _experts_kernel(ie_ref, rsb_ref, nrb_ref, xs_hbm, wi_ref, wo_ref, y_hbm,
                    xt, acc, act, ybuf, wibf, wobf, sem_x, sem_y, ycnt_ref, *, n_items, n_k, n_b):
    i = pl.program_id(0)
    st = pl.program_id(1)
    nrb = nrb_ref[i]
    rb = EXP_ROWBLK
    tkw = xt.shape[2]
    tdw = ybuf.shape[2]
    f = act.shape[1]

    def for_chunks(nblocks, fn):
        b0 = 0
        for size in EXP_CHUNKS:
            cnt = (nblocks - b0) // size

            def body(ci, carry, size=size, b0=b0):
                fn(b0 + ci * size, size)
                return carry
            lax.fori_loop(0, cnt, body, 0)
            b0 = b0 + cnt * size

    def rows_of(b0, nblk):
        return pl.ds(pl.multiple_of(b0 * rb, rb), nblk * rb)

    def x_copy(item, kt, b0, nblk):
        src = pl.multiple_of((rsb_ref[item] + b0) * rb, rb)
        return pltpu.make_async_copy(
            xs_hbm.at[pl.ds(src, nblk * rb), pl.ds(pl.multiple_of(kt * tkw, tkw), tkw)],
            xt.at[kt % 2, rows_of(b0, nblk)], sem_x.at[kt % 2])

    def start_x(item, kt):
        for_chunks(nrb_ref[item], lambda b0, nblk: x_copy(item, kt, b0, nblk).start())

    def wait_x(item, kt):
        for_chunks(nrb_ref[item], lambda b0, nblk: x_copy(item, kt, b0, nblk).wait())

    def wait_out(s):
        for_chunks(ycnt_ref[s], lambda b0, nblk: pltpu.make_async_copy(
            ybuf.at[s, pl.ds(0, nblk * rb)], y_hbm.at[pl.ds(0, nblk * rb), pl.ds(0, tdw)], sem_y.at[s]).wait())
        ycnt_ref[s] = 0

    @pl.when((i == 0) & (st == 0))
    def _():
        ycnt_ref[0] = 0
        ycnt_ref[1] = 0
        xt[...] = jnp.zeros(xt.shape, xt.dtype)
        start_x(0, 0)

    @pl.when(st < n_k)
    def _():
        wait_x(i, st)

    @pl.when(st + 1 < n_k)
    def _():
        start_x(i, st + 1)

    @pl.when((st == n_k) & (i + 1 < n_items))
    def _():
        start_x(jnp.minimum(i + 1, n_items - 1), 0)

    @pl.when((st < n_k) & (nrb > 0))
    def _():
        wibf[...] = wi_ref[0, 0].astype(BF16)
        slot = st % 2

        def partial_sum(rows):
            return _dot(_unpack_pairs(xt[slot, rows, :], BF16), wibf[...])

        if n_k > 1:
            @pl.when(st == 0)
            def _():
                def first(b0, nblk):
                    rows = rows_of(b0, nblk)
                    acc[rows, :] = partial_sum(rows)
                for_chunks(nrb, first)

            @pl.when((st > 0) & (st < n_k - 1))
            def _():
                def middle(b0, nblk):
                    rows = rows_of(b0, nblk)
                    acc[rows, :] = acc[rows, :] + partial_sum(rows)
                for_chunks(nrb, middle)

        @pl.when(st == n_k - 1)
        def _():
            def last(b0, nblk):
                rows = rows_of(b0, nblk)
                a = partial_sum(rows)
                if n_k > 1:
                    a = a + acc[rows, :]
                act[rows, :] = (_silu(a[:, :f]) * a[:, f:]).astype(BF16)
            for_chunks(nrb, last)

    @pl.when((st >= n_k) & (nrb > 0))
    def _():
        dj = st - n_k
        ys = (i * n_b + dj) % 2
        wobf[...] = wo_ref[0, 0].astype(BF16)
        wait_out(ys)
        row0 = rsb_ref[i] * rb
        col0 = pl.multiple_of(dj * tdw, tdw)

        def chunk(b0, nblk):
            rows = rows_of(b0, nblk)
            ybuf[ys, rows, :] = _pack_pairs(_dot(act[rows, :], wobf[...]))
            pltpu.make_async_copy(
                ybuf.at[ys, rows],
                y_hbm.at[pl.ds(pl.multiple_of(row0 + b0 * rb, rb), nblk * rb), pl.ds(col0, tdw)],
                sem_y.at[ys]).start()
        for_chunks(nrb, chunk)
        ycnt_ref[ys] = nrb

    @pl.when((i == n_items - 1) & (st == n_k + n_b - 1))
    def _():
        wait_out(0)
        wait_out(1)


def _experts_call(layer, ie, rsb, nrb, n_items, xs, w_in, w_out):
    _, _, d, f2 = w_in.shape
    f = f2 // 2
    np_rows, dh = xs.shape
    tk = min(EXP_TK, d)
    td = min(EXP_TD, d)
    n_k, n_b = d // tk, d // td
    rmax = EXP_NBLK * EXP_ROWBLK

    def kt(st, nr):
        return jnp.where((nr > 0) & (st < n_k), st, n_k - 1)

    def dj(st, nr):
        return jnp.where(nr > 0, jnp.maximum(st - n_k, 0), n_b - 1)

    grid_spec = pltpu.PrefetchScalarGridSpec(
        num_scalar_prefetch=3,
        grid=(n_items, n_k + n_b),
        in_specs=[pl.BlockSpec(memory_space=pl.ANY),
                  pl.BlockSpec((1, 1, tk, f2), lambda i, st, ie, rsb, nrb: (layer, ie[i], kt(st, nrb[i]), 0)),
                  pl.BlockSpec((1, 1, f, td), lambda i, st, ie, rsb, nrb: (layer, ie[i], 0, dj(st, nrb[i])))],
        out_specs=pl.BlockSpec(memory_space=pl.ANY),
        scratch_shapes=[pltpu.VMEM((2, rmax, tk // 2), jnp.uint32),
                        pltpu.VMEM((rmax, f2), F32),
                        pltpu.VMEM((rmax, f), BF16),
                        pltpu.VMEM((2, rmax, td // 2), jnp.uint32),
                        pltpu.VMEM((tk, f2), BF16),
                        pltpu.VMEM((f, td), BF16),
                        pltpu.SemaphoreType.DMA((2,)),
                        pltpu.SemaphoreType.DMA((2,)),
                        pltpu.SMEM((2,), I32)])
    return pl.pallas_call(
        functools.partial(_experts_kernel, n_items=n_items, n_k=n_k, n_b=n_b),
        grid_spec=grid_spec,
        out_shape=jax.ShapeDtypeStruct((np_rows, dh), jnp.uint32),
        compiler_params=_cparams(2),
        name="moe_experts",
    )(ie, rsb, nrb, xs, w_in, w_out)


def _combine_kernel(pos_hbm, w_ref, sh_ref, x_ref, g_ref, fg_ref, y_hbm, *rest,
                    lrep, blk0, ntiles, final):
    o_ref, idx_ref, gbuf, sem_i, sem_g = rest[-5:]
    i = pl.program_id(0)
    slot = i % 2
    nslot = 1 - slot
    tile = x_ref.shape[0]
    nidx = tile * TOP_K

    def idx_copy(t, s):
        return pltpu.make_async_copy(pos_hbm.at[pl.ds(pl.multiple_of((blk0 + t) * nidx, nidx), nidx)],
                                     idx_ref.at[pl.ds(pl.multiple_of(s * nidx, nidx), nidx)], sem_i.at[s])

    def issue_gather(s):
        def tok(t, carry):
            base = s * nidx + t * TOP_K
            for j in range(TOP_K):
                p = idx_ref[base + j]
                pltpu.make_async_copy(y_hbm.at[pl.ds(p, 1)], gbuf.at[s, j, pl.ds(t, 1)], sem_g.at[s]).start()
            return carry
        lax.fori_loop(0, tile, tok, 0)

    @pl.when(i == 0)
    def _():
        first = idx_copy(0, 0)
        first.start()
        first.wait()
        issue_gather(0)
        if ntiles > 1:
            idx_copy(1, 1).start()

    @pl.when(i + 1 < ntiles)
    def _():
        idx_copy(jnp.minimum(i + 1, ntiles - 1), nslot).wait()

    for j in range(TOP_K):
        pltpu.make_async_copy(y_hbm.at[pl.ds(0, tile)], gbuf.at[slot, j], sem_g.at[slot]).wait()

    @pl.when(i + 1 < ntiles)
    def _():
        issue_gather(nslot)

    @pl.when(i + 2 < ntiles)
    def _():
        idx_copy(jnp.minimum(i + 2, ntiles - 1), slot).start()

    acc = sh_ref[...]
    for j in range(TOP_K):
        acc = acc + w_ref[:, j:j + 1] * _unpack_pairs(gbuf[slot, j], F32)
    o = x_ref[...] + _expand_rows(g_ref[...], lrep, tile, i * tile) * acc
    if final:
        ms = jnp.mean(o * o, axis=-1, keepdims=True)
        o = o * lax.rsqrt(ms + EPS) * fg_ref[...]
    o_ref[...] = o


def _combine_call(reg, t_rows, pos_flat, wts, shared, x_all, g, final_g, y_sorted, final, prev):
    d = x_all.shape[1]
    row = lambda i: (reg.blk0 + i, 0)
    fixed = lambda i: (0, 0)
    in_specs = [pl.BlockSpec(memory_space=pl.ANY),
                pl.BlockSpec((reg.tile, TOP_K), row),
                pl.BlockSpec((reg.tile, d), row),
                pl.BlockSpec((reg.tile, d), row),
                pl.BlockSpec((reg.nseq, d), fixed),
                pl.BlockSpec((1, d), fixed),
                pl.BlockSpec(memory_space=pl.ANY)]
    args = [pos_flat, wts, shared, x_all, g, final_g, y_sorted]
    aliases = {}
    if prev is not None:
        aliases[len(args)] = 0
        in_specs.append(pl.BlockSpec(memory_space=pl.ANY))
        args.append(prev)
    if final:
        out_spec = pl.BlockSpec((reg.tile, d), lambda i: (i, 0))
        out_rows = reg.rows
    else:
        out_spec = pl.BlockSpec((reg.tile, d), row)
        out_rows = t_rows
    return pl.pallas_call(
        functools.partial(_combine_kernel, lrep=reg.lrep, blk0=reg.blk0, ntiles=reg.ntiles, final=final),
        grid=(reg.ntiles,),
        in_specs=in_specs,
        out_specs=out_spec,
        out_shape=jax.ShapeDtypeStruct((out_rows, d), F32),
        scratch_shapes=[pltpu.SMEM((2 * reg.tile * TOP_K,), I32),
                        pltpu.VMEM((2, TOP_K, reg.tile, d // 2), jnp.uint32),
                        pltpu.SemaphoreType.DMA((2,)),
                        pltpu.SemaphoreType.DMA((2,))],
        input_output_aliases=aliases,
        compiler_params=_cparams(1),
        name="moe_combine",
    )(*args)


def _both(fn, regs):
    out = fn(regs[0], None)
    return fn(regs[1], out)


def kernel(x_prompt, x_sample, state_s5_re, state_s5_im, state_gla, c_prompt, c_sample, w_ada, b_ada, norm_g, s5_lam_re, s5_lam_im, s5_log_dt, s5_b_re, s5_b_im, s5_c_re, s5_c_im, s5_d, s5_w_glu, s5_b_glu, gla_w_in, gla_w_a1, gla_w_a2, gla_b_a, gla_g_norm, gla_w_o, moe_w_router, moe_b_router, moe_w_in, moe_w_out, moe_ws_in, moe_ws_out, final_g):
    bp, lp, d = x_prompt.shape
    bs, ls, _ = x_sample.shape
    assert bp == 1 and bs == SUBLANES and ls % SUBLANES == 0
    depth = w_ada.shape[0]
    n_exp = moe_w_in.shape[1]
    f_sh = moe_ws_out.shape[1]
    srows = bs * ls
    t_rows = lp + srows
    s5_steps = 64
    s5_blk = s5_steps * SUBLANES
    gla_chunk = 64
    assert lp % PROMPT_TILE == 0 and lp % s5_blk == 0 and lp % srows == 0

    mm_tile = MM_TILE if lp % MM_TILE == 0 else PROMPT_TILE
    regs_mm = (_Region(0, lp, mm_tile, 1, mm_tile), _Region(lp, srows, srows, bs, ls))
    regs_nm = (_Region(0, lp, PROMPT_TILE // 2, 1, PROMPT_TILE // 2), _Region(lp, srows, srows, bs, ls))
    regs_cb = (_Region(0, lp, COMB_TILE, 1, COMB_TILE), _Region(lp, srows, COMB_TILE, bs, ls))

    x_parts = (x_prompt.reshape(lp, d), x_sample.reshape(srows, d))
    x_all = None

    def x_of(r):
        return x_parts[r.nseq > 1] if x_parts is not None else x_all

    c_all = jnp.concatenate([c_prompt, c_sample, jnp.zeros((2 * SUBLANES - 1 - bs, d), F32)], axis=0)
    mod = _ada_call(c_all, w_ada, b_ada)

    def mods(layer, k):
        m = mod[layer, :, k * d:(k + 1) * d]
        return (m[0:1], m[1:1 + bs])

    new_re_p, new_im_p, new_gla_p, new_re_s, new_im_s, new_gla_s = [], [], [], [], [], []
    for i in range(depth):
        sh1, sc1, g1, sh2, sc2, g2 = [mods(i, k) for k in range(6)]
        j = i // 2
        ng1 = norm_g[i, 0].reshape(1, d)
        ng2 = norm_g[i, 1].reshape(1, d)
        if i % 2 == 0:
            own = x_parts is not None
            (h_all,) = _both(lambda r, prev: _norm_call(
                "plain", r, t_rows, x_of(r), ng1, sc1[r.nseq > 1], sh1[r.nseq > 1], [], [(d, F32)], prev,
                x_own=own), regs_nm)
            grp, pst = s5_lam_re.shape[1:]
            ncol = grp * pst
            lam, bbr, bbi = _s5_disc_call(s5_lam_re[j], s5_lam_im[j], s5_log_dt[j], s5_b_re[j], s5_b_im[j], s5_steps)
            bmat, cmat = _s5_block_weights(bbr, bbi, s5_c_re[j], s5_c_im[j])
            kt = bmat.shape[0]
            lamt = jnp.transpose(lam.reshape(4, kt, ncol // kt), (1, 0, 2))
            lamt = jnp.concatenate([lamt, jnp.zeros_like(lamt)], axis=1)
            dsk = s5_d[j].reshape(1, d)
            zeros_st = jnp.zeros((SUBLANES, ncol), F32)
            z_all, pre, pim = _s5_call(0, lp, t_rows, h_all, bmat, cmat, lamt, dsk, zeros_st, zeros_st,
                                       s5_steps, True, None)
            z_all, sre, sim = _s5_call(lp, srows, t_rows, h_all, bmat, cmat, lamt, dsk,
                                       state_s5_re[j].reshape(bs, ncol), state_s5_im[j].reshape(bs, ncol),
                                       ls, False, z_all)
            new_re_p.append(pre[0].reshape(1, grp, pst))
            new_im_p.append(pim[0].reshape(1, grp, pst))
            new_re_s.append(sre.reshape(bs, grp, pst))
            new_im_s.append(sim.reshape(bs, grp, pst))
            bglu = s5_b_glu[j].reshape(1, d)
            x_all = _both(lambda r, prev: _mm_call(
                "glu", r, t_rows, z_all, [s5_w_glu[j]], [0], d, 512, F32,
                [(bglu, "col"), (z_all, "tile"), (x_of(r), "own" if own else "tile"), (g1[r.nseq > 1], "seq")],
                prev, "s5_glu"), regs_mm)
            x_parts = None
        else:
            hk = gla_w_a2.shape[2]
            assert 2 * hk == d
            own = x_parts is not None
            h_bf, lg = _both(lambda r, prev: _norm_call(
                "gate", r, t_rows, x_of(r), ng1, sc1[r.nseq > 1], sh1[r.nseq > 1],
                [gla_w_a1[j], gla_w_a2[j], gla_b_a[j].reshape(1, hk)], [(d, BF16), (hk, F32)], prev,
                x_own=own), regs_nm)
            proj = _both(lambda r, prev: _mm_call(
                "plain", r, t_rows, h_bf, [gla_w_in[j]], [0], 3 * d, 512, F32, [], prev, "gla_proj"), regs_mm)
            gn = gla_g_norm[j].reshape(1, -1)
            s0p = jnp.zeros((1,) + state_gla.shape[2:], F32)
            o_all, gla_p = _gla_call(0, 1, lp, gla_chunk, t_rows, proj, lg, s0p, gn, None)
            o_all, gla_s = _gla_call(lp, bs, ls, ls, t_rows, proj, lg, state_gla[j], gn, o_all)
            new_gla_p.append(gla_p)
            new_gla_s.append(gla_s)
            x_all = _both(lambda r, prev: _mm_call(
                "res", r, t_rows, o_all, [gla_w_o[j]], [0], d, 512, F32,
                [(x_of(r), "own" if own else "tile"), (g1[r.nseq > 1], "seq")], prev, "gla_out"), regs_mm)
            x_parts = None

        brt = moe_b_router[i].reshape(1, n_exp)
        r_p = _router_call(regs_nm[0], t_rows, x_all, ng2, sc2[0], sh2[0], moe_w_router[i], brt,
                           jnp.zeros((SUBLANES, n_exp), F32), None)
        h_pk, h_bf, idx, wts, rank, cnt = _router_call(
            regs_nm[1], t_rows, x_all, ng2, sc2[1], sh2[1], moe_w_router[i], brt, r_p[5], r_p[:5])
        counts = cnt[0].astype(I32)
        ie, rsb, nrb, pad_start, dest, np_rows, n_items = _dispatch_plan(idx, rank, counts, t_rows)
        pos_flat = dest.reshape(-1)
        xs = _dispatch_call(counts, pad_start, pos_flat, h_pk, np_rows, DISP_TILE)
        y_sorted = _experts_call(i, ie, rsb, nrb, n_items, xs, moe_w_in, moe_w_out)
        act = _both(lambda r, prev: _mm_call(
            "swiglu", r, t_rows, h_bf, [moe_ws_in[i], moe_ws_in[i]], [0, f_sh // 256], f_sh, 256, BF16,
            [], prev, "shared_in"), regs_mm)
        shared = _both(lambda r, prev: _mm_call(
            "plain", r, t_rows, act, [moe_ws_out[i]], [0], d, 512, F32, [], prev, "shared_out"), regs_mm)
        last = i == depth - 1
        fg = final_g.reshape(1, d)
        if last:
            y_prompt, y_sample = [_combine_call(r, t_rows, pos_flat, wts, shared, x_all, g2[r.nseq > 1], fg,
                                                y_sorted, True, None) for r in regs_cb]
        else:
            x_all = _both(lambda r, prev: _combine_call(
                r, t_rows, pos_flat, wts, shared, x_all, g2[r.nseq > 1], fg, y_sorted, False, prev), regs_cb)

    y_prompt = y_prompt.reshape(bp, lp, d)
    y_sample = y_sample.reshape(bs, ls, d)
    return (y_prompt, y_sample, jnp.stack(new_re_p), jnp.stack(new_im_p), jnp.stack(new_gla_p),
            jnp.stack(new_re_s), jnp.stack(new_im_s), jnp.stack(new_gla_s))
```

```python
import functools

import numpy as np
import jax
import jax.numpy as jnp
from jax import lax
from jax.experimental import pallas as pl
from jax.experimental.pallas import tpu as pltpu

F32 = jnp.float32
BF16 = jnp.bfloat16
I32 = jnp.int32

EPS = 1e-6
GLA_TAU = 16.0
TOP_K = 8
N_GROUPS = 8
TOPK_GROUPS = 4
ROUTED_SCALE = 2.5

LANES = 128
SUBLANES = 8
VMEM_LIMIT = 56 << 20

PROMPT_TILE = 512
MM_TILE = 1024
S5_KTILE = 256
S5_COLS = 512
EXP_ROWBLK = 128
EXP_NBLK = 18
EXP_CHUNKS = (4, 2, 1)
EXP_TK = 1024
EXP_TD = 1024
PACK_GROUP = 512
DISP_TILE = 128
COMB_TILE = 128


def _cparams(n_axes, vmem=VMEM_LIMIT):
    return pltpu.CompilerParams(dimension_semantics=("arbitrary",) * n_axes, vmem_limit_bytes=vmem)


class _Region:
    def __init__(self, row0, rows, tile, nseq, lrep):
        self.row0, self.rows, self.tile, self.nseq, self.lrep = row0, rows, tile, nseq, lrep
        self.blk0 = row0 // tile
        self.ntiles = rows // tile
        assert row0 % tile == 0 and rows % tile == 0


def _expand_rows(m, lrep, rows=None, row0=0):
    nseq, n = m.shape
    if nseq == 1:
        return m
    rows = nseq * lrep if rows is None else rows
    r = lax.broadcasted_iota(I32, (rows, n), 0) + row0
    out = jnp.broadcast_to(m[0:1, :], (rows, n))
    for b in range(1, nseq):
        out = jnp.where(r >= b * lrep, jnp.broadcast_to(m[b:b + 1, :], (rows, n)), out)
    return out


def _sigmoid(x):
    return 1.0 / (1.0 + jnp.exp(-x))


def _silu(x):
    return x * _sigmoid(x)


def _gelu_tanh(x):
    return 0.5 * x * (1.0 + jnp.tanh(0.7978845608028654 * (x + 0.044715 * x * x * x)))


def _log_sigmoid(x):
    return jnp.minimum(x, 0.0) - jnp.log1p(jnp.exp(-jnp.abs(x)))


def _split3(x):
    p1 = x.astype(BF16)
    r1 = x - p1.astype(F32)
    p2 = r1.astype(BF16)
    r2 = r1 - p2.astype(F32)
    return p1, p2, r2.astype(BF16)


def _pack_pairs(x):
    half = PACK_GROUP // 2
    words = []
    for g in range(x.shape[1] // PACK_GROUP):
        lo = pltpu.bitcast(x[:, g * PACK_GROUP:g * PACK_GROUP + half].astype(BF16).astype(F32), jnp.uint32)
        hi = pltpu.bitcast(x[:, g * PACK_GROUP + half:(g + 1) * PACK_GROUP].astype(BF16).astype(F32), jnp.uint32)
        words.append(hi | (lo >> 16))
    return words[0] if len(words) == 1 else jnp.concatenate(words, axis=1)


def _unpack_pairs(w, dtype):
    half = PACK_GROUP // 2
    parts = []
    for g in range(w.shape[1] // half):
        ww = w[:, g * half:(g + 1) * half]
        parts.append(pltpu.bitcast(ww << 16, F32).astype(dtype))
        parts.append(pltpu.bitcast(ww & jnp.uint32(0xFFFF0000), F32).astype(dtype))
    return jnp.concatenate(parts, axis=1)


def _dot(a, b):
    return jnp.dot(a, b, preferred_element_type=F32)


def _dot_t0(a, b):
    return lax.dot_general(a, b, (((0,), (0,)), ((), ())), preferred_element_type=F32)


def _dot_t1(a, b):
    return lax.dot_general(a, b, (((1,), (1,)), ((), ())), preferred_element_type=F32)


def _ada_kernel(c_ref, w_ref, b_ref, o_ref):
    s = _silu(c_ref[...])
    o_ref[0] = _dot(s.astype(BF16), w_ref[0].astype(BF16)) + b_ref[0]


def _ada_call(c_all, w_ada, b_ada):
    depth, d, n6 = w_ada.shape
    nc = c_all.shape[0]
    tn = 512
    return pl.pallas_call(
        _ada_kernel,
        grid=(depth, n6 // tn),
        in_specs=[pl.BlockSpec((nc, d), lambda l, j: (0, 0)),
                  pl.BlockSpec((1, d, tn), lambda l, j: (l, 0, j)),
                  pl.BlockSpec((1, 1, tn), lambda l, j: (l, 0, j))],
        out_specs=pl.BlockSpec((1, nc, tn), lambda l, j: (l, 0, j)),
        out_shape=jax.ShapeDtypeStruct((depth, nc, n6), F32),
        compiler_params=_cparams(2),
        name="adaln",
    )(c_all, w_ada, b_ada.reshape(depth, 1, n6))


def _norm_mod(x, g, sc, sh, lrep):
    ms = jnp.mean(x * x, axis=-1, keepdims=True)
    y = x * lax.rsqrt(ms + EPS) * g
    return y * (1.0 + _expand_rows(sc, lrep)) + _expand_rows(sh, lrep)


def _norm_kernel(x_ref, g_ref, sc_ref, sh_ref, *rest, lrep):
    o_ref = rest[-1]
    h = _norm_mod(x_ref[...], g_ref[...], sc_ref[...], sh_ref[...], lrep)
    o_ref[...] = h.astype(o_ref.dtype)


def _cols_to_lanes(cols, dtype):
    r = cols[0].shape[0]
    lane = lax.broadcasted_iota(I32, (r, len(cols)), 1)
    out = jnp.zeros((r, len(cols)), dtype)
    for j, c in enumerate(cols):
        out = jnp.where(lane == j, c.astype(dtype), out)
    return out


def _router_kernel(x_ref, g_ref, sc_ref, sh_ref, w_ref, b_ref, cin_ref, *rest, lrep):
    hp_ref, hb_ref, idx_ref, wt_ref, rk_ref, cout_ref, run_ref = rest[-7:]
    i = pl.program_id(0)

    @pl.when(i == 0)
    def _():
        run_ref[...] = cin_ref[0:1, :]

    h = _norm_mod(x_ref[...], g_ref[...], sc_ref[...], sh_ref[...], lrep)
    hb_ref[...] = h.astype(BF16)
    hp_ref[...] = _pack_pairs(h)
    w = w_ref[...]
    hh = h.astype(BF16)
    hl = (h - hh.astype(F32)).astype(BF16)
    wh = w.astype(BF16)
    wl = (w - wh.astype(F32)).astype(BF16)
    scores = _sigmoid(_dot(hh, wh) + _dot(hl, wh) + _dot(hh, wl))

    rows, n_exp = scores.shape
    per = n_exp // N_GROUPS
    neg = -jnp.inf
    big = n_exp + 1
    lane_i = lax.broadcasted_iota(I32, (rows, n_exp), 1)
    lane = lane_i.astype(F32)
    grp_i = lane_i // per
    grp = grp_i.astype(F32)
    choice = scores + b_ref[...]
    gs = jnp.zeros((rows, n_exp), F32)
    for g in range(N_GROUPS):
        ing = grp_i == g
        m = jnp.where(ing, choice, neg)
        m1 = jnp.max(m, axis=-1, keepdims=True)
        top = m == m1
        ntop = jnp.sum(jnp.where(top, 1.0, 0.0), axis=-1, keepdims=True)
        below = jnp.max(jnp.where(top, neg, m), axis=-1, keepdims=True)
        m2 = jnp.where(ntop > 1.5, m1, below)
        gs = jnp.where(ing, m1 + m2, gs)
    masked = jnp.full((rows, n_exp), neg, F32)
    for _ in range(TOPK_GROUPS):
        mx = jnp.max(gs, axis=-1, keepdims=True)
        gi = jnp.min(jnp.where(gs == mx, grp, big), axis=-1, keepdims=True)
        sel = grp == gi
        masked = jnp.where(sel, choice, masked)
        gs = jnp.where(sel, neg, gs)
    idx_cols, w_cols, hits = [], [], []
    for _ in range(TOP_K):
        mx = jnp.max(masked, axis=-1, keepdims=True)
        ei = jnp.min(jnp.where(masked == mx, lane, big), axis=-1, keepdims=True)
        hit = lane == ei
        hits.append(hit)
        idx_cols.append(ei)
        w_cols.append(jnp.sum(jnp.where(hit, scores, 0.0), axis=-1, keepdims=True))
        masked = jnp.where(hit, neg, masked)
    wsum = w_cols[0]
    for c in w_cols[1:]:
        wsum = wsum + c
    scale = ROUTED_SCALE / wsum
    idx_ref[...] = _cols_to_lanes(idx_cols, I32)
    wt_ref[...] = _cols_to_lanes([c * scale for c in w_cols], F32)
    onehot = jnp.zeros((rows, n_exp), F32)
    for hit in hits:
        onehot = jnp.where(hit, 1.0, onehot)
    rr = lax.broadcasted_iota(I32, (rows, rows), 0)
    cc = lax.broadcasted_iota(I32, (rows, rows), 1)
    before = jnp.where(rr > cc, 1.0, 0.0).astype(BF16)
    cum = _dot(before, onehot.astype(BF16)) + run_ref[...]
    rk_ref[...] = _cols_to_lanes(
        [jnp.sum(jnp.where(hit, cum, 0.0), axis=-1, keepdims=True) for hit in hits], I32)
    run_ref[...] = run_ref[...] + jnp.sum(onehot, axis=0, keepdims=True)
    cout_ref[...] = jnp.broadcast_to(run_ref[...], cout_ref.shape)


def _router_call(reg, t_rows, x_all, g, sc, sh, w_router, b_router, cnt_in, prev):
    d = x_all.shape[1]
    n_exp = w_router.shape[1]
    row = lambda i: (reg.blk0 + i, 0)
    fixed = lambda i: (0, 0)
    in_specs = [pl.BlockSpec((reg.tile, d), row), pl.BlockSpec((1, d), fixed),
                pl.BlockSpec((reg.nseq, d), fixed), pl.BlockSpec((reg.nseq, d), fixed),
                pl.BlockSpec((d, n_exp), fixed), pl.BlockSpec((1, n_exp), fixed),
                pl.BlockSpec((SUBLANES, n_exp), fixed)]
    args = [x_all, g, sc, sh, w_router, b_router, cnt_in]
    outs = [(d // 2, jnp.uint32), (d, BF16), (TOP_K, I32), (TOP_K, F32), (TOP_K, I32)]
    aliases = {}
    if prev is not None:
        for k in range(len(outs)):
            aliases[len(args)] = k
            in_specs.append(pl.BlockSpec(memory_space=pl.ANY))
            args.append(prev[k])
    res = pl.pallas_call(
        functools.partial(_router_kernel, lrep=reg.lrep),
        grid=(reg.ntiles,),
        in_specs=in_specs,
        out_specs=[pl.BlockSpec((reg.tile, n), row) for n, _ in outs] + [pl.BlockSpec((SUBLANES, n_exp), fixed)],
        out_shape=[jax.ShapeDtypeStruct((t_rows, n), dt) for n, dt in outs]
        + [jax.ShapeDtypeStruct((SUBLANES, n_exp), F32)],
        scratch_shapes=[pltpu.VMEM((1, n_exp), F32)],
        input_output_aliases=aliases,
        compiler_params=_cparams(1),
        name="moe_router",
    )(*args)
    return list(res)


def _norm_gate_kernel(x_ref, g_ref, sc_ref, sh_ref, wa1_ref, wa2_ref, ba_ref, *rest, lrep):
    h_ref, lg_ref = rest[-2:]
    h = _norm_mod(x_ref[...], g_ref[...], sc_ref[...], sh_ref[...], lrep)
    hb = h.astype(BF16)
    h_ref[...] = hb
    a = _dot(hb, wa1_ref[...].astype(BF16))
    z = _dot(a.astype(BF16), wa2_ref[...].astype(BF16)) + ba_ref[...]
    lg_ref[...] = _log_sigmoid(z) * (1.0 / GLA_TAU)


def _norm_call(kind, reg, t_rows, x_all, g, sc, sh, extra, outs, prev, x_own=False):
    d = x_all.shape[1]
    kern = {"plain": _norm_kernel, "gate": _norm_gate_kernel}[kind]
    row = lambda i: (reg.blk0 + i, 0)
    xrow = (lambda i: (i, 0)) if x_own else row
    fixed = lambda i: (0, 0)
    in_specs = [pl.BlockSpec((reg.tile, d), xrow), pl.BlockSpec((1, d), fixed),
                pl.BlockSpec((reg.nseq, d), fixed), pl.BlockSpec((reg.nseq, d), fixed)]
    args = [x_all, g, sc, sh]
    for e in extra:
        in_specs.append(pl.BlockSpec(e.shape, fixed))
        args.append(e)
    aliases = {}
    if prev is not None:
        for k, p in enumerate(prev):
            aliases[len(args)] = k
            in_specs.append(pl.BlockSpec(memory_space=pl.ANY))
            args.append(p)
    res = pl.pallas_call(
        functools.partial(kern, lrep=reg.lrep),
        grid=(reg.ntiles,),
        in_specs=in_specs,
        out_specs=[pl.BlockSpec((reg.tile, n), row) for n, _ in outs],
        out_shape=[jax.ShapeDtypeStruct((t_rows, n), dt) for n, dt in outs],
        input_output_aliases=aliases,
        compiler_params=_cparams(1),
        name="norm_" + kind,
    )(*args)
    return list(res)


def _s5_disc_kernel(lr_ref, li_ref, ldt_ref, br_ref, bi_ref, lam_ref, bbr_ref, bbi_ref, *, nsteps):
    lr = lr_ref[...]
    li = li_ref[...]
    dt = jnp.exp(ldt_ref[...])
    mag = jnp.exp(lr * dt)
    ar = mag * jnp.cos(li * dt)
    ai = mag * jnp.sin(li * dt)
    den = lr * lr + li * li
    fr = ((ar - 1.0) * lr + ai * li) / den
    fi = (ai * lr - (ar - 1.0) * li) / den
    lam_ref[0] = ar
    lam_ref[1] = ai
    mags = jnp.exp(nsteps * (lr * dt))
    lam_ref[2] = mags * jnp.cos(nsteps * (li * dt))
    lam_ref[3] = mags * jnp.sin(nsteps * (li * dt))
    for h in range(br_ref.shape[0]):
        br = br_ref[h]
        bi = bi_ref[h]
        bbr_ref[h] = fr * br - fi * bi
        bbi_ref[h] = fr * bi + fi * br


def _s5_disc_call(lam_re, lam_im, log_dt, b_re, b_im, nsteps):
    g, p = lam_re.shape
    hg = b_re.shape[2]
    ldt = jnp.broadcast_to(log_dt[:, None], (g, p))
    brt = jnp.transpose(b_re, (2, 0, 1))
    bit = jnp.transpose(b_im, (2, 0, 1))
    return pl.pallas_call(
        functools.partial(_s5_disc_kernel, nsteps=float(nsteps)),
        out_shape=[jax.ShapeDtypeStruct((4, g, p), F32),
                   jax.ShapeDtypeStruct((hg, g, p), F32),
                   jax.ShapeDtypeStruct((hg, g, p), F32)],
        name="s5_disc",
    )(lam_re, lam_im, ldt, brt, bit)


def _s5_block_weights(bbr, bbi, c_re, c_im):
    hg, g, p = bbr.shape
    gt = S5_KTILE // hg
    kt = g // gt
    eye = jnp.eye(gt, dtype=F32)

    def bd_in(b):
        b = b.reshape(hg, kt, gt, p)
        return jnp.einsum("hkgp,gq->kghqp", b, eye).reshape(kt, gt * hg, gt * p)

    def bd_out(c):
        c = c.reshape(kt, gt, hg, p)
        return jnp.einsum("kghp,gq->kqpgh", c, eye).reshape(kt, gt * p, gt * hg)

    bmat = jnp.concatenate([bd_in(bbr), bd_in(bbi)], axis=2).astype(BF16)
    cmat = jnp.concatenate([bd_out(c_re), -bd_out(c_im)], axis=1).astype(BF16)
    return bmat, cmat


def _s5_perm(nsteps):
    r = np.arange(nsteps * SUBLANES)
    p = np.zeros((r.size, r.size), np.float32)
    p[(r % nsteps) * SUBLANES + r // nsteps, r] = 1.0
    return jnp.asarray(p, BF16), jnp.asarray(p.T, BF16)


def _s5_kernel(h_ref, p_ref, pt_ref, b_ref, c_ref, lam_ref, d_ref, sre_ref, sim_ref, *rest, nsteps, chain):
    z_ref, ore_ref, oim_ref, bu_ref, st_ref = rest[-5:]
    rb = pl.program_id(1)
    nc = sre_ref.shape[1]

    @pl.when(rb == 0)
    def _():
        st_ref[0] = sre_ref[...]
        st_ref[1] = sim_ref[...]

    perm = p_ref[...]
    h1, h2, h3 = _split3(h_ref[...])
    u = _dot(perm, h1) + _dot(perm, h2) + _dot(perm, h3)
    bu_ref[...] = _dot(u.astype(BF16), b_ref[0])
    rowid = lax.broadcasted_iota(I32, (SUBLANES, S5_COLS), 0)

    for cb in range(nc // S5_COLS):
        c_re = slice(cb * S5_COLS, (cb + 1) * S5_COLS)
        c_im = slice(nc + cb * S5_COLS, nc + (cb + 1) * S5_COLS)
        ar = jnp.broadcast_to(lam_ref[0, 0:1, c_re], (SUBLANES, S5_COLS))
        ai = jnp.broadcast_to(lam_ref[0, 1:2, c_re], (SUBLANES, S5_COLS))

        def step(s, carry, store):
            xr, xi = carry
            r0 = pl.multiple_of(s * SUBLANES, SUBLANES)
            br = bu_ref[pl.ds(r0, SUBLANES), c_re]
            bi = bu_ref[pl.ds(r0, SUBLANES), c_im]
            nxr = ar * xr - ai * xi + br
            nxi = ar * xi + ai * xr + bi
            if store:
                bu_ref[pl.ds(r0, SUBLANES), c_re] = nxr
                bu_ref[pl.ds(r0, SUBLANES), c_im] = nxi
            return nxr, nxi

        if chain:
            zero = jnp.zeros((SUBLANES, S5_COLS), F32)
            er, ei = lax.fori_loop(0, nsteps, functools.partial(step, store=False), (zero, zero))
            asr = lam_ref[0, 2:3, c_re]
            asi = lam_ref[0, 3:4, c_re]
            pr = st_ref[0, 0:1, c_re]
            pi = st_ref[1, 0:1, c_re]
            x0r, x0i = zero, zero
            for j in range(SUBLANES):
                x0r = jnp.where(rowid == j, jnp.broadcast_to(pr, (SUBLANES, S5_COLS)), x0r)
                x0i = jnp.where(rowid == j, jnp.broadcast_to(pi, (SUBLANES, S5_COLS)), x0i)
                nr = asr * pr - asi * pi + er[j:j + 1]
                ni = asr * pi + asi * pr + ei[j:j + 1]
                pr, pi = nr, ni
            st_ref[0, :, c_re] = jnp.broadcast_to(pr, (SUBLANES, S5_COLS))
            st_ref[1, :, c_re] = jnp.broadcast_to(pi, (SUBLANES, S5_COLS))
            lax.fori_loop(0, nsteps, functools.partial(step, store=True), (x0r, x0i))
        else:
            fr, fi = lax.fori_loop(0, nsteps, functools.partial(step, store=True),
                                   (st_ref[0, :, c_re], st_ref[1, :, c_re]))
            st_ref[0, :, c_re] = fr
            st_ref[1, :, c_re] = fi

    y = _dot(bu_ref[...].astype(BF16), c_ref[0]) + d_ref[...] * u
    z = _gelu_tanh(y).astype(BF16)
    z_ref[...] = _dot(pt_ref[...], z).astype(z_ref.dtype)

    @pl.when(rb == pl.num_programs(1) - 1)
    def _():
        ore_ref[...] = st_ref[0]
        oim_ref[...] = st_ref[1]


def _s5_call(row0, rows, t_rows, h_all, bmat, cmat, lam, d_skip, s_re, s_im, nsteps, chain, prev):
    d = h_all.shape[1]
    kt = bmat.shape[0]
    nc = bmat.shape[2] // 2
    rblk = nsteps * SUBLANES
    blk0 = row0 // rblk
    perm, perm_t = _s5_perm(nsteps)
    fixed = lambda k, r: (0, 0)
    in_specs = [pl.BlockSpec((rblk, S5_KTILE), lambda k, r: (blk0 + r, k)),
                pl.BlockSpec((rblk, rblk), fixed),
                pl.BlockSpec((rblk, rblk), fixed),
                pl.BlockSpec((1, S5_KTILE, 2 * nc), lambda k, r: (k, 0, 0)),
                pl.BlockSpec((1, 2 * nc, S5_KTILE), lambda k, r: (k, 0, 0)),
                pl.BlockSpec((1, SUBLANES, nc), lambda k, r: (k, 0, 0)),
                pl.BlockSpec((1, S5_KTILE), lambda k, r: (0, k)),
                pl.BlockSpec((SUBLANES, nc), lambda k, r: (0, k)),
                pl.BlockSpec((SUBLANES, nc), lambda k, r: (0, k))]
    args = [h_all, perm, perm_t, bmat, cmat, lam, d_skip, s_re, s_im]
    aliases = {}
    if prev is not None:
        aliases[len(args)] = 0
        in_specs.append(pl.BlockSpec(memory_space=pl.ANY))
        args.append(prev)
    return pl.pallas_call(
        functools.partial(_s5_kernel, nsteps=nsteps, chain=chain),
        grid=(kt, rows // rblk),
        in_specs=in_specs,
        out_specs=[pl.BlockSpec((rblk, S5_KTILE), lambda k, r: (blk0 + r, k)),
                   pl.BlockSpec((SUBLANES, nc), lambda k, r: (0, k)),
                   pl.BlockSpec((SUBLANES, nc), lambda k, r: (0, k))],
        out_shape=[jax.ShapeDtypeStruct((t_rows, d), BF16),
                   jax.ShapeDtypeStruct(s_re.shape, F32),
                   jax.ShapeDtypeStruct(s_im.shape, F32)],
        scratch_shapes=[pltpu.VMEM((rblk, 2 * nc), F32), pltpu.VMEM((2, SUBLANES, nc), F32)],
        input_output_aliases=aliases,
        compiler_params=_cparams(2),
        name="s5_scan",
    )(*args)


def _mm_kernel(*refs, mode, lrep, n_w):
    lhs_ref = refs[0]
    w_refs = refs[1:1 + n_w]
    pos = 1 + n_w
    i = pl.program_id(1)
    wbf = refs[len(refs) - n_w:]
    o_ref = refs[len(refs) - n_w - 1]

    @pl.when(i == 0)
    def _():
        for w_ref, s_ref in zip(w_refs, wbf):
            s_ref[...] = w_ref[...].astype(BF16)

    lhs = lhs_ref[...]
    if mode == "glu":
        b_ref, zt_ref, x_ref, g_ref = refs[pos:pos + 4]
        t = _dot(lhs, wbf[0][...]) + b_ref[...]
        o = zt_ref[...].astype(F32) * _sigmoid(t)
        o_ref[...] = x_ref[...] + _expand_rows(g_ref[...], lrep) * o
    elif mode == "res":
        x_ref, g_ref = refs[pos:pos + 2]
        o_ref[...] = x_ref[...] + _expand_rows(g_ref[...], lrep) * _dot(lhs, wbf[0][...])
    elif mode == "plain":
        o_ref[...] = _dot(lhs, wbf[0][...]).astype(o_ref.dtype)
    elif mode == "swiglu":
        o_ref[...] = (_silu(_dot(lhs, wbf[0][...])) * _dot(lhs, wbf[1][...])).astype(o_ref.dtype)


def _mm_call(mode, reg, t_rows, lhs, w_list, w_colblk0, n_out, tn, out_dtype, extras, prev, name):
    k_dim = lhs.shape[1]
    nj = n_out // tn
    in_specs = [pl.BlockSpec((reg.tile, k_dim), lambda j, i: (reg.blk0 + i, 0))]
    args = [lhs]
    for w, c0 in zip(w_list, w_colblk0):
        in_specs.append(pl.BlockSpec((k_dim, tn), lambda j, i, c0=c0: (0, c0 + j)))
        args.append(w)
    for a, kind in extras:
        if kind == "col":
            in_specs.append(pl.BlockSpec((1, tn), lambda j, i: (0, j)))
        elif kind == "tile":
            in_specs.append(pl.BlockSpec((reg.tile, tn), lambda j, i: (reg.blk0 + i, j)))
        elif kind == "own":
            in_specs.append(pl.BlockSpec((reg.tile, tn), lambda j, i: (i, j)))
        else:
            in_specs.append(pl.BlockSpec((reg.nseq, tn), lambda j, i: (0, j)))
        args.append(a)
    aliases = {}
    if prev is not None:
        aliases[len(args)] = 0
        in_specs.append(pl.BlockSpec(memory_space=pl.ANY))
        args.append(prev)
    return pl.pallas_call(
        functools.partial(_mm_kernel, mode=mode, lrep=reg.lrep, n_w=len(w_list)),
        grid=(nj, reg.ntiles),
        in_specs=in_specs,
        out_specs=pl.BlockSpec((reg.tile, tn), lambda j, i: (reg.blk0 + i, j)),
        out_shape=jax.ShapeDtypeStruct((t_rows, n_out), out_dtype),
        scratch_shapes=[pltpu.VMEM((k_dim, tn), BF16) for _ in w_list],
        input_output_aliases=aliases,
        compiler_params=_cparams(2),
        name=name,
    )(*args)


def _gla_consts(chunk):
    nlev = int(np.log2(chunk))
    assert 1 << nlev == chunk
    tri = np.tril(np.ones((chunk, chunk), np.float32))
    r = np.arange(chunk)
    wall, masks = [tri], []
    for l in range(nlev):
        w = chunk >> (l + 1)
        blk = r // (2 * w)
        second = (r & w) != 0
        wall.append(tri[blk * 2 * w + w - 1])
        masks.append(((blk[:, None] == blk[None, :]) & second[:, None] & (~second)[None, :]).astype(np.float32))
    masks.append(np.eye(chunk, dtype=np.float32))
    return jnp.asarray(np.concatenate(wall, 0), BF16), jnp.asarray(np.stack(masks, 0), F32)


def _gla_kernel(q_ref, k_ref, v_ref, gate_ref, lg_ref, s0_ref, wall_ref, mask_ref, gn_ref,
                *rest, chunk, nheads, dk, dv):
    o_ref, sout_ref, s_ref = rest[-3:]
    c = pl.program_id(1)
    nlev = mask_ref.shape[0] - 1

    @pl.when(c == 0)
    def _():
        s_ref[...] = s0_ref[0]

    lg = lg_ref[...]
    p1, p2, p3 = _split3(lg)
    wall = wall_ref[...]
    bg = _dot(wall, p1) + _dot(wall, p2) + _dot(wall, p3)
    b = bg[0:chunk]
    q = q_ref[...] * (dk ** -0.5)
    k = k_ref[...]
    row = lax.broadcasted_iota(I32, q.shape, 0)
    qs, ks = [], []
    for l in range(nlev):
        w = chunk >> (l + 1)
        g = bg[(l + 1) * chunk:(l + 2) * chunk]
        second = (row & w) != 0
        e = jnp.exp(jnp.where(second, b - g, g - b))
        qk = jnp.where(second, q, k) * e
        qs.append(jnp.where(second, qk, 0.0).astype(BF16))
        ks.append(jnp.where(second, 0.0, qk).astype(BF16))
    qb = q.astype(BF16)
    kb = k.astype(BF16)
    q_in = (q * jnp.exp(b)).astype(BF16)
    k_dec = (k * jnp.exp(b[chunk - 1:chunk] - b)).astype(BF16)
    ones = jnp.ones((chunk, LANES), BF16)
    dcol_all = jnp.exp(_dot_t0(p1, ones) + _dot_t0(p2, ones) + _dot_t0(p3, ones))
    gn = gn_ref[...]
    for h in range(nheads):
        ck = slice(h * dk, (h + 1) * dk)
        cv = slice(h * dv, (h + 1) * dv)
        att = _dot_t1(qb[:, ck], kb[:, ck]) * mask_ref[nlev]
        for l in range(nlev):
            att = att + _dot_t1(qs[l][:, ck], ks[l][:, ck]) * mask_ref[l]
        vh = v_ref[:, cv].astype(BF16)
        s_h = s_ref[h]
        o = _dot(att.astype(BF16), vh) + _dot(q_in[:, ck], s_h.astype(BF16))
        dcol = dcol_all[h * dk:(h + 1) * dk, :]
        s_ref[h] = jnp.concatenate([dcol] * (dv // LANES), axis=1) * s_h + _dot_t0(k_dec[:, ck], vh)
        ms = jnp.mean(o * o, axis=-1, keepdims=True)
        on = o * lax.rsqrt(ms + EPS) * gn
        o_ref[:, cv] = (on * _silu(gate_ref[:, cv])).astype(o_ref.dtype)

    @pl.when(c == pl.num_programs(1) - 1)
    def _():
        sout_ref[0] = s_ref[...]


def _gla_call(reg_row0, nseq, seqlen, chunk, t_rows, proj, lg, s0, g_norm, prev):
    nheads, dk, dv = s0.shape[1:]
    hk = nheads * dk
    d = nheads * dv
    nch = seqlen // chunk
    rb0 = reg_row0 // chunk
    wall, masks = _gla_consts(chunk)
    rowblk = lambda b, c: rb0 + b * nch + c
    in_specs = [pl.BlockSpec((chunk, hk), lambda b, c: (rowblk(b, c), 0)),
                pl.BlockSpec((chunk, hk), lambda b, c: (rowblk(b, c), 1)),
                pl.BlockSpec((chunk, d), lambda b, c: (rowblk(b, c), 1)),
                pl.BlockSpec((chunk, d), lambda b, c: (rowblk(b, c), 2)),
                pl.BlockSpec((chunk, hk), lambda b, c: (rowblk(b, c), 0)),
                pl.BlockSpec((1, nheads, dk, dv), lambda b, c: (b, 0, 0, 0)),
                pl.BlockSpec(wall.shape, lambda b, c: (0, 0)),
                pl.BlockSpec(masks.shape, lambda b, c: (0, 0, 0)),
                pl.BlockSpec((1, dv), lambda b, c: (0, 0))]
    args = [proj, proj, proj, proj, lg, s0, wall, masks, g_norm]
    aliases = {}
    if prev is not None:
        aliases[len(args)] = 0
        in_specs.append(pl.BlockSpec(memory_space=pl.ANY))
        args.append(prev)
    return pl.pallas_call(
        functools.partial(_gla_kernel, chunk=chunk, nheads=nheads, dk=dk, dv=dv),
        grid=(nseq, nch),
        in_specs=in_specs,
        out_specs=[pl.BlockSpec((chunk, d), lambda b, c: (rowblk(b, c), 0)),
                   pl.BlockSpec((1, nheads, dk, dv), lambda b, c: (b, 0, 0, 0))],
        out_shape=[jax.ShapeDtypeStruct((t_rows, d), BF16),
                   jax.ShapeDtypeStruct(s0.shape, F32)],
        scratch_shapes=[pltpu.VMEM((nheads, dk, dv), F32)],
        input_output_aliases=aliases,
        compiler_params=_cparams(2),
        name="gla_chunk",
    )(*args)


def _dispatch_plan(idx, rank, counts, n_tokens):
    n_experts = counts.shape[0]
    rb = EXP_ROWBLK
    a = n_tokens * TOP_K
    blocks_e = (counts + rb - 1) // rb
    padded = blocks_e * rb
    pad_start = jnp.cumsum(padded) - padded
    hit = idx[:, :, None] == jnp.arange(n_experts, dtype=I32)[None, None, :]
    dest = jnp.sum(jnp.where(hit, pad_start[None, None, :], 0), axis=-1) + rank
    np_rows = (a + rb - 1) // rb * rb + rb * n_experts
    items_e = (blocks_e + EXP_NBLK - 1) // EXP_NBLK
    item_end = jnp.cumsum(items_e)
    item_start = item_end - items_e
    n_items = n_experts + (np_rows // rb) // EXP_NBLK
    ii = jnp.arange(n_items, dtype=I32)
    total = item_end[-1]
    e_of = jnp.minimum(jnp.sum(item_end[None, :] <= ii[:, None], axis=1), n_experts - 1).astype(I32)
    valid = ii < total
    local = ii - item_start[e_of]
    e_last = e_of[jnp.maximum(total - 1, 0)]
    ie = jnp.where(valid, e_of, e_last).astype(I32)
    rsb = jnp.where(valid, pad_start[e_of] // rb + local * EXP_NBLK, 0).astype(I32)
    nrb = jnp.where(valid, jnp.minimum(EXP_NBLK, blocks_e[e_of] - local * EXP_NBLK), 0).astype(I32)
    return ie, rsb, nrb, pad_start.astype(I32), dest.astype(I32), np_rows, n_items


def _dispatch_kernel(cnt_ref, pst_ref, dest_hbm, h_ref, xs_hbm, idx_ref, hbuf, zrow, sem_i, sem_d, sem_z,
                     *, ntiles, n_experts):
    i = pl.program_id(0)
    slot = i % 2
    tile = h_ref.shape[0]
    nidx = tile * TOP_K
    rb = EXP_ROWBLK

    def idx_copy(t, s):
        return pltpu.make_async_copy(dest_hbm.at[pl.ds(pl.multiple_of(t * nidx, nidx), nidx)],
                                     idx_ref.at[pl.ds(pl.multiple_of(s * nidx, nidx), nidx)], sem_i.at[s])

    def wait_rows(s):
        for _ in range(TOP_K):
            pltpu.make_async_copy(hbuf.at[s], xs_hbm.at[pl.ds(0, tile)], sem_d.at[s]).wait()

    @pl.when(i == 0)
    def _():
        idx_copy(0, 0).start()

    idx_copy(i, slot).wait()

    @pl.when(i + 1 < ntiles)
    def _():
        idx_copy(jnp.minimum(i + 1, ntiles - 1), 1 - slot).start()

    hbuf[slot] = h_ref[...]

    def tok(t, carry):
        base = slot * nidx + t * TOP_K
        for j in range(TOP_K):
            p = idx_ref[base + j]
            pltpu.make_async_copy(hbuf.at[slot, pl.ds(t, 1)], xs_hbm.at[pl.ds(p, 1)], sem_d.at[slot]).start()
        return carry
    lax.fori_loop(0, tile, tok, 0, unroll=2)

    @pl.when(i > 0)
    def _():
        wait_rows(1 - slot)

    @pl.when(i == ntiles - 1)
    def _():
        wait_rows(slot)
        zrow[...] = jnp.zeros(zrow.shape, zrow.dtype)

        def expert(e, carry):
            cnt = cnt_ref[e]
            base = pst_ref[e]
            end = (cnt + rb - 1) // rb * rb

            def zstart(r, c2):
                pltpu.make_async_copy(zrow.at[pl.ds(0, 1)], xs_hbm.at[pl.ds(base + r, 1)], sem_z).start()
                return c2

            def zwait(r, c2):
                pltpu.make_async_copy(zrow.at[pl.ds(0, 1)], xs_hbm.at[pl.ds(0, 1)], sem_z).wait()
                return c2
            lax.fori_loop(cnt, end, zstart, 0)
            lax.fori_loop(cnt, end, zwait, 0)
            return carry
        lax.fori_loop(0, n_experts, expert, 0)


def _dispatch_call(counts, pad_start, dest_flat, h_packed, np_rows, tile):
    t_rows, dh = h_packed.shape
    ntiles = t_rows // tile
    n_experts = counts.shape[0]
    grid_spec = pltpu.PrefetchScalarGridSpec(
        num_scalar_prefetch=2,
        grid=(ntiles,),
        in_specs=[pl.BlockSpec(memory_space=pl.ANY),
                  pl.BlockSpec((tile, dh), lambda i, c, p: (i, 0))],
        out_specs=pl.BlockSpec(memory_space=pl.ANY),
        scratch_shapes=[pltpu.SMEM((2 * tile * TOP_K,), I32),
                        pltpu.VMEM((2, tile, dh), jnp.uint32),
                        pltpu.VMEM((SUBLANES, dh), jnp.uint32),
                        pltpu.SemaphoreType.DMA((2,)),
                        pltpu.SemaphoreType.DMA((2,)),
                        pltpu.SemaphoreType.DMA])
    return pl.pallas_call(
        functools.partial(_dispatch_kernel, ntiles=ntiles, n_experts=n_experts),
        grid_spec=grid_spec,
        out_shape=jax.ShapeDtypeStruct((np_rows, dh), jnp.uint32),
        compiler_params=_cparams(1),
        name="moe_dispatch",
    )(counts, pad_start, dest_flat, h_packed)


def _experts_kernel(ie_ref, rsb_ref, nrb_ref, xs_hbm, wi_ref, wo_ref, y_hbm,
                    xt, acc, act, ybuf, wibf, wobf, sem_x, sem_y, ycnt_ref, *, n_items, n_k, n_b):
    i = pl.program_id(0)
    st = pl.program_id(1)
    nrb = nrb_ref[i]
    rb = EXP_ROWBLK
    tkw = xt.shape[2]
    tdw = ybuf.shape[2]
    f = act.shape[1]

    def for_chunks(nblocks, fn):
        b0 = 0
        for size in EXP_CHUNKS:
            cnt = (nblocks - b0) // size

            def body(ci, carry, size=size, b0=b0):
                fn(b0 + ci * size, size)
                return carry
            lax.fori_loop(0, cnt, body, 0)
            b0 = b0 + cnt * size

    def rows_of(b0, nblk):
        return pl.ds(pl.multiple_of(b0 * rb, rb), nblk * rb)

    def x_copy(item, kt, b0, nblk):
        src = pl.multiple_of((rsb_ref[item] + b0) * rb, rb)
        return pltpu.make_async_copy(
            xs_hbm.at[pl.ds(src, nblk * rb), pl.ds(pl.multiple_of(kt * tkw, tkw), tkw)],
            xt.at[kt % 2, rows_of(b0, nblk)], sem_x.at[kt % 2])

    def start_x(item, kt):
        for_chunks(nrb_ref[item], lambda b0, nblk: x_copy(item, kt, b0, nblk).start())

    def wait_x(item, kt):
        for_chunks(nrb_ref[item], lambda b0, nblk: x_copy(item, kt, b0, nblk).wait())

    def wait_out(s):
        for_chunks(ycnt_ref[s], lambda b0, nblk: pltpu.make_async_copy(
            ybuf.at[s, pl.ds(0, nblk * rb)], y_hbm.at[pl.ds(0, nblk * rb), pl.ds(0, tdw)], sem_y.at[s]).wait())
        ycnt_ref[s] = 0

    @pl.when((i == 0) & (st == 0))
    def _():
        ycnt_ref[0] = 0
        ycnt_ref[1] = 0
        xt[...] = jnp.zeros(xt.shape, xt.dtype)
        start_x(0, 0)

    @pl.when(st < n_k)
    def _():
        wait_x(i, st)

    @pl.when(st + 1 < n_k)
    def _():
        start_x(i, st + 1)

    @pl.when((st == n_k) & (i + 1 < n_items))
    def _():
        start_x(jnp.minimum(i + 1, n_items - 1), 0)

    @pl.when((st < n_k) & (nrb > 0))
    def _():
        wibf[...] = wi_ref[0, 0].astype(BF16)
        slot = st % 2

        def partial_sum(rows):
            return _dot(_unpack_pairs(xt[slot, rows, :], BF16), wibf[...])

        if n_k > 1:
            @pl.when(st == 0)
            def _():
                def first(b0, nblk):
                    rows = rows_of(b0, nblk)
                    acc[rows, :] = partial_sum(rows)
                for_chunks(nrb, first)

            @pl.when((st > 0) & (st < n_k - 1))
            def _():
                def middle(b0, nblk):
                    rows = rows_of(b0, nblk)
                    acc[rows, :] = acc[rows, :] + partial_sum(rows)
                for_chunks(nrb, middle)

        @pl.when(st == n_k - 1)
        def _():
            def last(b0, nblk):
                rows = rows_of(b0, nblk)
                a = partial_sum(rows)
                if n_k > 1:
                    a = a + acc[rows, :]
                act[rows, :] = (_silu(a[:, :f]) * a[:, f:]).astype(BF16)
            for_chunks(nrb, last)

    @pl.when((st >= n_k) & (nrb > 0))
    def _():
        dj = st - n_k
        ys = (i * n_b + dj) % 2
        wobf[...] = wo_ref[0, 0].astype(BF16)
        wait_out(ys)
        row0 = rsb_ref[i] * rb
        col0 = pl.multiple_of(dj * tdw, tdw)

        def chunk(b0, nblk):
            rows = rows_of(b0, nblk)
            ybuf[ys, rows, :] = _pack_pairs(_dot(act[rows, :], wobf[...]))
            pltpu.make_async_copy(
                ybuf.at[ys, rows],
                y_hbm.at[pl.ds(pl.multiple_of(row0 + b0 * rb, rb), nblk * rb), pl.ds(col0, tdw)],
                sem_y.at[ys]).start()
        for_chunks(nrb, chunk)
        ycnt_ref[ys] = nrb

    @pl.when((i == n_items - 1) & (st == n_k + n_b - 1))
    def _():
        wait_out(0)
        wait_out(1)


def _experts_call(layer, ie, rsb, nrb, n_items, xs, w_in, w_out):
    _, _, d, f2 = w_in.shape
    f = f2 // 2
    np_rows, dh = xs.shape
    tk = min(EXP_TK, d)
    td = min(EXP_TD, d)
    n_k, n_b = d // tk, d // td
    rmax = EXP_NBLK * EXP_ROWBLK

    def kt(st, nr):
        return jnp.where((nr > 0) & (st < n_k), st, n_k - 1)

    def dj(st, nr):
        return jnp.where(nr > 0, jnp.maximum(st - n_k, 0), n_b - 1)

    grid_spec = pltpu.PrefetchScalarGridSpec(
        num_scalar_prefetch=3,
        grid=(n_items, n_k + n_b),
        in_specs=[pl.BlockSpec(memory_space=pl.ANY),
                  pl.BlockSpec((1, 1, tk, f2), lambda i, st, ie, rsb, nrb: (layer, ie[i], kt(st, nrb[i]), 0)),
                  pl.BlockSpec((1, 1, f, td), lambda i, st, ie, rsb, nrb: (layer, ie[i], 0, dj(st, nrb[i])))],
        out_specs=pl.BlockSpec(memory_space=pl.ANY),
        scratch_shapes=[pltpu.VMEM((2, rmax, tk // 2), jnp.uint32),
                        pltpu.VMEM((rmax, f2), F32),
                        pltpu.VMEM((rmax, f), BF16),
                        pltpu.VMEM((2, rmax, td // 2), jnp.uint32),
                        pltpu.VMEM((tk, f2), BF16),
                        pltpu.VMEM((f, td), BF16),
                        pltpu.SemaphoreType.DMA((2,)),
                        pltpu.SemaphoreType.DMA((2,)),
                        pltpu.SMEM((2,), I32)])
    return pl.pallas_call(
        functools.partial(_experts_kernel, n_items=n_items, n_k=n_k, n_b=n_b),
        grid_spec=grid_spec,
        out_shape=jax.ShapeDtypeStruct((np_rows, dh), jnp.uint32),
        compiler_params=_cparams(2),
        name="moe_experts",
    )(ie, rsb, nrb, xs, w_in, w_out)


def _combine_kernel(pos_hbm, w_ref, sh_ref, x_ref, g_ref, fg_ref, y_hbm, *rest,
                    lrep, blk0, ntiles, final):
    o_ref, idx_ref, gbuf, sem_i, sem_g = rest[-5:]
    i = pl.program_id(0)
    slot = i % 2
    nslot = 1 - slot
    tile = x_ref.shape[0]
    nidx = tile * TOP_K

    def idx_copy(t, s):
        return pltpu.make_async_copy(pos_hbm.at[pl.ds(pl.multiple_of((blk0 + t) * nidx, nidx), nidx)],
                                     idx_ref.at[pl.ds(pl.multiple_of(s * nidx, nidx), nidx)], sem_i.at[s])

    def issue_gather(s):
        def tok(t, carry):
            base = s * nidx + t * TOP_K
            for j in range(TOP_K):
                p = idx_ref[base + j]
                pltpu.make_async_copy(y_hbm.at[pl.ds(p, 1)], gbuf.at[s, j, pl.ds(t, 1)], sem_g.at[s]).start()
            return carry
        lax.fori_loop(0, tile, tok, 0, unroll=2)

    @pl.when(i == 0)
    def _():
        first = idx_copy(0, 0)
        first.start()
        first.wait()
        issue_gather(0)
        if ntiles > 1:
            idx_copy(1, 1).start()

    @pl.when(i + 1 < ntiles)
    def _():
        idx_copy(jnp.minimum(i + 1, ntiles - 1), nslot).wait()

    for j in range(TOP_K):
        pltpu.make_async_copy(y_hbm.at[pl.ds(0, tile)], gbuf.at[slot, j], sem_g.at[slot]).wait()

    @pl.when(i + 1 < ntiles)
    def _():
        issue_gather(nslot)

    @pl.when(i + 2 < ntiles)
    def _():
        idx_copy(jnp.minimum(i + 2, ntiles - 1), slot).start()

    acc = sh_ref[...]
    for j in range(TOP_K):
        acc = acc + w_ref[:, j:j + 1] * _unpack_pairs(gbuf[slot, j], F32)
    o = x_ref[...] + _expand_rows(g_ref[...], lrep, tile, i * tile) * acc
    if final:
        ms = jnp.mean(o * o, axis=-1, keepdims=True)
        o = o * lax.rsqrt(ms + EPS) * fg_ref[...]
    o_ref[...] = o


def _combine_call(reg, t_rows, pos_flat, wts, shared, x_all, g, final_g, y_sorted, final, prev):
    d = x_all.shape[1]
    row = lambda i: (reg.blk0 + i, 0)
    fixed = lambda i: (0, 0)
    in_specs = [pl.BlockSpec(memory_space=pl.ANY),
                pl.BlockSpec((reg.tile, TOP_K), row),
                pl.BlockSpec((reg.tile, d), row),
                pl.BlockSpec((reg.tile, d), row),
                pl.BlockSpec((reg.nseq, d), fixed),
                pl.BlockSpec((1, d), fixed),
                pl.BlockSpec(memory_space=pl.ANY)]
    args = [pos_flat, wts, shared, x_all, g, final_g, y_sorted]
    aliases = {}
    if prev is not None:
        aliases[len(args)] = 0
        in_specs.append(pl.BlockSpec(memory_space=pl.ANY))
        args.append(prev)
    if final:
        out_spec = pl.BlockSpec((reg.tile, d), lambda i: (i, 0))
        out_rows = reg.rows
    else:
        out_spec = pl.BlockSpec((reg.tile, d), row)
        out_rows = t_rows
    return pl.pallas_call(
        functools.partial(_combine_kernel, lrep=reg.lrep, blk0=reg.blk0, ntiles=reg.ntiles, final=final),
        grid=(reg.ntiles,),
        in_specs=in_specs,
        out_specs=out_spec,
        out_shape=jax.ShapeDtypeStruct((out_rows, d), F32),
        scratch_shapes=[pltpu.SMEM((2 * reg.tile * TOP_K,), I32),
                        pltpu.VMEM((2, TOP_K, reg.tile, d // 2), jnp.uint32),
                        pltpu.SemaphoreType.DMA((2,)),
                        pltpu.SemaphoreType.DMA((2,))],
        input_output_aliases=aliases,
        compiler_params=_cparams(1),
        name="moe_combine",
    )(*args)


def _both(fn, regs):
    out = fn(regs[0], None)
    return fn(regs[1], out)


def kernel(x_prompt, x_sample, state_s5_re, state_s5_im, state_gla, c_prompt, c_sample, w_ada, b_ada, norm_g, s5_lam_re, s5_lam_im, s5_log_dt, s5_b_re, s5_b_im, s5_c_re, s5_c_im, s5_d, s5_w_glu, s5_b_glu, gla_w_in, gla_w_a1, gla_w_a2, gla_b_a, gla_g_norm, gla_w_o, moe_w_router, moe_b_router, moe_w_in, moe_w_out, moe_ws_in, moe_ws_out, final_g):
    bp, lp, d = x_prompt.shape
    bs, ls, _ = x_sample.shape
    assert bp == 1 and bs == SUBLANES and ls % SUBLANES == 0
    depth = w_ada.shape[0]
    n_exp = moe_w_in.shape[1]
    f_sh = moe_ws_out.shape[1]
    srows = bs * ls
    t_rows = lp + srows
    s5_steps = 64
    s5_blk = s5_steps * SUBLANES
    gla_chunk = 64
    assert lp % PROMPT_TILE == 0 and lp % s5_blk == 0 and lp % srows == 0

    mm_tile = MM_TILE if lp % MM_TILE == 0 else PROMPT_TILE
    regs_mm = (_Region(0, lp, mm_tile, 1, mm_tile), _Region(lp, srows, srows, bs, ls))
    regs_nm = (_Region(0, lp, PROMPT_TILE // 2, 1, PROMPT_TILE // 2), _Region(lp, srows, srows, bs, ls))
    regs_cb = (_Region(0, lp, COMB_TILE, 1, COMB_TILE), _Region(lp, srows, COMB_TILE, bs, ls))

    x_parts = (x_prompt.reshape(lp, d), x_sample.reshape(srows, d))
    x_all = None

    def x_of(r):
        return x_parts[r.nseq > 1] if x_parts is not None else x_all

    c_all = jnp.concatenate([c_prompt, c_sample, jnp.zeros((2 * SUBLANES - 1 - bs, d), F32)], axis=0)
    mod = _ada_call(c_all, w_ada, b_ada)

    def mods(layer, k):
        m = mod[layer, :, k * d:(k + 1) * d]
        return (m[0:1], m[1:1 + bs])

    new_re_p, new_im_p, new_gla_p, new_re_s, new_im_s, new_gla_s = [], [], [], [], [], []
    for i in range(depth):
        sh1, sc1, g1, sh2, sc2, g2 = [mods(i, k) for k in range(6)]
        j = i // 2
        ng1 = norm_g[i, 0].reshape(1, d)
        ng2 = norm_g[i, 1].reshape(1, d)
        if i % 2 == 0:
            own = x_parts is not None
            (h_all,) = _both(lambda r, prev: _norm_call(
                "plain", r, t_rows, x_of(r), ng1, sc1[r.nseq > 1], sh1[r.nseq > 1], [], [(d, F32)], prev,
                x_own=own), regs_nm)
            grp, pst = s5_lam_re.shape[1:]
            ncol = grp * pst
            lam, bbr, bbi = _s5_disc_call(s5_lam_re[j], s5_lam_im[j], s5_log_dt[j], s5_b_re[j], s5_b_im[j], s5_steps)
            bmat, cmat = _s5_block_weights(bbr, bbi, s5_c_re[j], s5_c_im[j])
            kt = bmat.shape[0]
            lamt = jnp.transpose(lam.reshape(4, kt, ncol // kt), (1, 0, 2))
            lamt = jnp.concatenate([lamt, jnp.zeros_like(lamt)], axis=1)
            dsk = s5_d[j].reshape(1, d)
            zeros_st = jnp.zeros((SUBLANES, ncol), F32)
            z_all, pre, pim = _s5_call(0, lp, t_rows, h_all, bmat, cmat, lamt, dsk, zeros_st, zeros_st,
                                       s5_steps, True, None)
            z_all, sre, sim = _s5_call(lp, srows, t_rows, h_all, bmat, cmat, lamt, dsk,
                                       state_s5_re[j].reshape(bs, ncol), state_s5_im[j].reshape(bs, ncol),
                                       ls, False, z_all)
            new_re_p.append(pre[0].reshape(1, grp, pst))
            new_im_p.append(pim[0].reshape(1, grp, pst))
            new_re_s.append(sre.reshape(bs, grp, pst))
            new_im_s.append(sim.reshape(bs, grp, pst))
            bglu = s5_b_glu[j].reshape(1, d)
            x_all = _both(lambda r, prev: _mm_call(
                "glu", r, t_rows, z_all, [s5_w_glu[j]], [0], d, 512, F32,
                [(bglu, "col"), (z_all, "tile"), (x_of(r), "own" if own else "tile"), (g1[r.nseq > 1], "seq")],
                prev, "s5_glu"), regs_mm)
            x_parts = None
        else:
            hk = gla_w_a2.shape[2]
            assert 2 * hk == d
            own = x_parts is not None
            h_bf, lg = _both(lambda r, prev: _norm_call(
                "gate", r, t_rows, x_of(r), ng1, sc1[r.nseq > 1], sh1[r.nseq > 1],
                [gla_w_a1[j], gla_w_a2[j], gla_b_a[j].reshape(1, hk)], [(d, BF16), (hk, F32)], prev,
                x_own=own), regs_nm)
            proj = _both(lambda r, prev: _mm_call(
                "plain", r, t_rows, h_bf, [gla_w_in[j]], [0], 3 * d, 512, F32, [], prev, "gla_proj"), regs_mm)
            gn = gla_g_norm[j].reshape(1, -1)
            s0p = jnp.zeros((1,) + state_gla.shape[2:], F32)
            o_all, gla_p = _gla_call(0, 1, lp, gla_chunk, t_rows, proj, lg, s0p, gn, None)
            o_all, gla_s = _gla_call(lp, bs, ls, ls, t_rows, proj, lg, state_gla[j], gn, o_all)
            new_gla_p.append(gla_p)
            new_gla_s.append(gla_s)
            x_all = _both(lambda r, prev: _mm_call(
                "res", r, t_rows, o_all, [gla_w_o[j]], [0], d, 512, F32,
                [(x_of(r), "own" if own else "tile"), (g1[r.nseq > 1], "seq")], prev, "gla_out"), regs_mm)
            x_parts = None

        brt = moe_b_router[i].reshape(1, n_exp)
        r_p = _router_call(regs_nm[0], t_rows, x_all, ng2, sc2[0], sh2[0], moe_w_router[i], brt,
                           jnp.zeros((SUBLANES, n_exp), F32), None)
        h_pk, h_bf, idx, wts, rank, cnt = _router_call(
            regs_nm[1], t_rows, x_all, ng2, sc2[1], sh2[1], moe_w_router[i], brt, r_p[5], r_p[:5])
        counts = cnt[0].astype(I32)
        ie, rsb, nrb, pad_start, dest, np_rows, n_items = _dispatch_plan(idx, rank, counts, t_rows)
        pos_flat = dest.reshape(-1)
        xs = _dispatch_call(counts, pad_start, pos_flat, h_pk, np_rows, DISP_TILE)
        y_sorted = _experts_call(i, ie, rsb, nrb, n_items, xs, moe_w_in, moe_w_out)
        act = _both(lambda r, prev: _mm_call(
            "swiglu", r, t_rows, h_bf, [moe_ws_in[i], moe_ws_in[i]], [0, f_sh // 256], f_sh, 256, BF16,
            [], prev, "shared_in"), regs_mm)
        shared = _both(lambda r, prev: _mm_call(
            "plain", r, t_rows, act, [moe_ws_out[i]], [0], d, 512, F32, [], prev, "shared_out"), regs_mm)
        last = i == depth - 1
        fg = final_g.reshape(1, d)
        if last:
            y_prompt, y_sample = [_combine_call(r, t_rows, pos_flat, wts, shared, x_all, g2[r.nseq > 1], fg,
                                                y_sorted, True, None) for r in regs_cb]
        else:
            x_all = _both(lambda r, prev: _combine_call(
                r, t_rows, pos_flat, wts, shared, x_all, g2[r.nseq > 1], fg, y_sorted, False, prev), regs_cb)

    y_prompt = y_prompt.reshape(bp, lp, d)
    y_sample = y_sample.reshape(bs, ls, d)
    return (y_prompt, y_sample, jnp.stack(new_re_p), jnp.stack(new_im_p), jnp.stack(new_gla_p),
            jnp.stack(new_re_s), jnp.stack(new_im_s), jnp.stack(new_gla_s))
```
